```python
import math
import jax
import jax.numpy as jnp
from jax import lax
import numpy as np

D_MODEL = 1024
BATCH = 4
SEQ = 4096
DEPTH = 2

GRID_W = 64
CTX_LEN = 256
N_GROUPS = 4
GROUP_W = D_MODEL // N_GROUPS
MIX_W = GROUP_W * N_GROUPS

DIFF_HEADS = 4
DIFF_DV = GROUP_W // DIFF_HEADS
DIFF_DQK = DIFF_DV // 2
Q_BLOCK = 128
ROPE_BASE = 10000.0

HY_CH = GROUP_W
HY_BANDS = 16
HY_EMB = 2 * HY_BANDS + 1
HY_FFN = 64
HY_FAST_DECAY_PCT = 0.3
HY_SLOW_DECAY_PCT = 1.5
HY_DECAY_TARGET = 1e-2

RET_HEADS = 4
RET_DH = GROUP_W // RET_HEADS

GLA_HEADS = 4
GLA_DK = GROUP_W // 2 // GLA_HEADS
GLA_DV = GROUP_W // GLA_HEADS
GLA_RANK = 16
GLA_GATE_NORM = 16.0

CHUNK = 64

N_EXPERTS = 32
TOP_K = 4
D_EXPERT = D_MODEL
SWIGLU_LIMIT = 7.0
SWIGLU_ALPHA = 1.702
EPS = 1e-6

SPLIT_SIZES = (GROUP_W, GROUP_W, GROUP_W,
               3 * HY_CH,
               GROUP_W, GROUP_W, GROUP_W, GROUP_W,
               GLA_HEADS * GLA_DK, GLA_HEADS * GLA_DK,
               GROUP_W, GROUP_W, 2 * GLA_RANK)
IN_W = sum(SPLIT_SIZES)

kernel_name = 'hybrid_diff_hyena_ret_gla_moe_dit'

F32 = jnp.float32


def rms_norm(x, g):
    xf = x.astype(F32)
    return xf * lax.rsqrt(jnp.mean(xf * xf, axis=-1, keepdims=True) + EPS) * g.astype(F32)


def split_heads(t, n):
    b, l, _ = t.shape
    return t.reshape(b, l, n, -1).transpose(0, 2, 1, 3)


def merge_heads(t):
    b, h, l, d = t.shape
    return t.transpose(0, 2, 1, 3).reshape(b, l, h * d)


def axial_rope_tables(n_tok, dim):
    rows = n_tok // GRID_W
    row = jnp.repeat(jnp.arange(rows, dtype=F32), GRID_W)
    col = jnp.tile(jnp.arange(GRID_W, dtype=F32), rows)
    quarter = dim // 4
    inv = ROPE_BASE ** (-jnp.arange(quarter, dtype=F32) / quarter)
    ar = row[:, None] * inv[None]
    ac = col[:, None] * inv[None]
    ang = jnp.concatenate([ar, ar, ac, ac], axis=-1)
    return jnp.cos(ang), jnp.sin(ang)


def apply_rope(x, cos, sin):
    h = x.shape[-1] // 2
    q = h // 2
    xr, xc = x[..., :h], x[..., h:]
    rot = jnp.concatenate([-xr[..., q:], xr[..., :q], -xc[..., q:], xc[..., :q]], axis=-1)
    return x * cos + rot * sin


def _diff_attend(q, k, v, lam):
    s = jnp.einsum('bhmqd,bhmkd->bhmqk', q, k) * (DIFF_DQK ** -0.5)
    p = jax.nn.softmax(s, axis=-1)
    a = p[:, :, 0] - lam * p[:, :, 1]
    return jnp.einsum('bhqk,bhkv->bhqv', a, v)


def diff_attention(q, k, v, qc, kc, vc, lam_p, subln_g, lam_init, need_ctx):
    b, n_lat, _ = q.shape

    def qk_heads(t):
        return t.reshape(b, t.shape[1], DIFF_HEADS, 2, DIFF_DQK).transpose(0, 2, 3, 1, 4).astype(F32)

    cos, sin = axial_rope_tables(n_lat, DIFF_DQK)
    q = apply_rope(qk_heads(q), cos, sin)
    k = apply_rope(qk_heads(k), cos, sin)
    qc = qk_heads(qc)
    kc = qk_heads(kc)
    v = split_heads(v, DIFF_HEADS).astype(F32)
    vc = split_heads(vc, DIFF_HEADS).astype(F32)
    lam_p = lam_p.astype(F32)
    lam = jnp.exp(jnp.sum(lam_p[0] * lam_p[1])) - jnp.exp(jnp.sum(lam_p[2] * lam_p[3])) + lam_init
    k_all = jnp.concatenate([k, kc], axis=3)
    v_all = jnp.concatenate([v, vc], axis=2)
    nb = n_lat // Q_BLOCK
    q_blocks = q.reshape(b, DIFF_HEADS, 2, nb, Q_BLOCK, DIFF_DQK).transpose(3, 0, 1, 2, 4, 5)
    o = lax.map(lambda qb: _diff_attend(qb, k_all, v_all, lam), q_blocks)
    o = o.transpose(1, 2, 0, 3, 4).reshape(b, DIFF_HEADS, n_lat, DIFF_DV)

    def post(t):
        return merge_heads(rms_norm(t, subln_g) * (1.0 - lam_init))

    out_c = post(_diff_attend(qc, kc, vc, lam)) if need_ctx else None
    return post(o), out_c


def hyena_filter(n, w1, b1, w2, b2, w3, freq):
    pos = jnp.arange(n, dtype=F32)
    t = pos / (n - 1)
    w = 2.0 * math.pi * pos / n
    f = jnp.linspace(1e-4, HY_BANDS - 1, HY_BANDS, dtype=F32)
    ang = w[:, None] * f[None]
    z = jnp.concatenate([t[:, None], jnp.cos(ang), -jnp.sin(ang)], axis=-1)
    hdn = jnp.sin(freq[0] * (z @ w1 + b1))
    hdn = jnp.sin(freq[1] * (hdn @ w2 + b2))
    raw = (hdn @ w3).astype(F32)
    max_decay = math.log(HY_DECAY_TARGET) / HY_FAST_DECAY_PCT
    min_decay = math.log(HY_DECAY_TARGET) / HY_SLOW_DECAY_PCT
    deltas = jnp.abs(jnp.linspace(min_decay, max_decay, HY_CH, dtype=F32))
    window = jnp.exp(-t[:, None] * deltas[None])
    h_fwd = raw[:, :HY_CH] * window
    h_bwd = raw[:, HY_CH:] * window
    filt = jnp.concatenate([h_fwd, jnp.zeros((1, HY_CH), F32), h_bwd[:0:-1]], axis=0)
    return filt / jnp.sum(jnp.abs(filt), axis=0, keepdims=True)


def short_conv3(u, w, b):
    up = jnp.pad(u, ((0, 0), (1, 1), (0, 0)))
    return up[:, :-2] * w[0] + up[:, 1:-1] * w[1] + up[:, 2:] * w[2] + b


def hyena_seq(u, short_w, short_b, filt, bias):
    u = short_conv3(u.astype(F32), short_w.astype(F32), short_b.astype(F32))
    x0, x1, v = jnp.split(u, 3, axis=-1)
    z = x1 * v
    n = z.shape[1]
    zf = jnp.fft.rfft(z, n=2 * n, axis=1)
    ff = jnp.fft.rfft(filt, n=2 * n, axis=0)
    y = jnp.fft.irfft(zf * ff[None], n=2 * n, axis=1)[:, :n] + bias.astype(F32) * z
    return x0 * y


def chunk_recurrence(q, k, v, log_a, s0):
    b, h, n_tok, _ = q.shape
    dv = v.shape[-1]
    n_chunks = n_tok // CHUNK

    def to_chunks(t):
        return t.reshape(b, h, n_chunks, CHUNK, t.shape[-1]).transpose(2, 0, 1, 3, 4)

    lower = jnp.tril(jnp.ones((CHUNK, CHUNK), dtype=bool))[:, :, None]

    def step(state, blk):
        qb, kb, vb, lb = blk
        cum = jnp.cumsum(lb, axis=2)
        rel = cum[:, :, :, None, :] - cum[:, :, None, :, :]
        w = jnp.exp(jnp.where(lower, rel, -jnp.inf))
        scores = jnp.sum(qb[:, :, :, None, :] * kb[:, :, None, :, :] * w, axis=-1)
        o = (jnp.einsum('bhts,bhsv->bhtv', scores, vb)
             + jnp.einsum('bhtk,bhkv->bhtv', qb * jnp.exp(cum), state))
        cum_end = cum[:, :, -1:, :]
        k_dec = kb * jnp.exp(cum_end - cum)
        state = (jnp.swapaxes(jnp.exp(cum_end), -1, -2) * state
                 + jnp.einsum('bhsk,bhsv->bhkv', k_dec, vb))
        return state, o

    s_fin, o = lax.scan(step, s0, (to_chunks(q), to_chunks(k), to_chunks(v), to_chunks(log_a)))
    return o.transpose(1, 2, 0, 3, 4).reshape(b, h, n_tok, dv), s_fin


def _orient(t, d):
    return jnp.flip(t, axis=2) if d == 1 else t


def bidir_scan(q, k, v, log_a, qc, kc, vc, log_ac, need_ctx):
    b, h, _, dk = q.shape
    s0 = jnp.zeros((b, h, dk, v.shape[-1]), F32)
    outs, outs_c = [], []
    for d in range(2):
        oc, sc = chunk_recurrence(_orient(qc, d), _orient(kc, d), _orient(vc, d), _orient(log_ac[d], d), s0)
        o, _ = chunk_recurrence(_orient(q, d), _orient(k, d), _orient(v, d), _orient(log_a[d], d), sc)
        outs.append(_orient(o, d))
        outs_c.append(_orient(oc, d))
    out_c = outs_c[0] + outs_c[1] if need_ctx else None
    return outs[0] + outs[1], out_c


def retention(q, k, v, g, qc, kc, vc, gc, decay_p, norm_g, need_ctx):
    b, n_lat, _ = q.shape
    n_ctx = qc.shape[1]
    cos, sin = axial_rope_tables(n_lat, RET_DH)
    scale = RET_DH ** -0.5
    q = apply_rope(split_heads(q, RET_HEADS).astype(F32), cos, sin)
    k = apply_rope(split_heads(k, RET_HEADS).astype(F32), cos, sin) * scale
    v = split_heads(v, RET_HEADS).astype(F32)
    qc = split_heads(qc, RET_HEADS).astype(F32)
    kc = split_heads(kc, RET_HEADS).astype(F32) * scale
    vc = split_heads(vc, RET_HEADS).astype(F32)
    log_gamma = -jnp.exp(decay_p.astype(F32))
    la = [jnp.broadcast_to(log_gamma[d][None, :, None, None], (b, RET_HEADS, n_lat, 1)) for d in range(2)]
    lac = [jnp.broadcast_to(log_gamma[d][None, :, None, None], (b, RET_HEADS, n_ctx, 1)) for d in range(2)]
    o, oc = bidir_scan(q, k, v, la, qc, kc, vc, lac, need_ctx)
    g_norm = norm_g.reshape(RET_HEADS, 1, RET_DH)

    def post(o_, g_):
        return merge_heads(rms_norm(o_, g_norm)) * jax.nn.silu(g_.astype(F32))

    out_c = post(oc, gc) if need_ctx else None
    return post(o, g), out_c


def gla(q, k, v, r, glr, qc, kc, vc, rc, glrc, gate_w, gate_b, norm_g, need_ctx):
    def decays(t):
        b, n, _ = t.shape
        t = t.astype(F32).reshape(b, n, 2, GLA_RANK)
        logit = jnp.einsum('bndr,drk->dbnk', t, gate_w.astype(F32)) + gate_b.astype(F32)[:, None, None, :]
        la = jax.nn.log_sigmoid(logit) / GLA_GATE_NORM
        return [split_heads(la[d], GLA_HEADS) for d in range(2)]

    scale = GLA_DK ** -0.5
    qh = split_heads(q, GLA_HEADS).astype(F32) * scale
    kh = split_heads(k, GLA_HEADS).astype(F32)
    vh = split_heads(v, GLA_HEADS).astype(F32)
    qch = split_heads(qc, GLA_HEADS).astype(F32) * scale
    kch = split_heads(kc, GLA_HEADS).astype(F32)
    vch = split_heads(vc, GLA_HEADS).astype(F32)
    o, oc = bidir_scan(qh, kh, vh, decays(glr), qch, kch, vch, decays(glrc), need_ctx)

    def post(o_, r_):
        return merge_heads(rms_norm(o_, norm_g)) * jax.nn.silu(r_.astype(F32))

    out_c = post(oc, rc) if need_ctx else None
    return post(o, r), out_c


def token_mixer(h, hc, w_in, w_out, diff_lambda, diff_subln_g, lam_init,
                hy_short_w, hy_short_b, hy_w1, hy_b1, hy_w2, hy_b2, hy_w3, hy_freq, hy_bias,
                ret_decay, ret_norm_g, gla_gate_w, gla_gate_b, gla_norm_g, need_ctx):
    n_lat = h.shape[1]
    n_ctx = hc.shape[1]
    cuts = np.cumsum(SPLIT_SIZES)[:-1].tolist()
    p = jnp.split(h @ w_in, cuts, axis=-1)
    pc = jnp.split(hc @ w_in, cuts, axis=-1)
    a, a_c = diff_attention(p[0], p[1], p[2], pc[0], pc[1], pc[2], diff_lambda, diff_subln_g, lam_init, need_ctx)
    hy = hyena_seq(p[3], hy_short_w, hy_short_b,
                   hyena_filter(n_lat, hy_w1, hy_b1, hy_w2, hy_b2, hy_w3, hy_freq), hy_bias)
    rt, rt_c = retention(p[4], p[5], p[6], p[7], pc[4], pc[5], pc[6], pc[7], ret_decay, ret_norm_g, need_ctx)
    gl, gl_c = gla(p[8], p[9], p[10], p[11], p[12], pc[8], pc[9], pc[10], pc[11], pc[12],
                   gla_gate_w, gla_gate_b, gla_norm_g, need_ctx)
    y = jnp.concatenate([a, hy, rt, gl], axis=-1) @ w_out
    y_c = None
    if need_ctx:
        hy_c = hyena_seq(pc[3], hy_short_w, hy_short_b,
                         hyena_filter(n_ctx, hy_w1, hy_b1, hy_w2, hy_b2, hy_w3, hy_freq), hy_bias)
        y_c = jnp.concatenate([a_c, hy_c, rt_c, gl_c], axis=-1) @ w_out
    return y, y_c


def moe_ffn(h, router_w, router_b, w1, b1, w2, b2):
    logits = (h @ router_w + router_b).astype(F32)
    top_val, top_idx = lax.top_k(logits, TOP_K)
    top_w = jax.nn.softmax(top_val, axis=-1)
    combine = jnp.einsum('nk,nke->ne', top_w, jax.nn.one_hot(top_idx, N_EXPERTS, dtype=F32))

    def expert(acc, blk):
        w1e, b1e, w2e, b2e, ce = blk
        u = (h @ w1e + b1e).astype(F32)
        glu = jnp.minimum(u[:, ::2], SWIGLU_LIMIT)
        lin = jnp.clip(u[:, 1::2], -SWIGLU_LIMIT, SWIGLU_LIMIT)
        act = glu * jax.nn.sigmoid(SWIGLU_ALPHA * glu) * (lin + 1.0)
        return acc + ce[:, None] * (act @ w2e + b2e).astype(F32), None

    out, _ = lax.scan(expert, jnp.zeros(h.shape, F32), (w1, b1, w2, b2, combine.T))
    return out


def setup_inputs(seed: int = 0) -> dict:
    key = jax.random.key(seed)
    ks = jax.random.split(key, 32)
    counter = [0]

    def nrm(shape, scale):
        k = ks[counter[0]]
        counter[0] += 1
        return jax.random.normal(k, shape, F32) * scale

    ret_init = jnp.log(-jnp.log(1.0 - 2.0 ** (-5.0 - jnp.arange(RET_HEADS, dtype=F32))))
    return {
        'x': nrm((BATCH, SEQ, D_MODEL), 1.0),
        'c': nrm((BATCH, D_MODEL), 1.0),
        'ctx': nrm((BATCH, CTX_LEN, D_MODEL), 1.0),
        'c_ctx': nrm((D_MODEL,), 1.0),
        'w_mod': nrm((DEPTH, D_MODEL, 6 * D_MODEL), 0.5 * D_MODEL ** -0.5),
        'b_mod': nrm((DEPTH, 6 * D_MODEL), 0.02),
        'norm_g': 1.0 + nrm((DEPTH, 4, D_MODEL), 0.02),
        'w_in': nrm((DEPTH, D_MODEL, IN_W), D_MODEL ** -0.5),
        'w_out': nrm((DEPTH, MIX_W, D_MODEL), MIX_W ** -0.5),
        'diff_lambda': nrm((DEPTH, 4, DIFF_DQK), 0.1),
        'diff_subln_g': 1.0 + nrm((DEPTH, DIFF_DV), 0.02),
        'hy_short_w': nrm((DEPTH, 3, 3 * HY_CH), 3 ** -0.5),
        'hy_short_b': nrm((DEPTH, 3 * HY_CH), 0.02),
        'hy_w1': nrm((DEPTH, HY_EMB, HY_FFN), HY_EMB ** -0.5),
        'hy_b1': nrm((DEPTH, HY_FFN), 0.02),
        'hy_w2': nrm((DEPTH, HY_FFN, HY_FFN), HY_FFN ** -0.5),
        'hy_b2': nrm((DEPTH, HY_FFN), 0.02),
        'hy_w3': nrm((DEPTH, HY_FFN, 2 * HY_CH), HY_FFN ** -0.5),
        'hy_freq': 1.0 + nrm((DEPTH, 2, HY_FFN), 0.02),
        'hy_bias': nrm((DEPTH, HY_CH), 1.0),
        'ret_decay': ret_init[None, None, :] + nrm((DEPTH, 2, RET_HEADS), 0.05),
        'ret_norm_g': 1.0 + nrm((DEPTH, GROUP_W), 0.02),
        'gla_gate_w': nrm((DEPTH, 2, GLA_RANK, GLA_HEADS * GLA_DK), GLA_RANK ** -0.5),
        'gla_gate_b': nrm((DEPTH, 2, GLA_HEADS * GLA_DK), 0.1),
        'gla_norm_g': 1.0 + nrm((DEPTH, GLA_DV), 0.02),
        'router_w': nrm((DEPTH, D_MODEL, N_EXPERTS), D_MODEL ** -0.5),
        'router_b': nrm((DEPTH, N_EXPERTS), 0.01),
        'exp_w1': nrm((DEPTH, N_EXPERTS, D_MODEL, 2 * D_EXPERT), D_MODEL ** -0.5),
        'exp_b1': nrm((DEPTH, N_EXPERTS, 2 * D_EXPERT), 0.01),
        'exp_w2': nrm((DEPTH, N_EXPERTS, D_EXPERT, D_MODEL), D_EXPERT ** -0.5),
        'exp_b2': nrm((DEPTH, N_EXPERTS, D_MODEL), 0.01),
    }


def reference(x, c, ctx, c_ctx, w_mod, b_mod, norm_g, w_in, w_out, diff_lambda, diff_subln_g,
              hy_short_w, hy_short_b, hy_w1, hy_b1, hy_w2, hy_b2, hy_w3, hy_freq, hy_bias,
              ret_decay, ret_norm_g, gla_gate_w, gla_gate_b, gla_norm_g,
              router_w, router_b, exp_w1, exp_b1, exp_w2, exp_b2):
    b, n_lat, d = x.shape
    n_ctx = ctx.shape[1]
    xc = ctx
    for l in range(DEPTH):
        need_ctx = l < DEPTH - 1
        mod = (jax.nn.silu(c) @ w_mod[l] + b_mod[l]).astype(F32)
        mod_c = (jax.nn.silu(c_ctx) @ w_mod[l] + b_mod[l]).astype(F32)
        sh_a, sc_a, gt_a, sh_f, sc_f, gt_f = jnp.split(mod[:, None, :], 6, axis=-1)
        shc_a, scc_a, gtc_a, shc_f, scc_f, gtc_f = jnp.split(mod_c, 6)
        lam_init = 0.8 - 0.6 * math.exp(-0.3 * l)

        h = rms_norm(x, norm_g[l, 0]) * (1.0 + sc_a) + sh_a
        hc = rms_norm(xc, norm_g[l, 0]) * (1.0 + scc_a) + shc_a
        y, y_c = token_mixer(h, hc, w_in[l], w_out[l], diff_lambda[l], diff_subln_g[l], lam_init,
                             hy_short_w[l], hy_short_b[l], hy_w1[l], hy_b1[l], hy_w2[l], hy_b2[l],
                             hy_w3[l], hy_freq[l], hy_bias[l], ret_decay[l], ret_norm_g[l],
                             gla_gate_w[l], gla_gate_b[l], gla_norm_g[l], need_ctx)
        x = x + (gt_a * rms_norm(y, norm_g[l, 1])).astype(x.dtype)
        if need_ctx:
            xc = xc + (gtc_a * rms_norm(y_c, norm_g[l, 1])).astype(xc.dtype)

        h = (rms_norm(x, norm_g[l, 2]) * (1.0 + sc_f) + sh_f).reshape(b * n_lat, d)
        if need_ctx:
            hc = (rms_norm(xc, norm_g[l, 2]) * (1.0 + scc_f) + shc_f).reshape(b * n_ctx, d)
            h = jnp.concatenate([h, hc], axis=0)
        y_all = moe_ffn(h, router_w[l], router_b[l], exp_w1[l], exp_b1[l], exp_w2[l], exp_b2[l])
        y = y_all[:b * n_lat].reshape(b, n_lat, d)
        x = x + (gt_f * rms_norm(y, norm_g[l, 3])).astype(x.dtype)
        if need_ctx:
            y_c = y_all[b * n_lat:].reshape(b, n_ctx, d)
            xc = xc + (gtc_f * rms_norm(y_c, norm_g[l, 3])).astype(xc.dtype)
    return x
```

```python
import functools
import math

import numpy as np
import jax
import jax.numpy as jnp
from jax import lax
from jax.experimental import pallas as pl
from jax.experimental.pallas import tpu as pltpu

F32 = jnp.float32
BF16 = jnp.bfloat16
HIGHEST = lax.Precision.HIGHEST

D_MODEL = 1024
GRID_W = 64
GROUP_W = 256
N_HEADS = 4
DIFF_DQK = 32
DIFF_DV = 64
ROPE_BASE = 10000.0
HY_CH = 256
HY_BANDS = 16
HY_FFN = 64
HY_FAST_DECAY_PCT = 0.3
HY_SLOW_DECAY_PCT = 1.5
HY_DECAY_TARGET = 1e-2
RET_DH = 64
GLA_DK = 32
GLA_DV = 64
GLA_RANK = 16
GLA_GATE_NORM = 16.0
N_EXPERTS = 32
TOP_K = 4
D_EXPERT = 1024
SWIGLU_LIMIT = 7.0
SWIGLU_ALPHA = 1.702
EPS = 1e-6

LANES = 128
ROW_TILE = 256
REC_CHUNK = 64
REC_SUB = 16
EXP_CLAMP = 80.0
MOE_TILE = 256
VMEM_LIMIT = 52 * 1024 * 1024

_A_Q, _A_QR, _A_K, _A_KR, _A_V = 0, 256, 512, 768, 1024
_B_U = 1280
_C_Q, _C_QR, _C_K, _C_KR, _C_V, _C_G = 2048, 2304, 2560, 2816, 3072, 3328
_D_Q, _D_K, _D_V, _D_R, _D_L = 3584, 3712, 3840, 4096, 4352
_IN_COLS = 4480


def _cparams(sem):
    return pltpu.CompilerParams(dimension_semantics=sem, vmem_limit_bytes=VMEM_LIMIT)


def _sigmoid(x):
    return 1.0 / (1.0 + jnp.exp(-x))


def _rms(x):
    return x * lax.rsqrt(jnp.mean(x * x, axis=-1, keepdims=True) + EPS)


def _dot(a, b):
    return jnp.dot(a, b, preferred_element_type=F32)


def _dot_exact(a, b):
    return jnp.dot(a, b, preferred_element_type=F32, precision=HIGHEST)


def _dot_nt(a, b):
    return lax.dot_general(a, b, (((1,), (1,)), ((), ())), preferred_element_type=F32)


def _dot_tn(a, b):
    return lax.dot_general(a, b, (((0,), (0,)), ((), ())), preferred_element_type=F32)


def _mod_kernel(c_ref, w_ref, b_ref, o_ref):
    c = c_ref[...]
    o_ref[...] = _dot_exact(c * _sigmoid(c), w_ref[...]) + b_ref[...]


def _modulation(cc, w_mod, b_mod):
    depth, d, n = w_mod.shape
    tn = n // 4
    rows = cc.shape[0]
    return pl.pallas_call(
        _mod_kernel,
        grid=(depth, n // tn),
        in_specs=[pl.BlockSpec((rows, d), lambda l, j: (0, 0)),
                  pl.BlockSpec((None, d, tn), lambda l, j: (l, 0, j)),
                  pl.BlockSpec((None, 1, tn), lambda l, j: (l, 0, j))],
        out_specs=pl.BlockSpec((None, rows, tn), lambda l, j: (l, 0, j)),
        out_shape=jax.ShapeDtypeStruct((depth, rows, n), F32),
        compiler_params=_cparams(("arbitrary", "arbitrary")),
        name="modulation",
    )(cc, w_mod, b_mod.reshape(depth, 1, n))


def _inproj_kernel(x_ref, mod_ref, g_ref, w_ref, cosa_ref, sina_ref, cosc_ref, sinc_ref,
                   aq_ref, ak_ref, av_ref, bu_ref, cq_ref, ck_ref, cv_ref, cg_ref,
                   dq_ref, dk_ref, dv_ref, dr_ref, dl_ref):
    xn = _rms(x_ref[...]) * g_ref[...]
    h = (xn * (1.0 + mod_ref[1:2, :]) + mod_ref[0:1, :]).astype(BF16)

    def proj(a, width):
        return _dot(h, w_ref[:, a:a + width])

    def roped(a, a_rot, cos_ref, sin_ref):
        return proj(a, GROUP_W) * cos_ref[...] + proj(a_rot, GROUP_W) * sin_ref[...]

    aq = roped(_A_Q, _A_QR, cosa_ref, sina_ref)
    ak = roped(_A_K, _A_KR, cosa_ref, sina_ref)
    av = proj(_A_V, GROUP_W)
    for hd in range(N_HEADS):
        sl = slice(hd * DIFF_DV, (hd + 1) * DIFF_DV)
        aq_ref[hd] = aq[:, sl].astype(BF16)
        ak_ref[hd] = ak[:, sl].astype(BF16)
        av_ref[hd] = av[:, sl].astype(BF16)
    bu_ref[...] = proj(_B_U, 3 * HY_CH)
    cq_ref[...] = roped(_C_Q, _C_QR, cosc_ref, sinc_ref)
    ck_ref[...] = roped(_C_K, _C_KR, cosc_ref, sinc_ref)
    cv_ref[...] = proj(_C_V, GROUP_W)
    cg_ref[...] = proj(_C_G, GROUP_W)
    dq_ref[...] = proj(_D_Q, LANES)
    dk_ref[...] = proj(_D_K, LANES)
    dv_ref[...] = proj(_D_V, GROUP_W)
    dr_ref[...] = proj(_D_R, GROUP_W)
    dl_ref[...] = proj(_D_L, LANES)


def _in_projection(x, mod, g, w_wide, rope, geo):
    nt, d = x.shape
    tm = ROW_TILE
    lat_tiles, per_batch, batch = geo["lat_tiles"], geo["tiles_per_batch"], geo["batch"]

    def mod_map(i):
        return (jnp.where(i < lat_tiles, i // per_batch, batch), 0, 0)

    def rope_map(i):
        return (jnp.where(i < lat_tiles, i % per_batch, per_batch), 0)

    row = lambda i: (i, 0)
    head = lambda i: (0, i, 0)
    const = lambda i: (0, 0)
    f32_out = lambda w: jax.ShapeDtypeStruct((nt, w), F32)
    head_out = jax.ShapeDtypeStruct((N_HEADS, nt, DIFF_DV), BF16)
    widths = [3 * HY_CH, GROUP_W, GROUP_W, GROUP_W, GROUP_W, LANES, LANES, GROUP_W, GROUP_W, LANES]
    return pl.pallas_call(
        _inproj_kernel,
        grid=(nt // tm,),
        in_specs=[pl.BlockSpec((tm, d), row),
                  pl.BlockSpec((None, 6, d), mod_map),
                  pl.BlockSpec((1, d), const),
                  pl.BlockSpec((d, _IN_COLS), const, pipeline_mode=pl.Buffered(1)),
                  pl.BlockSpec((tm, GROUP_W), rope_map), pl.BlockSpec((tm, GROUP_W), rope_map),
                  pl.BlockSpec((tm, GROUP_W), rope_map), pl.BlockSpec((tm, GROUP_W), rope_map)],
        out_specs=[pl.BlockSpec((N_HEADS, tm, DIFF_DV), head)] * 3
                  + [pl.BlockSpec((tm, w), row) for w in widths],
        out_shape=[head_out] * 3 + [f32_out(w) for w in widths],
        compiler_params=_cparams(("arbitrary",)),
        name="in_projection",
    )(x, mod, g, w_wide, *rope)


def _attn_kernel(lam_ref, g_ref, q_ref, *rest, lam_init, has_lat):
    if has_lat:
        kl_ref, vl_ref, kc_ref, vc_ref, o_ref = rest
        keys = [(kl_ref, vl_ref), (kc_ref, vc_ref)]
    else:
        kc_ref, vc_ref, o_ref = rest
        keys = [(kc_ref, vc_ref)]
    lp = lam_ref[...]
    lam = (jnp.exp(jnp.sum(lp[0:1] * lp[1:2], axis=-1, keepdims=True))
           - jnp.exp(jnp.sum(lp[2:3] * lp[3:4], axis=-1, keepdims=True)) + lam_init)
    q = q_ref[...]
    scale = DIFF_DQK ** -0.5
    probs = []
    for m in range(2):
        sl = slice(m * DIFF_DQK, (m + 1) * DIFF_DQK)
        s = [_dot_nt(q[:, sl], k_ref[:, sl]) * scale for k_ref, _ in keys]
        mx = functools.reduce(jnp.maximum, [jnp.max(t, axis=-1, keepdims=True) for t in s])
        p = [jnp.exp(t - mx) for t in s]
        den = functools.reduce(lambda a, b: a + b, [jnp.sum(t, axis=-1, keepdims=True) for t in p])
        probs.append((p, 1.0 / den))
    (p1, r1), (p2, r2) = probs
    o = None
    for j, (_, v_ref) in enumerate(keys):
        a = (p1[j] * r1 - p2[j] * (lam * r2)).astype(BF16)
        t = _dot(a, v_ref[...])
        o = t if o is None else o + t
    o_ref[...] = _rms(o) * g_ref[...] * (1.0 - lam_init)


def _diff_attention(lam_p, subln_g, aq, ak, av, geo, lam_init, prev=None):
    batch, seq, ctx = geo["batch"], geo["seq"], geo["ctx"]
    nt = aq.shape[1]
    has_lat = prev is None
    tq = ROW_TILE
    n_q = (seq if has_lat else ctx) // tq
    q_off = 0 if has_lat else (batch * seq) // tq
    ctx_blk0 = (batch * seq) // ctx

    qmap = lambda b, h, i: (h, q_off + b * n_q + i, 0)
    lat_map = lambda b, h, i: (h, b, 0)
    ctx_map = lambda b, h, i: (h, ctx_blk0 + b, 0)
    const = lambda b, h, i: (0, 0)
    in_specs = [pl.BlockSpec((4, DIFF_DQK), const), pl.BlockSpec((1, DIFF_DV), const),
                pl.BlockSpec((None, tq, DIFF_DV), qmap)]
    args = [lam_p, subln_g, aq]
    if has_lat:
        in_specs += [pl.BlockSpec((None, seq, DIFF_DV), lat_map), pl.BlockSpec((None, seq, DIFF_DV), lat_map)]
        args += [ak, av]
    in_specs += [pl.BlockSpec((None, ctx, DIFF_DV), ctx_map), pl.BlockSpec((None, ctx, DIFF_DV), ctx_map)]
    args += [ak, av]
    aliases = {}
    if not has_lat:
        in_specs.append(pl.BlockSpec(memory_space=pl.ANY))
        args.append(prev)
        aliases = {len(args) - 1: 0}
    kern = functools.partial(_attn_kernel, lam_init=lam_init, has_lat=has_lat)
    if not has_lat:
        kern = _drop_last_input(kern, n_in=len(args))
    return pl.pallas_call(
        kern,
        grid=(batch, N_HEADS, n_q),
        in_specs=in_specs,
        out_specs=pl.BlockSpec((None, tq, DIFF_DV), qmap),
        out_shape=jax.ShapeDtypeStruct((N_HEADS, nt, DIFF_DV), F32),
        input_output_aliases=aliases,
        compiler_params=_cparams(("arbitrary", "arbitrary", "arbitrary")),
        name="diff_attention" if has_lat else "diff_attention_ctx",
    )(*args)


def _drop_last_input(kern, n_in):
    def wrapped(*refs):
        return kern(*refs[:n_in - 1], *refs[n_in:])
    return wrapped


def _dft_tables(n):
    k = jnp.arange(n, dtype=jnp.int32)
    ang = ((k[:, None] * k[None, :]) % (2 * n)).astype(F32) * (math.pi / n)
    return jnp.cos(ang).astype(BF16), jnp.sin(ang).astype(BF16)


def _hy_filter_kernel(w1t_ref, w1c_ref, w1s_ref, b1_ref, w2_ref, b2_ref, w3_ref, fr_ref,
                      bands_ref, deltas_ref, e_ref, d_ref, nyq_ref, *, n):
    pos_i = lax.broadcasted_iota(jnp.int32, (n, 1), 0)
    pos = pos_i.astype(F32)
    t = pos / (n - 1)
    ang = ((2.0 * math.pi) * pos / n) * bands_ref[...]
    pre = t * w1t_ref[...] + _dot_exact(jnp.cos(ang), w1c_ref[...]) - _dot_exact(jnp.sin(ang), w1s_ref[...])
    hdn = jnp.sin(fr_ref[0:1, :] * (pre + b1_ref[...]))
    hdn = jnp.sin(fr_ref[1:2, :] * (_dot_exact(hdn, w2_ref[...]) + b2_ref[...]))
    raw = _dot_exact(hdn, w3_ref[...])
    window = jnp.exp(-t * deltas_ref[...])
    hf = raw[:, :HY_CH] * window
    hb = jnp.where(pos_i > 0, raw[:, HY_CH:] * window, 0.0)
    inv = 1.0 / (jnp.sum(jnp.abs(hf), axis=0, keepdims=True) + jnp.sum(jnp.abs(hb), axis=0, keepdims=True))
    e = (hf + hb) * inv
    e_ref[...] = e.astype(BF16)
    d_ref[...] = ((hb - hf) * inv).astype(BF16)
    sign = (1 - 2 * (pos_i & 1)).astype(F32)
    nyq_ref[...] = jnp.sum(e * sign, axis=0, keepdims=True)


def _hyena_filter(n, w1, b1, w2, b2, w3, freq):
    bands = jnp.linspace(1e-4, HY_BANDS - 1, HY_BANDS, dtype=F32).reshape(1, HY_BANDS)
    max_decay = math.log(HY_DECAY_TARGET) / HY_FAST_DECAY_PCT
    min_decay = math.log(HY_DECAY_TARGET) / HY_SLOW_DECAY_PCT
    deltas = jnp.abs(jnp.linspace(min_decay, max_decay, HY_CH, dtype=F32)).reshape(1, HY_CH)
    args = [w1[0:1], w1[1:1 + HY_BANDS], w1[1 + HY_BANDS:], b1.reshape(1, HY_FFN), w2, b2.reshape(1, HY_FFN),
            w3, freq, bands, deltas]
    return pl.pallas_call(
        functools.partial(_hy_filter_kernel, n=n),
        out_shape=[jax.ShapeDtypeStruct((n, HY_CH), BF16), jax.ShapeDtypeStruct((n, HY_CH), BF16),
                   jax.ShapeDtypeStruct((1, HY_CH), F32)],
        compiler_params=pltpu.CompilerParams(vmem_limit_bytes=VMEM_LIMIT),
        name="hyena_filter",
    )(*args)


def _hy_gate_kernel(x0_ref, x1_ref, v_ref, w0_ref, w1_ref, wv_ref, b0_ref, b1_ref, bv_ref,
                    zb_ref, x0c_ref, zf_ref, nyq_ref, *, n):
    row = lax.broadcasted_iota(jnp.int32, (n, 1), 0)

    def conv(u_ref, w_ref, b_ref):
        u = u_ref[...]
        up = jnp.where(row > 0, pltpu.roll(u, 1, 0), 0.0)
        dn = jnp.where(row < n - 1, pltpu.roll(u, n - 1, 0), 0.0)
        return up * w_ref[0:1, :] + u * w_ref[1:2, :] + dn * w_ref[2:3, :] + b_ref[...]

    z = conv(x1_ref, w1_ref, b1_ref) * conv(v_ref, wv_ref, bv_ref)
    x0c_ref[...] = conv(x0_ref, w0_ref, b0_ref)
    zf_ref[...] = z
    zb_ref[...] = z.astype(BF16)
    sign = (1 - 2 * (row & 1)).astype(F32)
    nyq_ref[...] = jnp.sum(z * sign, axis=0, keepdims=True)


def _hyena_gate(bu, short_w, short_b, n, batch, row_blk0):
    halves = HY_CH // LANES
    sb = short_b.reshape(1, 3 * HY_CH)
    seg = lambda part: pl.BlockSpec((n, LANES), lambda b, j: (row_blk0 + b, part * halves + j))
    wsp = lambda part: pl.BlockSpec((3, LANES), lambda b, j: (0, part * halves + j))
    bsp = lambda part: pl.BlockSpec((1, LANES), lambda b, j: (0, part * halves + j))
    return pl.pallas_call(
        functools.partial(_hy_gate_kernel, n=n),
        grid=(batch, halves),
        in_specs=[seg(0), seg(1), seg(2), wsp(0), wsp(1), wsp(2), bsp(0), bsp(1), bsp(2)],
        out_specs=[pl.BlockSpec((n, LANES), lambda b, j: (0, b * halves + j)),
                   pl.BlockSpec((n, LANES), lambda b, j: (b, j)),
                   pl.BlockSpec((n, LANES), lambda b, j: (b, j)),
                   pl.BlockSpec((None, 1, LANES), lambda b, j: (b, 0, j))],
        out_shape=[jax.ShapeDtypeStruct((n, batch * HY_CH), BF16),
                   jax.ShapeDtypeStruct((batch * n, HY_CH), F32),
                   jax.ShapeDtypeStruct((batch * n, HY_CH), F32),
                   jax.ShapeDtypeStruct((batch, 1, HY_CH), F32)],
        compiler_params=_cparams(("arbitrary", "arbitrary")),
        name="hyena_gate",
    )(bu, bu, bu, short_w, short_w, short_w, sb, sb, sb)


def _hy_spectrum_kernel(c_ref, s_ref, z_ref, e_ref, d_ref, yr_ref, yi_ref, *, n, tk, batch):
    c = c_ref[...]
    s = s_ref[...]
    zr = _dot(c, z_ref[...])
    zs = _dot(s, z_ref[...])
    fr = _dot(c, e_ref[...])
    fi = _dot(s, d_ref[...])
    k = pl.program_id(0) * tk + lax.broadcasted_iota(jnp.int32, (tk, 1), 0)
    wk = jnp.where(k == 0, 1.0, 2.0) * (1.0 / (2 * n))
    for b in range(batch):
        sl = slice(b * HY_CH, (b + 1) * HY_CH)
        yr = zr[:, sl] * fr + zs[:, sl] * fi
        yi = zr[:, sl] * fi - zs[:, sl] * fr
        yr_ref[:, sl] = (yr * wk).astype(BF16)
        yi_ref[:, sl] = (-(yi * wk)).astype(BF16)


def _hy_inverse_kernel(c_ref, s_ref, yr_ref, yi_ref, x0c_ref, zf_ref, nyqz_ref, nyqf_ref, bias_ref,
                       *rest, n, tt, batch):
    o_ref, y_scr = rest[-2], rest[-1]
    b = pl.program_id(1)

    @pl.when(b == 0)
    def _():
        y = _dot(c_ref[...], yr_ref[...]) + _dot(s_ref[...], yi_ref[...])
        for bb in range(batch):
            y_scr[bb] = y[:, bb * HY_CH:(bb + 1) * HY_CH]

    t = pl.program_id(0) * tt + lax.broadcasted_iota(jnp.int32, (tt, 1), 0)
    sign = (1 - 2 * (t & 1)).astype(F32)
    nyq = nyqz_ref[...] * nyqf_ref[...] * (1.0 / (2 * n))
    zf = zf_ref[...]
    o_ref[...] = x0c_ref[...] * (y_scr[b] + sign * nyq + bias_ref[...] * zf)


def _hyena_conv(cs, zb, e, d, x0c, zf, nyqz, nyqf, bias, n, batch, nt, out_blk0, prev=None):
    c_tab, s_tab = cs
    bw = batch * HY_CH
    tk = min(512, n)
    whole = lambda shape: pl.BlockSpec(shape, lambda *_: (0,) * len(shape), pipeline_mode=pl.Buffered(1))
    yr, yi = pl.pallas_call(
        functools.partial(_hy_spectrum_kernel, n=n, tk=tk, batch=batch),
        grid=(n // tk,),
        in_specs=[pl.BlockSpec((tk, n), lambda i: (i, 0)), pl.BlockSpec((tk, n), lambda i: (i, 0)),
                  whole((n, bw)), whole((n, HY_CH)), whole((n, HY_CH))],
        out_specs=[pl.BlockSpec((tk, bw), lambda i: (i, 0))] * 2,
        out_shape=[jax.ShapeDtypeStruct((n, bw), BF16)] * 2,
        compiler_params=_cparams(("arbitrary",)),
        name="hyena_spectrum",
    )(c_tab, s_tab, zb, e, d)

    tt = min(512, n)
    n_t = n // tt
    seg = lambda i, b: (b * n_t + i, 0)
    in_specs = [pl.BlockSpec((tt, n), lambda i, b: (i, 0)), pl.BlockSpec((tt, n), lambda i, b: (i, 0)),
                whole((n, bw)), whole((n, bw)),
                pl.BlockSpec((tt, HY_CH), seg), pl.BlockSpec((tt, HY_CH), seg),
                pl.BlockSpec((None, 1, HY_CH), lambda i, b: (b, 0, 0)),
                pl.BlockSpec((1, HY_CH), lambda i, b: (0, 0)), pl.BlockSpec((1, HY_CH), lambda i, b: (0, 0))]
    args = [c_tab, s_tab, yr, yi, x0c, zf, nyqz, nyqf, bias.reshape(1, HY_CH)]
    aliases = {}
    if prev is not None:
        in_specs.append(pl.BlockSpec(memory_space=pl.ANY))
        args.append(prev)
        aliases = {len(args) - 1: 0}
    return pl.pallas_call(
        functools.partial(_hy_inverse_kernel, n=n, tt=tt, batch=batch),
        grid=(n_t, batch),
        in_specs=in_specs,
        out_specs=pl.BlockSpec((tt, HY_CH), lambda i, b: (out_blk0 + b * n_t + i, 0)),
        out_shape=jax.ShapeDtypeStruct((nt, HY_CH), F32),
        scratch_shapes=[pltpu.VMEM((batch, tt, HY_CH), F32)],
        input_output_aliases=aliases,
        compiler_params=_cparams(("arbitrary", "arbitrary")),
        name="hyena_inverse" if prev is None else "hyena_inverse_ctx",
    )(*args)


def _rec_direction(q, k, v, la, st_ref, o_ref, *, reverse, dk, dv, first):
    cc, width = q.shape
    r_i = lax.broadcasted_iota(jnp.int32, (cc, cc), 0)
    c_i = lax.broadcasted_iota(jnp.int32, (cc, cc), 1)
    incl = (c_i >= r_i) if reverse else (c_i <= r_i)
    cum = _dot_exact(incl.astype(F32), la)
    cum_end = cum[0:1] if reverse else cum[cc - 1:cc]

    @pl.when(first)
    def _():
        st_ref[...] = jnp.zeros_like(st_ref)

    qd = (q * jnp.exp(cum)).astype(BF16)
    kd = (k * jnp.exp(cum_end - cum)).astype(BF16)
    vb = v.astype(BF16)
    gain = jnp.exp(cum_end)
    inter = []
    for h in range(N_HEADS):
        ks = slice(h * dk, (h + 1) * dk)
        vs = slice(h * dv, (h + 1) * dv)
        st = st_ref[h]
        inter.append(_dot_nt(qd[:, ks], st.astype(BF16)))
        st_ref[h] = st * gain[:, ks] + _dot_tn(vb[:, vs], kd[:, ks])

    for j in range(cc // REC_SUB):
        r0, r1 = j * REC_SUB, (j + 1) * REC_SUB
        if reverse:
            ka, kb = r0, cc
            base = cum[r1:r1 + 1] if r1 < cc else jnp.zeros((1, width), F32)
        else:
            ka, kb = 0, r1
            base = cum[r0 - 1:r0] if r0 > 0 else jnp.zeros((1, width), F32)
        qj = (q[r0:r1] * jnp.exp(cum[r0:r1] - base)).astype(BF16)
        kj = (k[ka:kb] * jnp.exp(jnp.minimum(base - cum[ka:kb], EXP_CLAMP))).astype(BF16)
        rows = r0 + lax.broadcasted_iota(jnp.int32, (REC_SUB, kb - ka), 0)
        cols = ka + lax.broadcasted_iota(jnp.int32, (REC_SUB, kb - ka), 1)
        keep = (cols >= rows) if reverse else (cols <= rows)
        for h in range(N_HEADS):
            ks = slice(h * dk, (h + 1) * dk)
            vs = slice(h * dv, (h + 1) * dv)
            sc = jnp.where(keep, _dot_nt(qj[:, ks], kj[:, ks]), 0.0).astype(BF16)
            o_ref[r0:r1, vs] = _dot(sc, vb[ka:kb, vs]) + inter[h][r0:r1]


def _rec_kernel(*refs, mode, dk, dv, scale_q, scale_k):
    if mode == "gla":
        (qf_ref, kf_ref, vf_ref, lf_ref, qb_ref, kb_ref, vb_ref, lb_ref, gw_ref, gb_ref,
         of_ref, ob_ref, st_ref) = refs
    else:
        qf_ref, kf_ref, vf_ref, qb_ref, kb_ref, vb_ref, dec_ref, of_ref, ob_ref, st_ref = refs
    first = pl.program_id(1) == 0
    for d, (q_ref, k_ref, v_ref, o_ref) in enumerate(((qf_ref, kf_ref, vf_ref, of_ref),
                                                      (qb_ref, kb_ref, vb_ref, ob_ref))):
        q = q_ref[...] * scale_q
        k = k_ref[...] * scale_k
        if mode == "gla":
            l_ref = lf_ref if d == 0 else lb_ref
            logit = _dot_exact(l_ref[...], gw_ref[d]) + gb_ref[d]
            la = (jnp.minimum(logit, 0.0) - jnp.log(1.0 + jnp.exp(-jnp.abs(logit)))) * (1.0 / GLA_GATE_NORM)
        else:
            la = jnp.broadcast_to(-jnp.exp(dec_ref[d]), q.shape)
        _rec_direction(q, k, v_ref[...], la, st_ref.at[d], o_ref, reverse=(d == 1), dk=dk, dv=dv, first=first)


def _linear_recurrence(mode, q, k, v, extra, geo, dk, dv, scale_q, scale_k):
    batch, seq, ctx = geo["batch"], geo["seq"], geo["ctx"]
    nt = q.shape[0]
    cc = REC_CHUNK
    n_cc, n_lc = ctx // cc, seq // cc
    ctx0 = (batch * seq) // cc

    def fwd(b, i):
        return (jnp.where(i < n_cc, ctx0 + b * n_cc + i, b * n_lc + i - n_cc), 0)

    def bwd(b, i):
        return (jnp.where(i < n_cc, ctx0 + b * n_cc + (n_cc - 1 - i), b * n_lc + (n_lc - 1 - (i - n_cc))), 0)

    wq, wv = N_HEADS * dk, N_HEADS * dv
    blk = lambda w, m: pl.BlockSpec((cc, w), m)
    if mode == "gla":
        glr, gw, gb = extra
        in_specs = [blk(wq, fwd), blk(wq, fwd), blk(wv, fwd), blk(LANES, fwd),
                    blk(wq, bwd), blk(wq, bwd), blk(wv, bwd), blk(LANES, bwd),
                    pl.BlockSpec((2, LANES, wq), lambda b, i: (0, 0, 0)),
                    pl.BlockSpec((2, 1, wq), lambda b, i: (0, 0, 0))]
        args = [q, k, v, glr, q, k, v, glr, gw, gb]
    else:
        (dec,) = extra
        in_specs = [blk(wq, fwd), blk(wq, fwd), blk(wv, fwd), blk(wq, bwd), blk(wq, bwd), blk(wv, bwd),
                    pl.BlockSpec((2, 1, wq), lambda b, i: (0, 0, 0))]
        args = [q, k, v, q, k, v, dec]
    return pl.pallas_call(
        functools.partial(_rec_kernel, mode=mode, dk=dk, dv=dv, scale_q=scale_q, scale_k=scale_k),
        grid=(batch, n_cc + n_lc),
        in_specs=in_specs,
        out_specs=[blk(wv, fwd), blk(wv, bwd)],
        out_shape=[jax.ShapeDtypeStruct((nt, wv), F32)] * 2,
        scratch_shapes=[pltpu.VMEM((2, N_HEADS, dv, dk), F32)],
        compiler_params=_cparams(("arbitrary", "arbitrary")),
        name="recurrence_" + mode,
    )(*args)


def _mixout_kernel(x_ref, mod_ref, at_ref, hy_ref, rf_ref, rb_ref, cg_ref, gf_ref, gb_ref, dr_ref,
                   seg_ref, rg_ref, gg_ref, wo_ref, g1_ref, g2_ref, rw_ref, rbias_ref,
                   x1_ref, h2_ref, ti_ref, tw_ref):
    def head_norm(o):
        ms = _dot_exact(o * o, seg_ref[...]) * (1.0 / RET_DH)
        return o * lax.rsqrt(ms + EPS)

    a = jnp.concatenate([at_ref[h] for h in range(N_HEADS)], axis=-1)
    cg = cg_ref[...]
    dr = dr_ref[...]
    rt = head_norm(rf_ref[...] + rb_ref[...]) * rg_ref[...] * (cg * _sigmoid(cg))
    gl = head_norm(gf_ref[...] + gb_ref[...]) * gg_ref[...] * (dr * _sigmoid(dr))
    cat = jnp.concatenate([a, hy_ref[...], rt, gl], axis=-1).astype(BF16)
    y = _dot(cat, wo_ref[...])
    x1 = x_ref[...] + mod_ref[2:3, :] * (_rms(y) * g1_ref[...])
    x1_ref[...] = x1
    h2 = _rms(x1) * g2_ref[...] * (1.0 + mod_ref[4:5, :]) + mod_ref[3:4, :]
    h2_ref[...] = h2.astype(BF16)

    vals = _dot_exact(h2, rw_ref[...]) + rbias_ref[...]
    lane = lax.broadcasted_iota(jnp.int32, vals.shape, 1)
    idx_out = jnp.zeros(vals.shape, jnp.int32)
    w_out = jnp.zeros(vals.shape, F32)
    top = None
    den = 0.0
    picks = []
    for r in range(TOP_K):
        m = jnp.max(vals, axis=-1, keepdims=True)
        idx = jnp.min(jnp.where(vals == m, lane, LANES), axis=-1, keepdims=True)
        vals = jnp.where(lane == idx, -jnp.inf, vals)
        top = m if top is None else top
        e = jnp.exp(m - top)
        den = den + e
        picks.append((idx, e))
    inv = 1.0 / den
    for r, (idx, e) in enumerate(picks):
        idx_out = jnp.where(lane == r, idx, idx_out)
        w_out = jnp.where(lane == r, e * inv, w_out)
    ti_ref[...] = idx_out
    tw_ref[...] = w_out


def _mix_out(n_tiles, x, mod, at, hy, rf, rb, cg, gf, gb, dr, seg, rg, gg, wo, g1, g2, rw, rbias, geo):
    nt, d = x.shape
    tm = ROW_TILE
    lat_tiles, per_batch, batch = geo["lat_tiles"], geo["tiles_per_batch"], geo["batch"]

    def mod_map(i):
        return (jnp.where(i < lat_tiles, i // per_batch, batch), 0, 0)

    row = lambda i: (i, 0)
    const = lambda i: (0, 0)
    g_blk = pl.BlockSpec((tm, GROUP_W), row)
    return pl.pallas_call(
        _mixout_kernel,
        grid=(n_tiles,),
        in_specs=[pl.BlockSpec((tm, d), row), pl.BlockSpec((None, 6, d), mod_map),
                  pl.BlockSpec((N_HEADS, tm, DIFF_DV), lambda i: (0, i, 0)),
                  g_blk, g_blk, g_blk, g_blk, g_blk, g_blk, g_blk,
                  pl.BlockSpec((GROUP_W, GROUP_W), const), pl.BlockSpec((1, GROUP_W), const),
                  pl.BlockSpec((1, GROUP_W), const),
                  pl.BlockSpec((d, d), const, pipeline_mode=pl.Buffered(1)),
                  pl.BlockSpec((1, d), const), pl.BlockSpec((1, d), const),
                  pl.BlockSpec((d, LANES), const), pl.BlockSpec((1, LANES), const)],
        out_specs=[pl.BlockSpec((tm, d), row), pl.BlockSpec((tm, d), row),
                   pl.BlockSpec((tm, LANES), row), pl.BlockSpec((tm, LANES), row)],
        out_shape=[jax.ShapeDtypeStruct((nt, d), F32), jax.ShapeDtypeStruct((nt, d), BF16),
                   jax.ShapeDtypeStruct((nt, LANES), jnp.int32), jax.ShapeDtypeStruct((nt, LANES), F32)],
        compiler_params=_cparams(("arbitrary",)),
        name="mix_out_router",
    )(x, mod, at, hy, rf, rb, cg, gf, gb, dr, seg, rg, gg, wo, g1, g2, rw, rbias)


def _expert_kernel(te_ref, na_ref, xs_ref, wg_ref, wl_ref, bg_ref, bl_ref, w2_ref, b2_ref, ys_ref):
    active = pl.program_id(0) < na_ref[0]

    @pl.when(active)
    def _():
        x = xs_ref[...]
        glu = jnp.minimum(_dot(x, wg_ref[...]) + bg_ref[...], SWIGLU_LIMIT)
        lin = jnp.clip(_dot(x, wl_ref[...]) + bl_ref[...], -SWIGLU_LIMIT, SWIGLU_LIMIT)
        act = glu * _sigmoid(SWIGLU_ALPHA * glu) * (lin + 1.0)
        ys_ref[...] = _dot(act.astype(BF16), w2_ref[...]) + b2_ref[...]

    @pl.when(jnp.logical_not(active))
    def _():
        ys_ref[...] = jnp.zeros_like(ys_ref)


def _expert_ffn(tile_e, n_active, xs, wg, wl, bg, bl, w2, b2):
    p, d = xs.shape
    tm = MOE_TILE
    de = wg.shape[-1]
    wmap = lambda i, te, na: (te[i], 0, 0)
    grid_spec = pltpu.PrefetchScalarGridSpec(
        num_scalar_prefetch=2,
        grid=(p // tm,),
        in_specs=[pl.BlockSpec((tm, d), lambda i, te, na: (i, 0)),
                  pl.BlockSpec((None, d, de), wmap), pl.BlockSpec((None, d, de), wmap),
                  pl.BlockSpec((None, 1, de), wmap), pl.BlockSpec((None, 1, de), wmap),
                  pl.BlockSpec((None, de, d), wmap), pl.BlockSpec((None, 1, d), wmap)],
        out_specs=pl.BlockSpec((tm, d), lambda i, te, na: (i, 0)),
    )
    return pl.pallas_call(
        _expert_kernel,
        grid_spec=grid_spec,
        out_shape=jax.ShapeDtypeStruct((p, d), F32),
        compiler_params=_cparams(("arbitrary",)),
        name="expert_ffn",
    )(tile_e, n_active, xs, wg, wl, bg, bl, w2, b2)


def _route(topi, tm):
    n, k = topi.shape
    flat = topi.reshape(-1)
    order = jnp.argsort(flat, stable=True).astype(jnp.int32)
    sorted_e = flat[order]
    counts = jnp.zeros((N_EXPERTS,), jnp.int32).at[flat].add(1)
    cstart = jnp.cumsum(counts) - counts
    padded = ((counts + tm - 1) // tm) * tm
    pend = jnp.cumsum(padded)
    pstart = pend - padded
    dest_sorted = pstart[sorted_e] + jnp.arange(n * k, dtype=jnp.int32) - cstart[sorted_e]
    p = n * k + N_EXPERTS * tm
    src_tok = jnp.zeros((p,), jnp.int32).at[dest_sorted].set(order // k)
    dest = jnp.zeros((n * k,), jnp.int32).at[order].set(dest_sorted).reshape(n, k)
    tile_start = jnp.arange(p // tm, dtype=jnp.int32) * tm
    tile_e = jnp.minimum(jnp.searchsorted(pend, tile_start, side="right"), N_EXPERTS - 1).astype(jnp.int32)
    n_active = (pend[-1:] // tm).astype(jnp.int32)
    return src_tok, dest, tile_e, n_active


def _ffn_residual_kernel(x_ref, y_ref, mod_ref, g_ref, o_ref):
    o_ref[...] = x_ref[...] + mod_ref[5:6, :] * (_rms(y_ref[...]) * g_ref[...])


def _ffn_residual(n_tiles, x, y, mod, g, geo):
    nt, d = x.shape
    tm = ROW_TILE
    lat_tiles, per_batch, batch = geo["lat_tiles"], geo["tiles_per_batch"], geo["batch"]

    def mod_map(i):
        return (jnp.where(i < lat_tiles, i // per_batch, batch), 0, 0)

    row = lambda i: (i, 0)
    return pl.pallas_call(
        _ffn_residual_kernel,
        grid=(n_tiles,),
        in_specs=[pl.BlockSpec((tm, d), row), pl.BlockSpec((tm, d), row),
                  pl.BlockSpec((None, 6, d), mod_map), pl.BlockSpec((1, d), lambda i: (0, 0))],
        out_specs=pl.BlockSpec((tm, d), row),
        out_shape=jax.ShapeDtypeStruct((nt, d), F32),
        compiler_params=_cparams(("arbitrary",)),
        name="ffn_residual",
    )(x, y, mod, g)


def _rope_partner(dim):
    h, q = dim // 2, dim // 4
    perm = np.zeros(dim, np.int32)
    sign = np.zeros(dim, np.float32)
    for base in (0, h):
        for j in range(q):
            perm[base + j], sign[base + j] = base + j + q, -1.0
            perm[base + q + j], sign[base + q + j] = base + j, 1.0
    return perm, sign


def _rot_cols(w, dim):
    perm, sign = _rope_partner(dim)
    reps = w.shape[1] // dim
    full_perm = np.concatenate([perm + r * dim for r in range(reps)])
    return w[:, full_perm] * jnp.asarray(np.tile(sign, reps))


def _rope_tables(n_tok, dim, reps, pad_rows):
    rows = n_tok // GRID_W
    row = jnp.repeat(jnp.arange(rows, dtype=F32), GRID_W)
    col = jnp.tile(jnp.arange(GRID_W, dtype=F32), rows)
    quarter = dim // 4
    inv = ROPE_BASE ** (-jnp.arange(quarter, dtype=F32) / quarter)
    ar = row[:, None] * inv[None]
    ac = col[:, None] * inv[None]
    ang = jnp.concatenate([ar, ar, ac, ac], axis=-1)
    cos = jnp.concatenate([jnp.tile(jnp.cos(ang), (1, reps)), jnp.ones((pad_rows, dim * reps), F32)])
    sin = jnp.concatenate([jnp.tile(jnp.sin(ang), (1, reps)), jnp.zeros((pad_rows, dim * reps), F32)])
    return cos, sin


def _widen_w_in(w):
    sizes = (256, 256, 256, 768, 256, 256, 256, 256, 128, 128, 256, 256, 32)
    cuts = np.cumsum(sizes)[:-1].tolist()
    aq, ak, av, bu, cq, ck, cv, cg, dq, dk, dv, dr, dl = jnp.split(w, cuts, axis=1)
    dl = jnp.pad(dl, ((0, 0), (0, LANES - dl.shape[1])))
    parts = [aq, _rot_cols(aq, DIFF_DQK), ak, _rot_cols(ak, DIFF_DQK), av, bu,
             cq, _rot_cols(cq, RET_DH), ck, _rot_cols(ck, RET_DH), cv, cg, dq, dk, dv, dr, dl]
    return jnp.concatenate(parts, axis=1).astype(BF16)


def kernel(x, c, ctx, c_ctx, w_mod, b_mod, norm_g, w_in, w_out, diff_lambda, diff_subln_g, hy_short_w, hy_short_b, hy_w1, hy_b1, hy_w2, hy_b2, hy_w3, hy_freq, hy_bias, ret_decay, ret_norm_g, gla_gate_w, gla_gate_b, gla_norm_g, router_w, router_b, exp_w1, exp_b1, exp_w2, exp_b2):
    batch, seq, d = x.shape
    n_ctx = ctx.shape[1]
    depth = w_mod.shape[0]
    n_lat_rows, n_ctx_rows = batch * seq, batch * n_ctx
    nt = n_lat_rows + n_ctx_rows
    assert d == D_MODEL and seq % ROW_TILE == 0 and n_ctx % ROW_TILE == 0 and seq % n_ctx == 0
    geo = dict(batch=batch, seq=seq, ctx=n_ctx, lat_tiles=n_lat_rows // ROW_TILE,
               tiles_per_batch=seq // ROW_TILE)

    xs = jnp.concatenate([x.reshape(n_lat_rows, d), ctx.reshape(n_ctx_rows, d)], axis=0)
    mod_rows = 8
    cc = jnp.zeros((mod_rows, d), F32).at[:batch].set(c).at[batch].set(c_ctx)
    mod_all = _modulation(cc, w_mod, b_mod).reshape(depth, mod_rows, 6, d)

    rope = (*_rope_tables(seq, DIFF_DQK, GROUP_W // DIFF_DQK, ROW_TILE),
            *_rope_tables(seq, RET_DH, GROUP_W // RET_DH, ROW_TILE))
    dft_lat = _dft_tables(seq)
    dft_ctx = _dft_tables(n_ctx)
    seg = jnp.asarray(np.kron(np.eye(N_HEADS, dtype=np.float32), np.ones((RET_DH, RET_DH), np.float32)))

    for l in range(depth):
        need_ctx = l < depth - 1
        lam_init = 0.8 - 0.6 * math.exp(-0.3 * l)
        mod = mod_all[l]
        (aq, ak, av, bu, cq, ck, cv, cg, dq, dk, dv, dr, dl) = _in_projection(
            xs, mod, norm_g[l, 0].reshape(1, d), _widen_w_in(w_in[l]), rope, geo)

        sub_g = diff_subln_g[l].reshape(1, DIFF_DV)
        at = _diff_attention(diff_lambda[l], sub_g, aq, ak, av, geo, lam_init)
        if need_ctx:
            at = _diff_attention(diff_lambda[l], sub_g, aq, ak, av, geo, lam_init, prev=at)

        def hyena(n, row_blk0, tables, out_blk0, prev):
            e, dd, nyqf = _hyena_filter(n, hy_w1[l], hy_b1[l], hy_w2[l], hy_b2[l], hy_w3[l], hy_freq[l])
            zb, x0c, zf, nyqz = _hyena_gate(bu, hy_short_w[l], hy_short_b[l], n, batch, row_blk0)
            return _hyena_conv(tables, zb, e, dd, x0c, zf, nyqz, nyqf, hy_bias[l], n, batch, nt, out_blk0, prev)

        hy = hyena(seq, 0, dft_lat, 0, None)
        if need_ctx:
            hy = hyena(n_ctx, n_lat_rows // n_ctx, dft_ctx, n_lat_rows // min(512, n_ctx), hy)

        dec = jnp.repeat(ret_decay[l], RET_DH, axis=-1).reshape(2, 1, N_HEADS * RET_DH)
        rf, rb = _linear_recurrence("ret", cq, ck, cv, (dec,), geo, RET_DH, RET_DH, 1.0, RET_DH ** -0.5)

        gw = jnp.zeros((2, LANES, N_HEADS * GLA_DK), F32)
        gw = gw.at[0, :GLA_RANK].set(gla_gate_w[l, 0]).at[1, GLA_RANK:2 * GLA_RANK].set(gla_gate_w[l, 1])
        gbias = gla_gate_b[l].reshape(2, 1, N_HEADS * GLA_DK)
        gf, gb = _linear_recurrence("gla", dq, dk, dv, (dl, gw, gbias), geo, GLA_DK, GLA_DV, GLA_DK ** -0.5, 1.0)

        n_tiles = nt // ROW_TILE if need_ctx else n_lat_rows // ROW_TILE
        rw = jnp.pad(router_w[l], ((0, 0), (0, LANES - N_EXPERTS)))
        rbias = jnp.pad(router_b[l], (0, LANES - N_EXPERTS), constant_values=-jnp.inf).reshape(1, LANES)
        x1, h2, topi, topw = _mix_out(
            n_tiles, xs, mod, at, hy, rf, rb, cg, gf, gb, dr, seg,
            ret_norm_g[l].reshape(1, GROUP_W), jnp.tile(gla_norm_g[l], N_HEADS).reshape(1, GROUP_W),
            w_out[l].astype(BF16), norm_g[l, 1].reshape(1, d), norm_g[l, 2].reshape(1, d), rw, rbias, geo)

        n_tok = n_tiles * ROW_TILE
        topi = topi[:n_tok, :TOP_K]
        topw = topw[:n_tok, :TOP_K]
        src_tok, dest, tile_e, n_active = _route(topi, MOE_TILE)
        gathered = jnp.take(h2, src_tok, axis=0)
        w1 = exp_w1[l]
        ys = _expert_ffn(tile_e, n_active, gathered,
                         w1[:, :, 0::2].astype(BF16), w1[:, :, 1::2].astype(BF16),
                         exp_b1[l][:, None, 0::2], exp_b1[l][:, None, 1::2],
                         exp_w2[l].astype(BF16), exp_b2[l][:, None, :])
        y = jnp.sum(jnp.take(ys, dest, axis=0) * topw[:, :, None], axis=1)
        if n_tok < nt:
            y = jnp.pad(y, ((0, nt - n_tok), (0, 0)))
        xs = _ffn_residual(n_tiles, x1, y, mod, norm_g[l, 3].reshape(1, d), geo)

    return xs[:n_lat_rows].reshape(batch, seq, d)
```

```python
import functools
import math

import numpy as np
import jax
import jax.numpy as jnp
from jax import lax
from jax.experimental import pallas as pl
from jax.experimental.pallas import tpu as pltpu

F32 = jnp.float32
BF16 = jnp.bfloat16
HIGHEST = lax.Precision.HIGHEST

D_MODEL = 1024
GRID_W = 64
GROUP_W = 256
N_HEADS = 4
DIFF_DQK = 32
DIFF_DV = 64
ROPE_BASE = 10000.0
HY_CH = 256
HY_BANDS = 16
HY_FFN = 64
HY_FAST_DECAY_PCT = 0.3
HY_SLOW_DECAY_PCT = 1.5
HY_DECAY_TARGET = 1e-2
RET_DH = 64
GLA_DK = 32
GLA_DV = 64
GLA_RANK = 16
GLA_GATE_NORM = 16.0
N_EXPERTS = 32
TOP_K = 4
D_EXPERT = 1024
SWIGLU_LIMIT = 7.0
SWIGLU_ALPHA = 1.702
EPS = 1e-6

LANES = 128
ROW_TILE = 256
REC_CHUNK = 64
REC_SUB = 16
EXP_CLAMP = 80.0
MOE_TILE = 256
VMEM_LIMIT = 52 * 1024 * 1024

_A_Q, _A_QR, _A_K, _A_KR, _A_V = 0, 256, 512, 768, 1024
_B_U = 1280
_C_Q, _C_QR, _C_K, _C_KR, _C_V, _C_G = 2048, 2304, 2560, 2816, 3072, 3328
_D_Q, _D_K, _D_V, _D_R, _D_L = 3584, 3712, 3840, 4096, 4352
_IN_COLS = 4480


def _cparams(sem):
    return pltpu.CompilerParams(dimension_semantics=sem, vmem_limit_bytes=VMEM_LIMIT)


def _sigmoid(x):
    return 1.0 / (1.0 + jnp.exp(-x))


def _rms(x):
    return x * lax.rsqrt(jnp.mean(x * x, axis=-1, keepdims=True) + EPS)


def _dot(a, b):
    return jnp.dot(a, b, preferred_element_type=F32)


def _dot_exact(a, b):
    return jnp.dot(a, b, preferred_element_type=F32, precision=HIGHEST)


def _dot_nt(a, b):
    return lax.dot_general(a, b, (((1,), (1,)), ((), ())), preferred_element_type=F32)


def _dot_tn(a, b):
    return lax.dot_general(a, b, (((0,), (0,)), ((), ())), preferred_element_type=F32)


def _mod_kernel(c_ref, w_ref, b_ref, o_ref):
    c = c_ref[...]
    o_ref[...] = _dot_exact(c * _sigmoid(c), w_ref[...]) + b_ref[...]


def _modulation(cc, w_mod, b_mod):
    depth, d, n = w_mod.shape
    tn = n // 4
    rows = cc.shape[0]
    return pl.pallas_call(
        _mod_kernel,
        grid=(depth, n // tn),
        in_specs=[pl.BlockSpec((rows, d), lambda l, j: (0, 0)),
                  pl.BlockSpec((None, d, tn), lambda l, j: (l, 0, j)),
                  pl.BlockSpec((None, 1, tn), lambda l, j: (l, 0, j))],
        out_specs=pl.BlockSpec((None, rows, tn), lambda l, j: (l, 0, j)),
        out_shape=jax.ShapeDtypeStruct((depth, rows, n), F32),
        compiler_params=_cparams(("arbitrary", "arbitrary")),
        name="modulation",
    )(cc, w_mod, b_mod.reshape(depth, 1, n))


def _inproj_kernel(x_ref, mod_ref, g_ref, w_ref, cosa_ref, sina_ref, cosc_ref, sinc_ref,
                   aq_ref, ak_ref, av_ref, bu_ref, cq_ref, ck_ref, cv_ref, cg_ref,
                   dq_ref, dk_ref, dv_ref, dr_ref, dl_ref):
    xn = _rms(x_ref[...]) * g_ref[...]
    h = (xn * (1.0 + mod_ref[1:2, :]) + mod_ref[0:1, :]).astype(BF16)

    def proj(a, width):
        return _dot(h, w_ref[:, a:a + width])

    def roped(a, a_rot, cos_ref, sin_ref):
        return proj(a, GROUP_W) * cos_ref[...] + proj(a_rot, GROUP_W) * sin_ref[...]

    aq = roped(_A_Q, _A_QR, cosa_ref, sina_ref)
    ak = roped(_A_K, _A_KR, cosa_ref, sina_ref)
    av = proj(_A_V, GROUP_W)
    for hd in range(N_HEADS):
        sl = slice(hd * DIFF_DV, (hd + 1) * DIFF_DV)
        aq_ref[hd] = aq[:, sl].astype(BF16)
        ak_ref[hd] = ak[:, sl].astype(BF16)
        av_ref[hd] = av[:, sl].astype(BF16)
    bu_ref[...] = proj(_B_U, 3 * HY_CH)
    cq_ref[...] = roped(_C_Q, _C_QR, cosc_ref, sinc_ref)
    ck_ref[...] = roped(_C_K, _C_KR, cosc_ref, sinc_ref)
    cv_ref[...] = proj(_C_V, GROUP_W)
    cg_ref[...] = proj(_C_G, GROUP_W)
    dq_ref[...] = proj(_D_Q, LANES)
    dk_ref[...] = proj(_D_K, LANES)
    dv_ref[...] = proj(_D_V, GROUP_W)
    dr_ref[...] = proj(_D_R, GROUP_W)
    dl_ref[...] = proj(_D_L, LANES)


def _in_projection(x, mod, g, w_wide, rope, geo):
    nt, d = x.shape
    tm = ROW_TILE
    lat_tiles, per_batch, batch = geo["lat_tiles"], geo["tiles_per_batch"], geo["batch"]

    def mod_map(i):
        return (jnp.where(i < lat_tiles, i // per_batch, batch), 0, 0)

    def rope_map(i):
        return (jnp.where(i < lat_tiles, i % per_batch, per_batch), 0)

    row = lambda i: (i, 0)
    head = lambda i: (0, i, 0)
    const = lambda i: (0, 0)
    f32_out = lambda w: jax.ShapeDtypeStruct((nt, w), F32)
    head_out = jax.ShapeDtypeStruct((N_HEADS, nt, DIFF_DV), BF16)
    widths = [3 * HY_CH, GROUP_W, GROUP_W, GROUP_W, GROUP_W, LANES, LANES, GROUP_W, GROUP_W, LANES]
    return pl.pallas_call(
        _inproj_kernel,
        grid=(nt // tm,),
        in_specs=[pl.BlockSpec((tm, d), row),
                  pl.BlockSpec((None, 6, d), mod_map),
                  pl.BlockSpec((1, d), const),
                  pl.BlockSpec((d, _IN_COLS), const, pipeline_mode=pl.Buffered(1)),
                  pl.BlockSpec((tm, GROUP_W), rope_map), pl.BlockSpec((tm, GROUP_W), rope_map),
                  pl.BlockSpec((tm, GROUP_W), rope_map), pl.BlockSpec((tm, GROUP_W), rope_map)],
        out_specs=[pl.BlockSpec((N_HEADS, tm, DIFF_DV), head)] * 3
                  + [pl.BlockSpec((tm, w), row) for w in widths],
        out_shape=[head_out] * 3 + [f32_out(w) for w in widths],
        compiler_params=_cparams(("arbitrary",)),
        name="in_projection",
    )(x, mod, g, w_wide, *rope)


def _attn_kernel(lam_ref, g_ref, q_ref, *rest, lam_init, has_lat):
    if has_lat:
        kl_ref, vl_ref, kc_ref, vc_ref, o_ref = rest
        keys = [(kl_ref, vl_ref), (kc_ref, vc_ref)]
    else:
        kc_ref, vc_ref, o_ref = rest
        keys = [(kc_ref, vc_ref)]
    lp = lam_ref[...]
    lam = (jnp.exp(jnp.sum(lp[0:1] * lp[1:2], axis=-1, keepdims=True))
           - jnp.exp(jnp.sum(lp[2:3] * lp[3:4], axis=-1, keepdims=True)) + lam_init)
    q = q_ref[...]
    scale = DIFF_DQK ** -0.5
    probs = []
    for m in range(2):
        sl = slice(m * DIFF_DQK, (m + 1) * DIFF_DQK)
        s = [_dot_nt(q[:, sl], k_ref[:, sl]) * scale for k_ref, _ in keys]
        mx = functools.reduce(jnp.maximum, [jnp.max(t, axis=-1, keepdims=True) for t in s])
        p = [jnp.exp(t - mx) for t in s]
        den = functools.reduce(lambda a, b: a + b, [jnp.sum(t, axis=-1, keepdims=True) for t in p])
        probs.append((p, 1.0 / den))
    (p1, r1), (p2, r2) = probs
    o = None
    for j, (_, v_ref) in enumerate(keys):
        a = (p1[j] * r1 - p2[j] * (lam * r2)).astype(BF16)
        t = _dot(a, v_ref[...])
        o = t if o is None else o + t
    o_ref[...] = _rms(o) * g_ref[...] * (1.0 - lam_init)


def _diff_attention(lam_p, subln_g, aq, ak, av, geo, lam_init, prev=None):
    batch, seq, ctx = geo["batch"], geo["seq"], geo["ctx"]
    nt = aq.shape[1]
    has_lat = prev is None
    tq = ROW_TILE
    n_q = (seq if has_lat else ctx) // tq
    q_off = 0 if has_lat else (batch * seq) // tq
    ctx_blk0 = (batch * seq) // ctx

    qmap = lambda b, h, i: (h, q_off + b * n_q + i, 0)
    lat_map = lambda b, h, i: (h, b, 0)
    ctx_map = lambda b, h, i: (h, ctx_blk0 + b, 0)
    const = lambda b, h, i: (0, 0)
    in_specs = [pl.BlockSpec((4, DIFF_DQK), const), pl.BlockSpec((1, DIFF_DV), const),
                pl.BlockSpec((None, tq, DIFF_DV), qmap)]
    args = [lam_p, subln_g, aq]
    if has_lat:
        in_specs += [pl.BlockSpec((None, seq, DIFF_DV), lat_map), pl.BlockSpec((None, seq, DIFF_DV), lat_map)]
        args += [ak, av]
    in_specs += [pl.BlockSpec((None, ctx, DIFF_DV), ctx_map), pl.BlockSpec((None, ctx, DIFF_DV), ctx_map)]
    args += [ak, av]
    aliases = {}
    if not has_lat:
        in_specs.append(pl.BlockSpec(memory_space=pl.ANY))
        args.append(prev)
        aliases = {len(args) - 1: 0}
    kern = functools.partial(_attn_kernel, lam_init=lam_init, has_lat=has_lat)
    if not has_lat:
        kern = _drop_last_input(kern, n_in=len(args))
    return pl.pallas_call(
        kern,
        grid=(batch, N_HEADS, n_q),
        in_specs=in_specs,
        out_specs=pl.BlockSpec((None, tq, DIFF_DV), qmap),
        out_shape=jax.ShapeDtypeStruct((N_HEADS, nt, DIFF_DV), F32),
        input_output_aliases=aliases,
        compiler_params=_cparams(("arbitrary", "arbitrary", "arbitrary")),
        name="diff_attention" if has_lat else "diff_attention_ctx",
    )(*args)


def _drop_last_input(kern, n_in):
    def wrapped(*refs):
        return kern(*refs[:n_in - 1], *refs[n_in:])
    return wrapped


def _dft_tables(n):
    k = jnp.arange(n, dtype=jnp.int32)
    ang = ((k[:, None] * k[None, :]) % (2 * n)).astype(F32) * (math.pi / n)
    return jnp.cos(ang).astype(BF16), jnp.sin(ang).astype(BF16)


def _hy_filter_kernel(w1t_ref, w1c_ref, w1s_ref, b1_ref, w2_ref, b2_ref, w3_ref, fr_ref,
                      bands_ref, deltas_ref, e_ref, d_ref, nyq_ref, *, n):
    pos_i = lax.broadcasted_iota(jnp.int32, (n, 1), 0)
    pos = pos_i.astype(F32)
    t = pos / (n - 1)
    ang = ((2.0 * math.pi) * pos / n) * bands_ref[...]
    pre = t * w1t_ref[...] + _dot_exact(jnp.cos(ang), w1c_ref[...]) - _dot_exact(jnp.sin(ang), w1s_ref[...])
    hdn = jnp.sin(fr_ref[0:1, :] * (pre + b1_ref[...]))
    hdn = jnp.sin(fr_ref[1:2, :] * (_dot_exact(hdn, w2_ref[...]) + b2_ref[...]))
    raw = _dot_exact(hdn, w3_ref[...])
    window = jnp.exp(-t * deltas_ref[...])
    hf = raw[:, :HY_CH] * window
    hb = jnp.where(pos_i > 0, raw[:, HY_CH:] * window, 0.0)
    inv = 1.0 / (jnp.sum(jnp.abs(hf), axis=0, keepdims=True) + jnp.sum(jnp.abs(hb), axis=0, keepdims=True))
    e = (hf + hb) * inv
    e_ref[...] = e.astype(BF16)
    d_ref[...] = ((hb - hf) * inv).astype(BF16)
    sign = (1 - 2 * (pos_i & 1)).astype(F32)
    nyq_ref[...] = jnp.sum(e * sign, axis=0, keepdims=True)


def _hyena_filter(n, w1, b1, w2, b2, w3, freq):
    bands = jnp.linspace(1e-4, HY_BANDS - 1, HY_BANDS, dtype=F32).reshape(1, HY_BANDS)
    max_decay = math.log(HY_DECAY_TARGET) / HY_FAST_DECAY_PCT
    min_decay = math.log(HY_DECAY_TARGET) / HY_SLOW_DECAY_PCT
    deltas = jnp.abs(jnp.linspace(min_decay, max_decay, HY_CH, dtype=F32)).reshape(1, HY_CH)
    args = [w1[0:1], w1[1:1 + HY_BANDS], w1[1 + HY_BANDS:], b1.reshape(1, HY_FFN), w2, b2.reshape(1, HY_FFN),
            w3, freq, bands, deltas]
    return pl.pallas_call(
        functools.partial(_hy_filter_kernel, n=n),
        out_shape=[jax.ShapeDtypeStruct((n, HY_CH), BF16), jax.ShapeDtypeStruct((n, HY_CH), BF16),
                   jax.ShapeDtypeStruct((1, HY_CH), F32)],
        compiler_params=pltpu.CompilerParams(vmem_limit_bytes=VMEM_LIMIT),
        name="hyena_filter",
    )(*args)


def _hy_gate_kernel(x0_ref, x1_ref, v_ref, w0_ref, w1_ref, wv_ref, b0_ref, b1_ref, bv_ref,
                    zb_ref, x0c_ref, zf_ref, nyq_ref, *, n):
    row = lax.broadcasted_iota(jnp.int32, (n, 1), 0)

    def conv(u_ref, w_ref, b_ref):
        u = u_ref[...]
        up = jnp.where(row > 0, pltpu.roll(u, 1, 0), 0.0)
        dn = jnp.where(row < n - 1, pltpu.roll(u, n - 1, 0), 0.0)
        return up * w_ref[0:1, :] + u * w_ref[1:2, :] + dn * w_ref[2:3, :] + b_ref[...]

    z = conv(x1_ref, w1_ref, b1_ref) * conv(v_ref, wv_ref, bv_ref)
    x0c_ref[...] = conv(x0_ref, w0_ref, b0_ref)
    zf_ref[...] = z
    zb_ref[...] = z.astype(BF16)
    sign = (1 - 2 * (row & 1)).astype(F32)
    nyq_ref[...] = jnp.sum(z * sign, axis=0, keepdims=True)


def _hyena_gate(bu, short_w, short_b, n, batch, row_blk0):
    halves = HY_CH // LANES
    sb = short_b.reshape(1, 3 * HY_CH)
    seg = lambda part: pl.BlockSpec((n, LANES), lambda b, j: (row_blk0 + b, part * halves + j))
    wsp = lambda part: pl.BlockSpec((3, LANES), lambda b, j: (0, part * halves + j))
    bsp = lambda part: pl.BlockSpec((1, LANES), lambda b, j: (0, part * halves + j))
    return pl.pallas_call(
        functools.partial(_hy_gate_kernel, n=n),
        grid=(batch, halves),
        in_specs=[seg(0), seg(1), seg(2), wsp(0), wsp(1), wsp(2), bsp(0), bsp(1), bsp(2)],
        out_specs=[pl.BlockSpec((n, LANES), lambda b, j: (0, b * halves + j)),
                   pl.BlockSpec((n, LANES), lambda b, j: (b, j)),
                   pl.BlockSpec((n, LANES), lambda b, j: (b, j)),
                   pl.BlockSpec((None, 1, LANES), lambda b, j: (b, 0, j))],
        out_shape=[jax.ShapeDtypeStruct((n, batch * HY_CH), BF16),
                   jax.ShapeDtypeStruct((batch * n, HY_CH), F32),
                   jax.ShapeDtypeStruct((batch * n, HY_CH), F32),
                   jax.ShapeDtypeStruct((batch, 1, HY_CH), F32)],
        compiler_params=_cparams(("arbitrary", "arbitrary")),
        name="hyena_gate",
    )(bu, bu, bu, short_w, short_w, short_w, sb, sb, sb)


def _hy_spectrum_kernel(c_ref, s_ref, z_ref, e_ref, d_ref, yr_ref, yi_ref, *, n, tk, batch):
    c = c_ref[...]
    s = s_ref[...]
    zr = _dot(c, z_ref[...])
    zs = _dot(s, z_ref[...])
    fr = _dot(c, e_ref[...])
    fi = _dot(s, d_ref[...])
    k = pl.program_id(0) * tk + lax.broadcasted_iota(jnp.int32, (tk, 1), 0)
    wk = jnp.where(k == 0, 1.0, 2.0) * (1.0 / (2 * n))
    for b in range(batch):
        sl = slice(b * HY_CH, (b + 1) * HY_CH)
        yr = zr[:, sl] * fr + zs[:, sl] * fi
        yi = zr[:, sl] * fi - zs[:, sl] * fr
        yr_ref[:, sl] = (yr * wk).astype(BF16)
        yi_ref[:, sl] = (-(yi * wk)).astype(BF16)


def _hy_inverse_kernel(c_ref, s_ref, yr_ref, yi_ref, x0c_ref, zf_ref, nyqz_ref, nyqf_ref, bias_ref,
                       *rest, n, tt, batch):
    o_ref, y_scr = rest[-2], rest[-1]
    b = pl.program_id(1)

    @pl.when(b == 0)
    def _():
        y = _dot(c_ref[...], yr_ref[...]) + _dot(s_ref[...], yi_ref[...])
        for bb in range(batch):
            y_scr[bb] = y[:, bb * HY_CH:(bb + 1) * HY_CH]

    t = pl.program_id(0) * tt + lax.broadcasted_iota(jnp.int32, (tt, 1), 0)
    sign = (1 - 2 * (t & 1)).astype(F32)
    nyq = nyqz_ref[...] * nyqf_ref[...] * (1.0 / (2 * n))
    zf = zf_ref[...]
    o_ref[...] = x0c_ref[...] * (y_scr[b] + sign * nyq + bias_ref[...] * zf)


def _hyena_conv(cs, zb, e, d, x0c, zf, nyqz, nyqf, bias, n, batch, nt, out_blk0, prev=None):
    c_tab, s_tab = cs
    bw = batch * HY_CH
    tk = min(512, n)
    whole = lambda shape: pl.BlockSpec(shape, lambda *_: (0,) * len(shape), pipeline_mode=pl.Buffered(1))
    yr, yi = pl.pallas_call(
        functools.partial(_hy_spectrum_kernel, n=n, tk=tk, batch=batch),
        grid=(n // tk,),
        in_specs=[pl.BlockSpec((tk, n), lambda i: (i, 0)), pl.BlockSpec((tk, n), lambda i: (i, 0)),
                  whole((n, bw)), whole((n, HY_CH)), whole((n, HY_CH))],
        out_specs=[pl.BlockSpec((tk, bw), lambda i: (i, 0))] * 2,
        out_shape=[jax.ShapeDtypeStruct((n, bw), BF16)] * 2,
        compiler_params=_cparams(("arbitrary",)),
        name="hyena_spectrum",
    )(c_tab, s_tab, zb, e, d)

    tt = min(512, n)
    n_t = n // tt
    seg = lambda i, b: (b * n_t + i, 0)
    in_specs = [pl.BlockSpec((tt, n), lambda i, b: (i, 0)), pl.BlockSpec((tt, n), lambda i, b: (i, 0)),
                whole((n, bw)), whole((n, bw)),
                pl.BlockSpec((tt, HY_CH), seg), pl.BlockSpec((tt, HY_CH), seg),
                pl.BlockSpec((None, 1, HY_CH), lambda i, b: (b, 0, 0)),
                pl.BlockSpec((1, HY_CH), lambda i, b: (0, 0)), pl.BlockSpec((1, HY_CH), lambda i, b: (0, 0))]
    args = [c_tab, s_tab, yr, yi, x0c, zf, nyqz, nyqf, bias.reshape(1, HY_CH)]
    aliases = {}
    if prev is not None:
        in_specs.append(pl.BlockSpec(memory_space=pl.ANY))
        args.append(prev)
        aliases = {len(args) - 1: 0}
    return pl.pallas_call(
        functools.partial(_hy_inverse_kernel, n=n, tt=tt, batch=batch),
        grid=(n_t, batch),
        in_specs=in_specs,
        out_specs=pl.BlockSpec((tt, HY_CH), lambda i, b: (out_blk0 + b * n_t + i, 0)),
        out_shape=jax.ShapeDtypeStruct((nt, HY_CH), F32),
        scratch_shapes=[pltpu.VMEM((batch, tt, HY_CH), F32)],
        input_output_aliases=aliases,
        compiler_params=_cparams(("arbitrary", "arbitrary")),
        name="hyena_inverse" if prev is None else "hyena_inverse_ctx",
    )(*args)


def _rec_direction(q, k, v, la, st_ref, o_ref, *, reverse, dk, dv, first):
    cc, width = q.shape
    r_i = lax.broadcasted_iota(jnp.int32, (cc, cc), 0)
    c_i = lax.broadcasted_iota(jnp.int32, (cc, cc), 1)
    incl = (c_i >= r_i) if reverse else (c_i <= r_i)
    cum = _dot_exact(incl.astype(F32), la)
    cum_end = cum[0:1] if reverse else cum[cc - 1:cc]

    @pl.when(first)
    def _():
        st_ref[...] = jnp.zeros_like(st_ref)

    qd = (q * jnp.exp(cum)).astype(BF16)
    kd = (k * jnp.exp(cum_end - cum)).astype(BF16)
    vb = v.astype(BF16)
    gain = jnp.exp(cum_end)
    inter = []
    for h in range(N_HEADS):
        ks = slice(h * dk, (h + 1) * dk)
        vs = slice(h * dv, (h + 1) * dv)
        st = st_ref[h]
        inter.append(_dot_nt(qd[:, ks], st.astype(BF16)))
        st_ref[h] = st * gain[:, ks] + _dot_tn(vb[:, vs], kd[:, ks])

    for j in range(cc // REC_SUB):
        r0, r1 = j * REC_SUB, (j + 1) * REC_SUB
        if reverse:
            ka, kb = r0, cc
            base = cum[r1:r1 + 1] if r1 < cc else jnp.zeros((1, width), F32)
        else:
            ka, kb = 0, r1
            base = cum[r0 - 1:r0] if r0 > 0 else jnp.zeros((1, width), F32)
        qj = (q[r0:r1] * jnp.exp(cum[r0:r1] - base)).astype(BF16)
        kj = (k[ka:kb] * jnp.exp(jnp.minimum(base - cum[ka:kb], EXP_CLAMP))).astype(BF16)
        rows = r0 + lax.broadcasted_iota(jnp.int32, (REC_SUB, kb - ka), 0)
        cols = ka + lax.broadcasted_iota(jnp.int32, (REC_SUB, kb - ka), 1)
        keep = (cols >= rows) if reverse else (cols <= rows)
        for h in range(N_HEADS):
            ks = slice(h * dk, (h + 1) * dk)
            vs = slice(h * dv, (h + 1) * dv)
            sc = jnp.where(keep, _dot_nt(qj[:, ks], kj[:, ks]), 0.0).astype(BF16)
            o_ref[r0:r1, vs] = _dot(sc, vb[ka:kb, vs]) + inter[h][r0:r1]


def _rec_kernel(*refs, mode, dk, dv, scale_q, scale_k):
    if mode == "gla":
        (qf_ref, kf_ref, vf_ref, lf_ref, qb_ref, kb_ref, vb_ref, lb_ref, gw_ref, gb_ref,
         of_ref, ob_ref, st_ref) = refs
    else:
        qf_ref, kf_ref, vf_ref, qb_ref, kb_ref, vb_ref, dec_ref, of_ref, ob_ref, st_ref = refs
    first = pl.program_id(1) == 0
    for d, (q_ref, k_ref, v_ref, o_ref) in enumerate(((qf_ref, kf_ref, vf_ref, of_ref),
                                                      (qb_ref, kb_ref, vb_ref, ob_ref))):
        q = q_ref[...] * scale_q
        k = k_ref[...] * scale_k
        if mode == "gla":
            l_ref = lf_ref if d == 0 else lb_ref
            logit = _dot_exact(l_ref[...], gw_ref[d]) + gb_ref[d]
            la = (jnp.minimum(logit, 0.0) - jnp.log(1.0 + jnp.exp(-jnp.abs(logit)))) * (1.0 / GLA_GATE_NORM)
        else:
            la = jnp.broadcast_to(-jnp.exp(dec_ref[d]), q.shape)
        _rec_direction(q, k, v_ref[...], la, st_ref.at[d], o_ref, reverse=(d == 1), dk=dk, dv=dv, first=first)


def _linear_recurrence(mode, q, k, v, extra, geo, dk, dv, scale_q, scale_k):
    batch, seq, ctx = geo["batch"], geo["seq"], geo["ctx"]
    nt = q.shape[0]
    cc = REC_CHUNK
    n_cc, n_lc = ctx // cc, seq // cc
    ctx0 = (batch * seq) // cc

    def fwd(b, i):
        return (jnp.where(i < n_cc, ctx0 + b * n_cc + i, b * n_lc + i - n_cc), 0)

    def bwd(b, i):
        return (jnp.where(i < n_cc, ctx0 + b * n_cc + (n_cc - 1 - i), b * n_lc + (n_lc - 1 - (i - n_cc))), 0)

    wq, wv = N_HEADS * dk, N_HEADS * dv
    blk = lambda w, m: pl.BlockSpec((cc, w), m)
    if mode == "gla":
        glr, gw, gb = extra
        in_specs = [blk(wq, fwd), blk(wq, fwd), blk(wv, fwd), blk(LANES, fwd),
                    blk(wq, bwd), blk(wq, bwd), blk(wv, bwd), blk(LANES, bwd),
                    pl.BlockSpec((2, LANES, wq), lambda b, i: (0, 0, 0)),
                    pl.BlockSpec((2, 1, wq), lambda b, i: (0, 0, 0))]
        args = [q, k, v, glr, q, k, v, glr, gw, gb]
    else:
        (dec,) = extra
        in_specs = [blk(wq, fwd), blk(wq, fwd), blk(wv, fwd), blk(wq, bwd), blk(wq, bwd), blk(wv, bwd),
                    pl.BlockSpec((2, 1, wq), lambda b, i: (0, 0, 0))]
        args = [q, k, v, q, k, v, dec]
    return pl.pallas_call(
        functools.partial(_rec_kernel, mode=mode, dk=dk, dv=dv, scale_q=scale_q, scale_k=scale_k),
        grid=(batch, n_cc + n_lc),
        in_specs=in_specs,
        out_specs=[blk(wv, fwd), blk(wv, bwd)],
        out_shape=[jax.ShapeDtypeStruct((nt, wv), F32)] * 2,
        scratch_shapes=[pltpu.VMEM((2, N_HEADS, dv, dk), F32)],
        compiler_params=_cparams(("arbitrary", "arbitrary")),
        name="recurrence_" + mode,
    )(*args)


def _mixout_kernel(x_ref, mod_ref, at_ref, hy_ref, rf_ref, rb_ref, cg_ref, gf_ref, gb_ref, dr_ref,
                   seg_ref, rg_ref, gg_ref, wo_ref, g1_ref, g2_ref, rw_ref, rbias_ref,
                   x1_ref, h2_ref, ti_ref, tw_ref, rk_ref, cnt_ref):
    @pl.when(pl.program_id(0) == 0)
    def _():
        cnt_ref[...] = jnp.zeros_like(cnt_ref)

    def head_norm(o):
        ms = _dot_exact(o * o, seg_ref[...]) * (1.0 / RET_DH)
        return o * lax.rsqrt(ms + EPS)

    a = jnp.concatenate([at_ref[h] for h in range(N_HEADS)], axis=-1)
    cg = cg_ref[...]
    dr = dr_ref[...]
    rt = head_norm(rf_ref[...] + rb_ref[...]) * rg_ref[...] * (cg * _sigmoid(cg))
    gl = head_norm(gf_ref[...] + gb_ref[...]) * gg_ref[...] * (dr * _sigmoid(dr))
    cat = jnp.concatenate([a, hy_ref[...], rt, gl], axis=-1).astype(BF16)
    y = _dot(cat, wo_ref[...])
    x1 = x_ref[...] + mod_ref[2:3, :] * (_rms(y) * g1_ref[...])
    x1_ref[...] = x1
    h2 = _rms(x1) * g2_ref[...] * (1.0 + mod_ref[4:5, :]) + mod_ref[3:4, :]
    h2_ref[...] = h2

    vals = _dot_exact(h2, rw_ref[...]) + rbias_ref[...]
    tm = vals.shape[0]
    lane = lax.broadcasted_iota(jnp.int32, vals.shape, 1)
    idx_out = jnp.zeros(vals.shape, jnp.int32)
    w_out = jnp.zeros(vals.shape, F32)
    top = None
    den = 0.0
    picks = []
    for r in range(TOP_K):
        m = jnp.max(vals, axis=-1, keepdims=True)
        idx = jnp.min(jnp.where(vals == m, lane, LANES), axis=-1, keepdims=True)
        hit = lane == idx
        vals = jnp.where(hit, -jnp.inf, vals)
        top = m if top is None else top
        e = jnp.exp(m - top)
        den = den + e
        picks.append((idx, e, hit))
    inv = 1.0 / den
    for r, (idx, e, _) in enumerate(picks):
        idx_out = jnp.where(lane == r, idx, idx_out)
        w_out = jnp.where(lane == r, e * inv, w_out)
    ti_ref[...] = idx_out
    tw_ref[...] = w_out

    chosen = functools.reduce(jnp.logical_or, [hit for _, _, hit in picks])
    chosen_f = jnp.where(chosen, 1.0, 0.0)
    earlier = (lax.broadcasted_iota(jnp.int32, (tm, tm), 1) < lax.broadcasted_iota(jnp.int32, (tm, tm), 0))
    before = _dot(jnp.where(earlier, 1.0, 0.0).astype(BF16), chosen_f.astype(BF16)) + cnt_ref[...]
    rk_out = jnp.zeros(vals.shape, jnp.int32)
    for r, (_, _, hit) in enumerate(picks):
        rank = jnp.sum(jnp.where(hit, before, 0.0), axis=-1, keepdims=True)
        rk_out = jnp.where(lane == r, rank.astype(jnp.int32), rk_out)
    rk_ref[...] = rk_out
    cnt_ref[...] = cnt_ref[...] + jnp.sum(chosen_f, axis=0, keepdims=True)


def _mix_out(n_tiles, x, mod, at, hy, rf, rb, cg, gf, gb, dr, seg, rg, gg, wo, g1, g2, rw, rbias, geo):
    nt, d = x.shape
    tm = ROW_TILE
    lat_tiles, per_batch, batch = geo["lat_tiles"], geo["tiles_per_batch"], geo["batch"]

    def mod_map(i):
        return (jnp.where(i < lat_tiles, i // per_batch, batch), 0, 0)

    row = lambda i: (i, 0)
    const = lambda i: (0, 0)
    g_blk = pl.BlockSpec((tm, GROUP_W), row)
    return pl.pallas_call(
        _mixout_kernel,
        grid=(n_tiles,),
        in_specs=[pl.BlockSpec((tm, d), row), pl.BlockSpec((None, 6, d), mod_map),
                  pl.BlockSpec((N_HEADS, tm, DIFF_DV), lambda i: (0, i, 0)),
                  g_blk, g_blk, g_blk, g_blk, g_blk, g_blk, g_blk,
                  pl.BlockSpec((GROUP_W, GROUP_W), const), pl.BlockSpec((1, GROUP_W), const),
                  pl.BlockSpec((1, GROUP_W), const),
                  pl.BlockSpec((d, d), const, pipeline_mode=pl.Buffered(1)),
                  pl.BlockSpec((1, d), const), pl.BlockSpec((1, d), const),
                  pl.BlockSpec((d, LANES), const), pl.BlockSpec((1, LANES), const)],
        out_specs=[pl.BlockSpec((tm, d), row), pl.BlockSpec((tm, d), row),
                   pl.BlockSpec((tm, LANES), row), pl.BlockSpec((tm, LANES), row),
                   pl.BlockSpec((tm, LANES), row), pl.BlockSpec((1, LANES), const)],
        out_shape=[jax.ShapeDtypeStruct((nt, d), F32), jax.ShapeDtypeStruct((nt, d), F32),
                   jax.ShapeDtypeStruct((nt, LANES), jnp.int32), jax.ShapeDtypeStruct((nt, LANES), F32),
                   jax.ShapeDtypeStruct((nt, LANES), jnp.int32), jax.ShapeDtypeStruct((1, LANES), F32)],
        compiler_params=_cparams(("arbitrary",)),
        name="mix_out_router",
    )(x, mod, at, hy, rf, rb, cg, gf, gb, dr, seg, rg, gg, wo, g1, g2, rw, rbias)


def _expert_kernel(te_ref, na_ref, xs_ref, wg_ref, wl_ref, bg_ref, bl_ref, w2_ref, b2_ref, ys_ref):
    active = pl.program_id(0) < na_ref[0]

    @pl.when(active)
    def _():
        x = xs_ref[...].astype(BF16)
        glu = jnp.minimum(_dot_nt(x, wg_ref[...]) + bg_ref[...], SWIGLU_LIMIT)
        lin = jnp.clip(_dot_nt(x, wl_ref[...]) + bl_ref[...], -SWIGLU_LIMIT, SWIGLU_LIMIT)
        act = glu * _sigmoid(SWIGLU_ALPHA * glu) * (lin + 1.0)
        ys_ref[...] = _dot(act.astype(BF16), w2_ref[...].astype(BF16)) + b2_ref[...]

    @pl.when(jnp.logical_not(active))
    def _():
        ys_ref[...] = jnp.zeros_like(ys_ref)


def _expert_ffn(tile_e, n_active, xs, wg, wl, bg, bl, w2, b2):
    p, d = xs.shape
    tm = MOE_TILE
    de = wg.shape[1]
    wmap = lambda i, te, na: (te[i], 0, 0)
    grid_spec = pltpu.PrefetchScalarGridSpec(
        num_scalar_prefetch=2,
        grid=(p // tm,),
        in_specs=[pl.BlockSpec((tm, d), lambda i, te, na: (i, 0)),
                  pl.BlockSpec((None, de, d), wmap), pl.BlockSpec((None, de, d), wmap),
                  pl.BlockSpec((None, 1, de), wmap), pl.BlockSpec((None, 1, de), wmap),
                  pl.BlockSpec((None, de, d), wmap), pl.BlockSpec((None, 1, d), wmap)],
        out_specs=pl.BlockSpec((tm, d), lambda i, te, na: (i, 0)),
    )
    return pl.pallas_call(
        _expert_kernel,
        grid_spec=grid_spec,
        out_shape=jax.ShapeDtypeStruct((p, d), F32),
        compiler_params=_cparams(("arbitrary",)),
        name="expert_ffn",
    )(tile_e, n_active, xs, wg, wl, bg, bl, w2, b2)


def _route_plan(topi, rank, counts, tm):
    n, k = topi.shape
    padded = ((counts + tm - 1) // tm) * tm
    pend = jnp.cumsum(padded)
    pstart = pend - padded
    experts = jnp.arange(N_EXPERTS, dtype=jnp.int32)
    start = jnp.sum(jnp.where(topi[:, :, None] == experts, pstart, 0), axis=-1)
    dest = (start + rank).astype(jnp.int32)
    p = n * k + N_EXPERTS * tm
    tile_start = jnp.arange(p // tm, dtype=jnp.int32) * tm
    tile_e = jnp.minimum(jnp.sum((tile_start[:, None] >= pend[None, :]).astype(jnp.int32), axis=1), N_EXPERTS - 1)
    n_active = (pend[-1:] // tm).astype(jnp.int32)
    return dest, tile_e, n_active, p


def _row_copy_kernel(si_ref, di_ref, src_ref, *rest, rows):
    dst_ref, sem = rest[-2], rest[-1]
    step = pl.program_id(0)

    def row_copy(s, t):
        return pltpu.make_async_copy(src_ref.at[pl.ds(s, 1)], dst_ref.at[pl.ds(t, 1)], sem)

    def drain():
        pltpu.make_async_copy(src_ref.at[pl.ds(0, rows)], dst_ref.at[pl.ds(0, rows)], sem).wait()

    def issue(r, carry):
        row_copy(si_ref[0, r], di_ref[0, r]).start()
        return carry

    lax.fori_loop(0, rows, issue, 0, unroll=8)

    @pl.when(step > 0)
    def _():
        drain()

    @pl.when(step == pl.num_programs(0) - 1)
    def _():
        drain()


def _row_copy(src_idx, dst_idx, src, n_dst, init=None):
    rows = 512
    n = src_idx.shape[0]
    steps = n // rows
    idx_spec = pl.BlockSpec((None, 1, rows), lambda i: (i, 0, 0), memory_space=pltpu.SMEM)
    in_specs = [idx_spec, idx_spec, pl.BlockSpec(memory_space=pl.ANY)]
    args = [src_idx.reshape(steps, 1, rows), dst_idx.reshape(steps, 1, rows), src]
    aliases = {}
    if init is not None:
        in_specs.append(pl.BlockSpec(memory_space=pl.ANY))
        args.append(init)
        aliases = {3: 0}
    return pl.pallas_call(
        functools.partial(_row_copy_kernel, rows=rows),
        grid=(steps,),
        in_specs=in_specs,
        out_specs=pl.BlockSpec(memory_space=pl.ANY),
        out_shape=jax.ShapeDtypeStruct((n_dst, src.shape[1]), src.dtype),
        scratch_shapes=[pltpu.SemaphoreType.DMA(())],
        input_output_aliases=aliases,
        compiler_params=pltpu.CompilerParams(dimension_semantics=("arbitrary",), has_side_effects=True),
        name="row_copy",
    )(*args)


def _combine_kernel(x_ref, y4_ref, tw_ref, mod_ref, g_ref, o_ref):
    d = x_ref.shape[1]
    y = None
    for k in range(TOP_K):
        t = y4_ref[:, k * d:(k + 1) * d] * tw_ref[:, k:k + 1]
        y = t if y is None else y + t
    o_ref[...] = x_ref[...] + mod_ref[5:6, :] * (_rms(y) * g_ref[...])


def _combine_residual(n_tiles, x, y4, topw, mod, g, geo):
    nt, d = x.shape
    tm = ROW_TILE
    lat_tiles, per_batch, batch = geo["lat_tiles"], geo["tiles_per_batch"], geo["batch"]

    def mod_map(i):
        return (jnp.where(i < lat_tiles, i // per_batch, batch), 0, 0)

    row = lambda i: (i, 0)
    return pl.pallas_call(
        _combine_kernel,
        grid=(n_tiles,),
        in_specs=[pl.BlockSpec((tm, d), row), pl.BlockSpec((tm, TOP_K * d), row), pl.BlockSpec((tm, LANES), row),
                  pl.BlockSpec((None, 6, d), mod_map), pl.BlockSpec((1, d), lambda i: (0, 0))],
        out_specs=pl.BlockSpec((tm, d), row),
        out_shape=jax.ShapeDtypeStruct((nt, d), F32),
        compiler_params=_cparams(("arbitrary",)),
        name="combine_residual",
    )(x, y4, topw, mod, g)


def _rope_partner(dim):
    h, q = dim // 2, dim // 4
    perm = np.zeros(dim, np.int32)
    sign = np.zeros(dim, np.float32)
    for base in (0, h):
        for j in range(q):
            perm[base + j], sign[base + j] = base + j + q, -1.0
            perm[base + q + j], sign[base + q + j] = base + j, 1.0
    return perm, sign


def _rot_cols(w, dim):
    perm, sign = _rope_partner(dim)
    reps = w.shape[1] // dim
    full_perm = np.concatenate([perm + r * dim for r in range(reps)])
    return w[:, full_perm] * jnp.asarray(np.tile(sign, reps))


def _rope_tables(n_tok, dim, reps, pad_rows):
    rows = n_tok // GRID_W
    row = jnp.repeat(jnp.arange(rows, dtype=F32), GRID_W)
    col = jnp.tile(jnp.arange(GRID_W, dtype=F32), rows)
    quarter = dim // 4
    inv = ROPE_BASE ** (-jnp.arange(quarter, dtype=F32) / quarter)
    ar = row[:, None] * inv[None]
    ac = col[:, None] * inv[None]
    ang = jnp.concatenate([ar, ar, ac, ac], axis=-1)
    cos = jnp.concatenate([jnp.tile(jnp.cos(ang), (1, reps)), jnp.ones((pad_rows, dim * reps), F32)])
    sin = jnp.concatenate([jnp.tile(jnp.sin(ang), (1, reps)), jnp.zeros((pad_rows, dim * reps), F32)])
    return cos, sin


def _widen_w_in(w):
    sizes = (256, 256, 256, 768, 256, 256, 256, 256, 128, 128, 256, 256, 32)
    cuts = np.cumsum(sizes)[:-1].tolist()
    aq, ak, av, bu, cq, ck, cv, cg, dq, dk, dv, dr, dl = jnp.split(w, cuts, axis=1)
    dl = jnp.pad(dl, ((0, 0), (0, LANES - dl.shape[1])))
    parts = [aq, _rot_cols(aq, DIFF_DQK), ak, _rot_cols(ak, DIFF_DQK), av, bu,
             cq, _rot_cols(cq, RET_DH), ck, _rot_cols(ck, RET_DH), cv, cg, dq, dk, dv, dr, dl]
    return jnp.concatenate(parts, axis=1).astype(BF16)


def kernel(x, c, ctx, c_ctx, w_mod, b_mod, norm_g, w_in, w_out, diff_lambda, diff_subln_g, hy_short_w, hy_short_b, hy_w1, hy_b1, hy_w2, hy_b2, hy_w3, hy_freq, hy_bias, ret_decay, ret_norm_g, gla_gate_w, gla_gate_b, gla_norm_g, router_w, router_b, exp_w1, exp_b1, exp_w2, exp_b2):
    batch, seq, d = x.shape
    n_ctx = ctx.shape[1]
    depth = w_mod.shape[0]
    n_lat_rows, n_ctx_rows = batch * seq, batch * n_ctx
    nt = n_lat_rows + n_ctx_rows
    assert d == D_MODEL and seq % ROW_TILE == 0 and n_ctx % ROW_TILE == 0 and seq % n_ctx == 0
    geo = dict(batch=batch, seq=seq, ctx=n_ctx, lat_tiles=n_lat_rows // ROW_TILE,
               tiles_per_batch=seq // ROW_TILE)

    xs = jnp.concatenate([x.reshape(n_lat_rows, d), ctx.reshape(n_ctx_rows, d)], axis=0)
    mod_rows = 8
    cc = jnp.zeros((mod_rows, d), F32).at[:batch].set(c).at[batch].set(c_ctx)
    mod_all = _modulation(cc, w_mod, b_mod).reshape(depth, mod_rows, 6, d)

    rope = (*_rope_tables(seq, DIFF_DQK, GROUP_W // DIFF_DQK, ROW_TILE),
            *_rope_tables(seq, RET_DH, GROUP_W // RET_DH, ROW_TILE))
    dft_lat = _dft_tables(seq)
    dft_ctx = _dft_tables(n_ctx)
    seg = jnp.asarray(np.kron(np.eye(N_HEADS, dtype=np.float32), np.ones((RET_DH, RET_DH), np.float32)))

    for l in range(depth):
        need_ctx = l < depth - 1
        lam_init = 0.8 - 0.6 * math.exp(-0.3 * l)
        mod = mod_all[l]
        (aq, ak, av, bu, cq, ck, cv, cg, dq, dk, dv, dr, dl) = _in_projection(
            xs, mod, norm_g[l, 0].reshape(1, d), _widen_w_in(w_in[l]), rope, geo)

        sub_g = diff_subln_g[l].reshape(1, DIFF_DV)
        at = _diff_attention(diff_lambda[l], sub_g, aq, ak, av, geo, lam_init)
        if need_ctx:
            at = _diff_attention(diff_lambda[l], sub_g, aq, ak, av, geo, lam_init, prev=at)

        def hyena(n, row_blk0, tables, out_blk0, prev):
            e, dd, nyqf = _hyena_filter(n, hy_w1[l], hy_b1[l], hy_w2[l], hy_b2[l], hy_w3[l], hy_freq[l])
            zb, x0c, zf, nyqz = _hyena_gate(bu, hy_short_w[l], hy_short_b[l], n, batch, row_blk0)
            return _hyena_conv(tables, zb, e, dd, x0c, zf, nyqz, nyqf, hy_bias[l], n, batch, nt, out_blk0, prev)

        hy = hyena(seq, 0, dft_lat, 0, None)
        if need_ctx:
            hy = hyena(n_ctx, n_lat_rows // n_ctx, dft_ctx, n_lat_rows // min(512, n_ctx), hy)

        dec = jnp.repeat(ret_decay[l], RET_DH, axis=-1).reshape(2, 1, N_HEADS * RET_DH)
        rf, rb = _linear_recurrence("ret", cq, ck, cv, (dec,), geo, RET_DH, RET_DH, 1.0, RET_DH ** -0.5)

        gw = jnp.zeros((2, LANES, N_HEADS * GLA_DK), F32)
        gw = gw.at[0, :GLA_RANK].set(gla_gate_w[l, 0]).at[1, GLA_RANK:2 * GLA_RANK].set(gla_gate_w[l, 1])
        gbias = gla_gate_b[l].reshape(2, 1, N_HEADS * GLA_DK)
        gf, gb = _linear_recurrence("gla", dq, dk, dv, (dl, gw, gbias), geo, GLA_DK, GLA_DV, GLA_DK ** -0.5, 1.0)

        n_tiles = nt // ROW_TILE if need_ctx else n_lat_rows // ROW_TILE
        rw = jnp.pad(router_w[l], ((0, 0), (0, LANES - N_EXPERTS)))
        rbias = jnp.pad(router_b[l], (0, LANES - N_EXPERTS), constant_values=-jnp.inf).reshape(1, LANES)
        x1, h2, topi, topw, rank, cnt = _mix_out(
            n_tiles, xs, mod, at, hy, rf, rb, cg, gf, gb, dr, seg,
            ret_norm_g[l].reshape(1, GROUP_W), jnp.tile(gla_norm_g[l], N_HEADS).reshape(1, GROUP_W),
            w_out[l].astype(BF16), norm_g[l, 1].reshape(1, d), norm_g[l, 2].reshape(1, d), rw, rbias, geo)

        n_tok = n_tiles * ROW_TILE
        dest, tile_e, n_active, p_rows = _route_plan(
            topi[:n_tok, :TOP_K], rank[:n_tok, :TOP_K], cnt[0, :N_EXPERTS].astype(jnp.int32), MOE_TILE)
        flat_dest = dest.reshape(-1)
        slots = jnp.arange(n_tok * TOP_K, dtype=jnp.int32)
        dispatched = _row_copy(slots // TOP_K, flat_dest, h2, p_rows, init=jnp.zeros((p_rows, d), F32))
        w1t = jnp.swapaxes(exp_w1[l], 1, 2).reshape(N_EXPERTS, D_EXPERT, 2, d)
        b1 = exp_b1[l].reshape(N_EXPERTS, 1, D_EXPERT, 2)
        ys = _expert_ffn(tile_e, n_active, dispatched,
                         w1t[:, :, 0, :].astype(BF16), w1t[:, :, 1, :].astype(BF16),
                         b1[..., 0], b1[..., 1], exp_w2[l], exp_b2[l][:, None, :])
        y4 = _row_copy(flat_dest, slots, ys, n_tok * TOP_K).reshape(n_tok, TOP_K * d)
        xs = _combine_residual(n_tiles, x1, y4, topw, mod, norm_g[l, 3].reshape(1, d), geo)

    return xs[:n_lat_rows].reshape(batch, seq, d)
```

```python
import functools
import math

import numpy as np
import jax
import jax.numpy as jnp
from jax import lax
from jax.experimental import pallas as pl
from jax.experimental.pallas import tpu as pltpu

F32 = jnp.float32
BF16 = jnp.bfloat16
HIGHEST = lax.Precision.HIGHEST

D_MODEL = 1024
GRID_W = 64
GROUP_W = 256
N_HEADS = 4
DIFF_DQK = 32
DIFF_DV = 64
ROPE_BASE = 10000.0
HY_CH = 256
HY_BANDS = 16
HY_FFN = 64
HY_FAST_DECAY_PCT = 0.3
HY_SLOW_DECAY_PCT = 1.5
HY_DECAY_TARGET = 1e-2
RET_DH = 64
GLA_DK = 32
GLA_DV = 64
GLA_RANK = 16
GLA_GATE_NORM = 16.0
N_EXPERTS = 32
TOP_K = 4
D_EXPERT = 1024
SWIGLU_LIMIT = 7.0
SWIGLU_ALPHA = 1.702
EPS = 1e-6

LANES = 128
ROW_TILE = 256
REC_CHUNK = 64
RET_CHUNK = 256
REC_SUB = 16
EXP_CLAMP = 80.0
MOE_TILE = 256
VMEM_LIMIT = 52 * 1024 * 1024

_A_Q, _A_QR, _A_K, _A_KR, _A_V = 0, 256, 512, 768, 1024
_B_U = 1280
_C_Q, _C_QR, _C_K, _C_KR, _C_V, _C_G = 2048, 2304, 2560, 2816, 3072, 3328
_D_Q, _D_K, _D_V, _D_R, _D_L = 3584, 3712, 3840, 4096, 4352
_IN_COLS = 4480


def _cparams(sem):
    return pltpu.CompilerParams(dimension_semantics=sem, vmem_limit_bytes=VMEM_LIMIT)


def _sigmoid(x):
    return 1.0 / (1.0 + jnp.exp(-x))


def _rms(x):
    return x * lax.rsqrt(jnp.mean(x * x, axis=-1, keepdims=True) + EPS)


def _dot(a, b):
    return jnp.dot(a, b, preferred_element_type=F32)


def _dot_exact(a, b):
    return jnp.dot(a, b, preferred_element_type=F32, precision=HIGHEST)


def _dot_nt(a, b):
    return lax.dot_general(a, b, (((1,), (1,)), ((), ())), preferred_element_type=F32)


def _dot_tn(a, b):
    return lax.dot_general(a, b, (((0,), (0,)), ((), ())), preferred_element_type=F32)


def _mod_kernel(c_ref, w_ref, b_ref, o_ref):
    c = c_ref[...]
    o_ref[...] = _dot_exact(c * _sigmoid(c), w_ref[...]) + b_ref[...]


def _modulation(cc, w_mod, b_mod):
    depth, d, n = w_mod.shape
    tn = n // 4
    rows = cc.shape[0]
    return pl.pallas_call(
        _mod_kernel,
        grid=(depth, n // tn),
        in_specs=[pl.BlockSpec((rows, d), lambda l, j: (0, 0)),
                  pl.BlockSpec((None, d, tn), lambda l, j: (l, 0, j)),
                  pl.BlockSpec((None, 1, tn), lambda l, j: (l, 0, j))],
        out_specs=pl.BlockSpec((None, rows, tn), lambda l, j: (l, 0, j)),
        out_shape=jax.ShapeDtypeStruct((depth, rows, n), F32),
        compiler_params=_cparams(("arbitrary", "arbitrary")),
        name="modulation",
    )(cc, w_mod, b_mod.reshape(depth, 1, n))


def _inproj_kernel(x_ref, mod_ref, g_ref, w_ref, cosa_ref, sina_ref, cosc_ref, sinc_ref,
                   aq_ref, ak_ref, av_ref, bu_ref, cq_ref, ck_ref, cv_ref, cg_ref,
                   dq_ref, dk_ref, dv_ref, dr_ref, dl_ref):
    xn = _rms(x_ref[...]) * g_ref[...]
    h = (xn * (1.0 + mod_ref[1:2, :]) + mod_ref[0:1, :]).astype(BF16)

    def proj(a, width):
        return _dot(h, w_ref[:, a:a + width])

    def roped(a, a_rot, cos_ref, sin_ref):
        return proj(a, GROUP_W) * cos_ref[...] + proj(a_rot, GROUP_W) * sin_ref[...]

    aq = roped(_A_Q, _A_QR, cosa_ref, sina_ref) * (DIFF_DQK ** -0.5)
    ak = roped(_A_K, _A_KR, cosa_ref, sina_ref)
    av = proj(_A_V, GROUP_W)
    ones_col = jnp.where(lax.broadcasted_iota(jnp.int32, (av.shape[0], LANES - DIFF_DV), 1) == 0, 1.0, 0.0)
    for hd in range(N_HEADS):
        sl = slice(hd * DIFF_DV, (hd + 1) * DIFF_DV)
        aq_ref[hd] = aq[:, sl].astype(BF16)
        ak_ref[hd] = ak[:, sl].astype(BF16)
        av_ref[hd] = jnp.concatenate([av[:, sl], ones_col], axis=-1).astype(BF16)
    bu_ref[...] = proj(_B_U, 3 * HY_CH)
    cq_ref[...] = roped(_C_Q, _C_QR, cosc_ref, sinc_ref)
    ck_ref[...] = roped(_C_K, _C_KR, cosc_ref, sinc_ref)
    cv_ref[...] = proj(_C_V, GROUP_W)
    cg_ref[...] = proj(_C_G, GROUP_W)
    dq_ref[...] = proj(_D_Q, LANES)
    dk_ref[...] = proj(_D_K, LANES)
    dv_ref[...] = proj(_D_V, GROUP_W)
    dr_ref[...] = proj(_D_R, GROUP_W)
    dl_ref[...] = proj(_D_L, LANES)


def _in_projection(x, mod, g, w_wide, rope, geo):
    nt, d = x.shape
    tm = ROW_TILE
    lat_tiles, per_batch, batch = geo["lat_tiles"], geo["tiles_per_batch"], geo["batch"]

    def mod_map(i):
        return (jnp.where(i < lat_tiles, i // per_batch, batch), 0, 0)

    def rope_map(i):
        return (jnp.where(i < lat_tiles, i % per_batch, per_batch), 0)

    row = lambda i: (i, 0)
    head = lambda i: (0, i, 0)
    const = lambda i: (0, 0)
    f32_out = lambda w: jax.ShapeDtypeStruct((nt, w), F32)
    head_out = lambda w: jax.ShapeDtypeStruct((N_HEADS, nt, w), BF16)
    head_spec = lambda w: pl.BlockSpec((N_HEADS, tm, w), head)
    widths = [3 * HY_CH, GROUP_W, GROUP_W, GROUP_W, GROUP_W, LANES, LANES, GROUP_W, GROUP_W, LANES]
    return pl.pallas_call(
        _inproj_kernel,
        grid=(nt // tm,),
        in_specs=[pl.BlockSpec((tm, d), row),
                  pl.BlockSpec((None, 6, d), mod_map),
                  pl.BlockSpec((1, d), const),
                  pl.BlockSpec((d, _IN_COLS), const, pipeline_mode=pl.Buffered(1)),
                  pl.BlockSpec((tm, GROUP_W), rope_map), pl.BlockSpec((tm, GROUP_W), rope_map),
                  pl.BlockSpec((tm, GROUP_W), rope_map), pl.BlockSpec((tm, GROUP_W), rope_map)],
        out_specs=[head_spec(DIFF_DV), head_spec(DIFF_DV), head_spec(LANES)]
                  + [pl.BlockSpec((tm, w), row) for w in widths],
        out_shape=[head_out(DIFF_DV), head_out(DIFF_DV), head_out(LANES)] + [f32_out(w) for w in widths],
        compiler_params=_cparams(("arbitrary",)),
        name="in_projection",
    )(x, mod, g, w_wide, *rope)


def _attn_kernel(lam_ref, g_ref, q_ref, *rest, lam_init, has_lat):
    if has_lat:
        kl_ref, vl_ref, kc_ref, vc_ref, o_ref = rest
        keys = [(kl_ref, vl_ref), (kc_ref, vc_ref)]
    else:
        kc_ref, vc_ref, o_ref = rest
        keys = [(kc_ref, vc_ref)]
    lp = lam_ref[...]
    lam = (jnp.exp(jnp.sum(lp[0:1] * lp[1:2], axis=-1, keepdims=True))
           - jnp.exp(jnp.sum(lp[2:3] * lp[3:4], axis=-1, keepdims=True)) + lam_init)
    q = q_ref[...]
    maps = []
    for m in range(2):
        sl = slice(m * DIFF_DQK, (m + 1) * DIFF_DQK)
        s = [_dot_nt(q[:, sl], k_ref[:, sl]) for k_ref, _ in keys]
        mx = functools.reduce(jnp.maximum, [jnp.max(t, axis=-1, keepdims=True) for t in s])
        acc = None
        for t, (_, v_ref) in zip(s, keys):
            part = _dot(jnp.exp(t - mx).astype(BF16), v_ref[...])
            acc = part if acc is None else acc + part
        maps.append(acc[:, :DIFF_DV] * (1.0 / acc[:, DIFF_DV:DIFF_DV + 1]))
    o = maps[0] - lam * maps[1]
    o_ref[...] = _rms(o) * g_ref[...] * (1.0 - lam_init)


def _diff_attention(lam_p, subln_g, aq, ak, av, geo, lam_init, prev=None):
    batch, seq, ctx = geo["batch"], geo["seq"], geo["ctx"]
    nt = aq.shape[1]
    has_lat = prev is None
    tq = ROW_TILE
    n_q = (seq if has_lat else ctx) // tq
    q_off = 0 if has_lat else (batch * seq) // tq
    ctx_blk0 = (batch * seq) // ctx

    qmap = lambda b, h, i: (h, q_off + b * n_q + i, 0)
    lat_map = lambda b, h, i: (h, b, 0)
    ctx_map = lambda b, h, i: (h, ctx_blk0 + b, 0)
    const = lambda b, h, i: (0, 0)
    in_specs = [pl.BlockSpec((4, DIFF_DQK), const), pl.BlockSpec((1, DIFF_DV), const),
                pl.BlockSpec((None, tq, DIFF_DV), qmap)]
    args = [lam_p, subln_g, aq]
    if has_lat:
        in_specs += [pl.BlockSpec((None, seq, DIFF_DV), lat_map), pl.BlockSpec((None, seq, LANES), lat_map)]
        args += [ak, av]
    in_specs += [pl.BlockSpec((None, ctx, DIFF_DV), ctx_map), pl.BlockSpec((None, ctx, LANES), ctx_map)]
    args += [ak, av]
    aliases = {}
    if not has_lat:
        in_specs.append(pl.BlockSpec(memory_space=pl.ANY))
        args.append(prev)
        aliases = {len(args) - 1: 0}
    kern = functools.partial(_attn_kernel, lam_init=lam_init, has_lat=has_lat)
    if not has_lat:
        kern = _drop_last_input(kern, n_in=len(args))
    return pl.pallas_call(
        kern,
        grid=(batch, N_HEADS, n_q),
        in_specs=in_specs,
        out_specs=pl.BlockSpec((None, tq, DIFF_DV), qmap),
        out_shape=jax.ShapeDtypeStruct((N_HEADS, nt, DIFF_DV), F32),
        input_output_aliases=aliases,
        compiler_params=_cparams(("arbitrary", "arbitrary", "arbitrary")),
        name="diff_attention" if has_lat else "diff_attention_ctx",
    )(*args)


def _drop_last_input(kern, n_in):
    def wrapped(*refs):
        return kern(*refs[:n_in - 1], *refs[n_in:])
    return wrapped


def _dft_tables(n):
    k = jnp.arange(n, dtype=jnp.int32)
    ang = ((k[:, None] * k[None, :]) % (2 * n)).astype(F32) * (math.pi / n)
    return jnp.cos(ang).astype(BF16), jnp.sin(ang).astype(BF16)


def _hy_filter_kernel(w1t_ref, w1c_ref, w1s_ref, b1_ref, w2_ref, b2_ref, w3_ref, fr_ref,
                      bands_ref, deltas_ref, e_ref, d_ref, nyq_ref, *, n):
    pos_i = lax.broadcasted_iota(jnp.int32, (n, 1), 0)
    pos = pos_i.astype(F32)
    t = pos / (n - 1)
    ang = ((2.0 * math.pi) * pos / n) * bands_ref[...]
    pre = t * w1t_ref[...] + _dot_exact(jnp.cos(ang), w1c_ref[...]) - _dot_exact(jnp.sin(ang), w1s_ref[...])
    hdn = jnp.sin(fr_ref[0:1, :] * (pre + b1_ref[...]))
    hdn = jnp.sin(fr_ref[1:2, :] * (_dot_exact(hdn, w2_ref[...]) + b2_ref[...]))
    raw = _dot_exact(hdn, w3_ref[...])
    window = jnp.exp(-t * deltas_ref[...])
    hf = raw[:, :HY_CH] * window
    hb = jnp.where(pos_i > 0, raw[:, HY_CH:] * window, 0.0)
    inv = 1.0 / (jnp.sum(jnp.abs(hf), axis=0, keepdims=True) + jnp.sum(jnp.abs(hb), axis=0, keepdims=True))
    e = (hf + hb) * inv
    e_ref[...] = e.astype(BF16)
    d_ref[...] = ((hb - hf) * inv).astype(BF16)
    sign = (1 - 2 * (pos_i & 1)).astype(F32)
    nyq_ref[...] = jnp.sum(e * sign, axis=0, keepdims=True)


def _hyena_filter(n, w1, b1, w2, b2, w3, freq):
    bands = jnp.linspace(1e-4, HY_BANDS - 1, HY_BANDS, dtype=F32).reshape(1, HY_BANDS)
    max_decay = math.log(HY_DECAY_TARGET) / HY_FAST_DECAY_PCT
    min_decay = math.log(HY_DECAY_TARGET) / HY_SLOW_DECAY_PCT
    deltas = jnp.abs(jnp.linspace(min_decay, max_decay, HY_CH, dtype=F32)).reshape(1, HY_CH)
    args = [w1[0:1], w1[1:1 + HY_BANDS], w1[1 + HY_BANDS:], b1.reshape(1, HY_FFN), w2, b2.reshape(1, HY_FFN),
            w3, freq, bands, deltas]
    return pl.pallas_call(
        functools.partial(_hy_filter_kernel, n=n),
        out_shape=[jax.ShapeDtypeStruct((n, HY_CH), BF16), jax.ShapeDtypeStruct((n, HY_CH), BF16),
                   jax.ShapeDtypeStruct((1, HY_CH), F32)],
        compiler_params=pltpu.CompilerParams(vmem_limit_bytes=VMEM_LIMIT),
        name="hyena_filter",
    )(*args)


def _hy_gate_kernel(x0_ref, x1_ref, v_ref, w0_ref, w1_ref, wv_ref, b0_ref, b1_ref, bv_ref,
                    zb_ref, x0c_ref, zf_ref, nyq_ref, *, n):
    row = lax.broadcasted_iota(jnp.int32, (n, 1), 0)

    def conv(u_ref, w_ref, b_ref):
        u = u_ref[...]
        up = jnp.where(row > 0, pltpu.roll(u, 1, 0), 0.0)
        dn = jnp.where(row < n - 1, pltpu.roll(u, n - 1, 0), 0.0)
        return up * w_ref[0:1, :] + u * w_ref[1:2, :] + dn * w_ref[2:3, :] + b_ref[...]

    z = conv(x1_ref, w1_ref, b1_ref) * conv(v_ref, wv_ref, bv_ref)
    x0c_ref[...] = conv(x0_ref, w0_ref, b0_ref)
    zf_ref[...] = z
    zb_ref[...] = z.astype(BF16)
    sign = (1 - 2 * (row & 1)).astype(F32)
    nyq_ref[...] = jnp.sum(z * sign, axis=0, keepdims=True)


def _hyena_gate(bu, short_w, short_b, n, batch, row_blk0):
    halves = HY_CH // LANES
    sb = short_b.reshape(1, 3 * HY_CH)
    seg = lambda part: pl.BlockSpec((n, LANES), lambda b, j: (row_blk0 + b, part * halves + j))
    wsp = lambda part: pl.BlockSpec((3, LANES), lambda b, j: (0, part * halves + j))
    bsp = lambda part: pl.BlockSpec((1, LANES), lambda b, j: (0, part * halves + j))
    return pl.pallas_call(
        functools.partial(_hy_gate_kernel, n=n),
        grid=(batch, halves),
        in_specs=[seg(0), seg(1), seg(2), wsp(0), wsp(1), wsp(2), bsp(0), bsp(1), bsp(2)],
        out_specs=[pl.BlockSpec((n, LANES), lambda b, j: (0, b * halves + j)),
                   pl.BlockSpec((n, LANES), lambda b, j: (b, j)),
                   pl.BlockSpec((n, LANES), lambda b, j: (b, j)),
                   pl.BlockSpec((None, 1, LANES), lambda b, j: (b, 0, j))],
        out_shape=[jax.ShapeDtypeStruct((n, batch * HY_CH), BF16),
                   jax.ShapeDtypeStruct((batch * n, HY_CH), F32),
                   jax.ShapeDtypeStruct((batch * n, HY_CH), F32),
                   jax.ShapeDtypeStruct((batch, 1, HY_CH), F32)],
        compiler_params=_cparams(("arbitrary", "arbitrary")),
        name="hyena_gate",
    )(bu, bu, bu, short_w, short_w, short_w, sb, sb, sb)


def _hy_spectrum_kernel(c_ref, s_ref, z_ref, e_ref, d_ref, yr_ref, yi_ref, *, n, tk, batch):
    c = c_ref[...]
    s = s_ref[...]
    zr = _dot(c, z_ref[...])
    zs = _dot(s, z_ref[...])
    fr = _dot(c, e_ref[...])
    fi = _dot(s, d_ref[...])
    k = pl.program_id(0) * tk + lax.broadcasted_iota(jnp.int32, (tk, 1), 0)
    wk = jnp.where(k == 0, 1.0, 2.0) * (1.0 / (2 * n))
    for b in range(batch):
        sl = slice(b * HY_CH, (b + 1) * HY_CH)
        yr = zr[:, sl] * fr + zs[:, sl] * fi
        yi = zr[:, sl] * fi - zs[:, sl] * fr
        yr_ref[:, sl] = (yr * wk).astype(BF16)
        yi_ref[:, sl] = (-(yi * wk)).astype(BF16)


def _hy_inverse_kernel(c_ref, s_ref, yr_ref, yi_ref, x0c_ref, zf_ref, nyqz_ref, nyqf_ref, bias_ref,
                       *rest, n, tt, batch):
    o_ref, y_scr = rest[-2], rest[-1]
    b = pl.program_id(1)

    @pl.when(b == 0)
    def _():
        y = _dot(c_ref[...], yr_ref[...]) + _dot(s_ref[...], yi_ref[...])
        for bb in range(batch):
            y_scr[bb] = y[:, bb * HY_CH:(bb + 1) * HY_CH]

    t = pl.program_id(0) * tt + lax.broadcasted_iota(jnp.int32, (tt, 1), 0)
    sign = (1 - 2 * (t & 1)).astype(F32)
    nyq = nyqz_ref[...] * nyqf_ref[...] * (1.0 / (2 * n))
    zf = zf_ref[...]
    o_ref[...] = x0c_ref[...] * (y_scr[b] + sign * nyq + bias_ref[...] * zf)


def _hyena_conv(cs, zb, e, d, x0c, zf, nyqz, nyqf, bias, n, batch, nt, out_blk0, prev=None):
    c_tab, s_tab = cs
    bw = batch * HY_CH
    tk = min(512, n)
    whole = lambda shape: pl.BlockSpec(shape, lambda *_: (0,) * len(shape), pipeline_mode=pl.Buffered(1))
    yr, yi = pl.pallas_call(
        functools.partial(_hy_spectrum_kernel, n=n, tk=tk, batch=batch),
        grid=(n // tk,),
        in_specs=[pl.BlockSpec((tk, n), lambda i: (i, 0)), pl.BlockSpec((tk, n), lambda i: (i, 0)),
                  whole((n, bw)), whole((n, HY_CH)), whole((n, HY_CH))],
        out_specs=[pl.BlockSpec((tk, bw), lambda i: (i, 0))] * 2,
        out_shape=[jax.ShapeDtypeStruct((n, bw), BF16)] * 2,
        compiler_params=_cparams(("arbitrary",)),
        name="hyena_spectrum",
    )(c_tab, s_tab, zb, e, d)

    tt = min(512, n)
    n_t = n // tt
    seg = lambda i, b: (b * n_t + i, 0)
    in_specs = [pl.BlockSpec((tt, n), lambda i, b: (i, 0)), pl.BlockSpec((tt, n), lambda i, b: (i, 0)),
                whole((n, bw)), whole((n, bw)),
                pl.BlockSpec((tt, HY_CH), seg), pl.BlockSpec((tt, HY_CH), seg),
                pl.BlockSpec((None, 1, HY_CH), lambda i, b: (b, 0, 0)),
                pl.BlockSpec((1, HY_CH), lambda i, b: (0, 0)), pl.BlockSpec((1, HY_CH), lambda i, b: (0, 0))]
    args = [c_tab, s_tab, yr, yi, x0c, zf, nyqz, nyqf, bias.reshape(1, HY_CH)]
    aliases = {}
    if prev is not None:
        in_specs.append(pl.BlockSpec(memory_space=pl.ANY))
        args.append(prev)
        aliases = {len(args) - 1: 0}
    return pl.pallas_call(
        functools.partial(_hy_inverse_kernel, n=n, tt=tt, batch=batch),
        grid=(n_t, batch),
        in_specs=in_specs,
        out_specs=pl.BlockSpec((tt, HY_CH), lambda i, b: (out_blk0 + b * n_t + i, 0)),
        out_shape=jax.ShapeDtypeStruct((nt, HY_CH), F32),
        scratch_shapes=[pltpu.VMEM((batch, tt, HY_CH), F32)],
        input_output_aliases=aliases,
        compiler_params=_cparams(("arbitrary", "arbitrary")),
        name="hyena_inverse" if prev is None else "hyena_inverse_ctx",
    )(*args)


def _iota(shape, axis):
    return lax.broadcasted_iota(jnp.int32, shape, axis)


def _scan_maps(geo, cc):
    batch, seq, ctx = geo["batch"], geo["seq"], geo["ctx"]
    n_cc, n_lc = ctx // cc, seq // cc
    ctx0 = (batch * seq) // cc

    def fwd(b, i):
        return (jnp.where(i < n_cc, ctx0 + b * n_cc + i, b * n_lc + i - n_cc), 0)

    def bwd(b, i):
        return (jnp.where(i < n_cc, ctx0 + b * n_cc + (n_cc - 1 - i), b * n_lc + (n_lc - 1 - (i - n_cc))), 0)

    return fwd, bwd, n_cc + n_lc


def _ret_kernel(qf_ref, kf_ref, vf_ref, qb_ref, kb_ref, vb_ref, dec_ref, of_ref, ob_ref, st_ref):
    @pl.when(pl.program_id(1) == 0)
    def _():
        st_ref[...] = jnp.zeros_like(st_ref)

    cc = qf_ref.shape[0]
    r_i, c_i = _iota((cc, cc), 0), _iota((cc, cc), 1)
    pos = _iota((cc, 1), 0).astype(F32)
    for d, (q_ref, k_ref, v_ref, o_ref) in enumerate(((qf_ref, kf_ref, vf_ref, of_ref),
                                                      (qb_ref, kb_ref, vb_ref, ob_ref))):
        reverse = d == 1
        lg = -jnp.exp(dec_ref[d])
        steps_in = (cc - pos) if reverse else (pos + 1.0)
        steps_out = pos if reverse else (cc - 1.0 - pos)
        q = q_ref[...]
        k = k_ref[...] * (RET_DH ** -0.5)
        qd = (q * jnp.exp(steps_in * lg)).astype(BF16)
        kd = (k * jnp.exp(steps_out * lg)).astype(BF16)
        qb, kb, vb = q.astype(BF16), k.astype(BF16), v_ref[...].astype(BF16)
        gain = jnp.exp(cc * lg)
        dist = (c_i - r_i) if reverse else (r_i - c_i)
        keep = dist >= 0
        dist_f = jnp.where(keep, dist, 0).astype(F32)
        for h in range(N_HEADS):
            hs = slice(h * RET_DH, (h + 1) * RET_DH)
            decay = jnp.where(keep, jnp.exp(dist_f * lg[:, h * RET_DH:h * RET_DH + 1]), 0.0)
            sc = (_dot_nt(qb[:, hs], kb[:, hs]) * decay).astype(BF16)
            st = st_ref[d, h]
            o_ref[:, hs] = _dot(sc, vb[:, hs]) + _dot_nt(qd[:, hs], st.astype(BF16))
            st_ref[d, h] = st * gain[:, hs] + _dot_tn(vb[:, hs], kd[:, hs])


def _retention(q, k, v, dec, geo):
    nt, w = q.shape
    cc = RET_CHUNK
    fwd, bwd, steps = _scan_maps(geo, cc)
    blk = lambda m: pl.BlockSpec((cc, w), m)
    return pl.pallas_call(
        _ret_kernel,
        grid=(geo["batch"], steps),
        in_specs=[blk(fwd), blk(fwd), blk(fwd), blk(bwd), blk(bwd), blk(bwd),
                  pl.BlockSpec((2, 1, w), lambda b, i: (0, 0, 0))],
        out_specs=[blk(fwd), blk(bwd)],
        out_shape=[jax.ShapeDtypeStruct((nt, w), F32)] * 2,
        scratch_shapes=[pltpu.VMEM((2, N_HEADS, RET_DH, RET_DH), F32)],
        compiler_params=_cparams(("arbitrary", "arbitrary")),
        name="recurrence_ret",
    )(q, k, v, q, k, v, dec)


def _gla_direction(q, k, v, la, st_ref, o_ref, *, reverse):
    cc, wk = q.shape
    wv = v.shape[1]
    r_i, c_i = _iota((cc, cc), 0), _iota((cc, cc), 1)
    incl = (c_i >= r_i) if reverse else (c_i <= r_i)
    cum = _dot_exact(jnp.where(incl, 1.0, 0.0), la)
    cum_end = cum[0:1] if reverse else cum[cc - 1:cc]

    qd = (q * jnp.exp(cum)).astype(BF16)
    kd = (k * jnp.exp(cum_end - cum)).astype(BF16)
    vb = v.astype(BF16)
    st = st_ref[...]
    inter = _dot_nt(qd, st.astype(BF16))
    same_head = (_iota((wv, wk), 0) // GLA_DV) == (_iota((wv, wk), 1) // GLA_DK)
    st_ref[...] = st * jnp.exp(cum_end) + jnp.where(same_head, _dot_tn(vb, kd), 0.0)

    sub = REC_SUB
    hs = N_HEADS * sub
    q_own = (_iota((hs, wk), 0) // sub) == (_iota((hs, wk), 1) // GLA_DK)
    for j in range(cc // sub):
        r0, r1 = j * sub, (j + 1) * sub
        if reverse:
            ka, kb = r0, cc
            base = cum[r1:r1 + 1] if r1 < cc else jnp.zeros((1, wk), F32)
        else:
            ka, kb = 0, r1
            base = cum[r0 - 1:r0] if r0 > 0 else jnp.zeros((1, wk), F32)
        qj = q[r0:r1] * jnp.exp(cum[r0:r1] - base)
        kj = (k[ka:kb] * jnp.exp(jnp.minimum(base - cum[ka:kb], EXP_CLAMP))).astype(BF16)
        q_stack = jnp.where(q_own, jnp.concatenate([qj] * N_HEADS, axis=0), 0.0).astype(BF16)
        rows = r0 + (_iota((hs, kb - ka), 0) % sub)
        cols = ka + _iota((hs, kb - ka), 1)
        keep = (cols >= rows) if reverse else (cols <= rows)
        sc = jnp.where(keep, _dot_nt(q_stack, kj), 0.0).astype(BF16)
        full = _dot(sc, vb[ka:kb])
        lane_head = _iota((sub, wv), 1) // GLA_DV
        oj = inter[r0:r1]
        for h in range(N_HEADS):
            oj = oj + jnp.where(lane_head == h, full[h * sub:(h + 1) * sub], 0.0)
        o_ref[r0:r1, :] = oj


def _gla_kernel(qf_ref, kf_ref, vf_ref, lf_ref, qb_ref, kb_ref, vb_ref, lb_ref, gw_ref, gb_ref,
                of_ref, ob_ref, st_ref):
    @pl.when(pl.program_id(1) == 0)
    def _():
        st_ref[...] = jnp.zeros_like(st_ref)

    for d, (q_ref, k_ref, v_ref, l_ref, o_ref) in enumerate(((qf_ref, kf_ref, vf_ref, lf_ref, of_ref),
                                                             (qb_ref, kb_ref, vb_ref, lb_ref, ob_ref))):
        logit = _dot_exact(l_ref[...], gw_ref[d]) + gb_ref[d]
        la = (jnp.minimum(logit, 0.0) - jnp.log(1.0 + jnp.exp(-jnp.abs(logit)))) * (1.0 / GLA_GATE_NORM)
        _gla_direction(q_ref[...] * (GLA_DK ** -0.5), k_ref[...], v_ref[...], la, st_ref.at[d], o_ref,
                       reverse=(d == 1))


def _gla(q, k, v, glr, gw, gb, geo):
    nt, wq = q.shape
    wv = v.shape[1]
    cc = REC_CHUNK
    fwd, bwd, steps = _scan_maps(geo, cc)
    blk = lambda w, m: pl.BlockSpec((cc, w), m)
    return pl.pallas_call(
        _gla_kernel,
        grid=(geo["batch"], steps),
        in_specs=[blk(wq, fwd), blk(wq, fwd), blk(wv, fwd), blk(LANES, fwd),
                  blk(wq, bwd), blk(wq, bwd), blk(wv, bwd), blk(LANES, bwd),
                  pl.BlockSpec((2, LANES, wq), lambda b, i: (0, 0, 0)),
                  pl.BlockSpec((2, 1, wq), lambda b, i: (0, 0, 0))],
        out_specs=[blk(wv, fwd), blk(wv, bwd)],
        out_shape=[jax.ShapeDtypeStruct((nt, wv), F32)] * 2,
        scratch_shapes=[pltpu.VMEM((2, wv, wq), F32)],
        compiler_params=_cparams(("arbitrary", "arbitrary")),
        name="recurrence_gla",
    )(q, k, v, glr, q, k, v, glr, gw, gb)


def _mixout_kernel(x_ref, mod_ref, at_ref, hy_ref, rf_ref, rb_ref, cg_ref, gf_ref, gb_ref, dr_ref,
                   seg_ref, rg_ref, gg_ref, wo_ref, g1_ref, g2_ref, rw_ref, rbias_ref,
                   x1_ref, h2_ref, ti_ref, tw_ref, rk_ref, cnt_ref):
    @pl.when(pl.program_id(0) == 0)
    def _():
        cnt_ref[...] = jnp.zeros_like(cnt_ref)

    def head_norm(o):
        ms = _dot_exact(o * o, seg_ref[...]) * (1.0 / RET_DH)
        return o * lax.rsqrt(ms + EPS)

    a = jnp.concatenate([at_ref[h] for h in range(N_HEADS)], axis=-1)
    cg = cg_ref[...]
    dr = dr_ref[...]
    rt = head_norm(rf_ref[...] + rb_ref[...]) * rg_ref[...] * (cg * _sigmoid(cg))
    gl = head_norm(gf_ref[...] + gb_ref[...]) * gg_ref[...] * (dr * _sigmoid(dr))
    cat = jnp.concatenate([a, hy_ref[...], rt, gl], axis=-1).astype(BF16)
    y = _dot(cat, wo_ref[...])
    x1 = x_ref[...] + mod_ref[2:3, :] * (_rms(y) * g1_ref[...])
    x1_ref[...] = x1
    h2 = _rms(x1) * g2_ref[...] * (1.0 + mod_ref[4:5, :]) + mod_ref[3:4, :]
    h2_ref[...] = h2

    vals = _dot_exact(h2, rw_ref[...]) + rbias_ref[...]
    tm = vals.shape[0]
    lane = lax.broadcasted_iota(jnp.int32, vals.shape, 1)
    idx_out = jnp.zeros(vals.shape, jnp.int32)
    w_out = jnp.zeros(vals.shape, F32)
    top = None
    den = 0.0
    picks = []
    for r in range(TOP_K):
        m = jnp.max(vals, axis=-1, keepdims=True)
        idx = jnp.min(jnp.where(vals == m, lane, LANES), axis=-1, keepdims=True)
        hit = lane == idx
        vals = jnp.where(hit, -jnp.inf, vals)
        top = m if top is None else top
        e = jnp.exp(m - top)
        den = den + e
        picks.append((idx, e, hit))
    inv = 1.0 / den
    for r, (idx, e, _) in enumerate(picks):
        idx_out = jnp.where(lane == r, idx, idx_out)
        w_out = jnp.where(lane == r, e * inv, w_out)
    ti_ref[...] = idx_out
    tw_ref[...] = w_out

    chosen = functools.reduce(jnp.logical_or, [hit for _, _, hit in picks])
    chosen_f = jnp.where(chosen, 1.0, 0.0)
    earlier = (lax.broadcasted_iota(jnp.int32, (tm, tm), 1) < lax.broadcasted_iota(jnp.int32, (tm, tm), 0))
    before = _dot(jnp.where(earlier, 1.0, 0.0).astype(BF16), chosen_f.astype(BF16)) + cnt_ref[...]
    rk_out = jnp.zeros(vals.shape, jnp.int32)
    for r, (_, _, hit) in enumerate(picks):
        rank = jnp.sum(jnp.where(hit, before, 0.0), axis=-1, keepdims=True)
        rk_out = jnp.where(lane == r, rank.astype(jnp.int32), rk_out)
    rk_ref[...] = rk_out
    cnt_ref[...] = cnt_ref[...] + jnp.sum(chosen_f, axis=0, keepdims=True)


def _mix_out(n_tiles, x, mod, at, hy, rf, rb, cg, gf, gb, dr, seg, rg, gg, wo, g1, g2, rw, rbias, geo):
    nt, d = x.shape
    tm = ROW_TILE
    lat_tiles, per_batch, batch = geo["lat_tiles"], geo["tiles_per_batch"], geo["batch"]

    def mod_map(i):
        return (jnp.where(i < lat_tiles, i // per_batch, batch), 0, 0)

    row = lambda i: (i, 0)
    const = lambda i: (0, 0)
    g_blk = pl.BlockSpec((tm, GROUP_W), row)
    return pl.pallas_call(
        _mixout_kernel,
        grid=(n_tiles,),
        in_specs=[pl.BlockSpec((tm, d), row), pl.BlockSpec((None, 6, d), mod_map),
                  pl.BlockSpec((N_HEADS, tm, DIFF_DV), lambda i: (0, i, 0)),
                  g_blk, g_blk, g_blk, g_blk, g_blk, g_blk, g_blk,
                  pl.BlockSpec((GROUP_W, GROUP_W), const), pl.BlockSpec((1, GROUP_W), const),
                  pl.BlockSpec((1, GROUP_W), const),
                  pl.BlockSpec((d, d), const, pipeline_mode=pl.Buffered(1)),
                  pl.BlockSpec((1, d), const), pl.BlockSpec((1, d), const),
                  pl.BlockSpec((d, LANES), const), pl.BlockSpec((1, LANES), const)],
        out_specs=[pl.BlockSpec((tm, d), row), pl.BlockSpec((tm, d), row),
                   pl.BlockSpec((tm, LANES), row), pl.BlockSpec((tm, LANES), row),
                   pl.BlockSpec((tm, LANES), row), pl.BlockSpec((1, LANES), const)],
        out_shape=[jax.ShapeDtypeStruct((nt, d), F32), jax.ShapeDtypeStruct((nt, d), F32),
                   jax.ShapeDtypeStruct((nt, LANES), jnp.int32), jax.ShapeDtypeStruct((nt, LANES), F32),
                   jax.ShapeDtypeStruct((nt, LANES), jnp.int32), jax.ShapeDtypeStruct((1, LANES), F32)],
        compiler_params=_cparams(("arbitrary",)),
        name="mix_out_router",
    )(x, mod, at, hy, rf, rb, cg, gf, gb, dr, seg, rg, gg, wo, g1, g2, rw, rbias)


def _expert_kernel(te_ref, na_ref, xs_ref, wg_ref, wl_ref, bg_ref, bl_ref, w2_ref, b2_ref, ys_ref):
    active = pl.program_id(0) < na_ref[0]

    @pl.when(active)
    def _():
        x = xs_ref[...].astype(BF16)
        glu = jnp.minimum(_dot_nt(x, wg_ref[...]) + bg_ref[...], SWIGLU_LIMIT)
        lin = jnp.clip(_dot_nt(x, wl_ref[...]) + bl_ref[...], -SWIGLU_LIMIT, SWIGLU_LIMIT)
        act = glu * _sigmoid(SWIGLU_ALPHA * glu) * (lin + 1.0)
        ys_ref[...] = _dot(act.astype(BF16), w2_ref[...].astype(BF16)) + b2_ref[...]

    @pl.when(jnp.logical_not(active))
    def _():
        ys_ref[...] = jnp.zeros_like(ys_ref)


def _expert_ffn(tile_e, n_active, xs, wg, wl, bg, bl, w2, b2):
    p, d = xs.shape
    tm = MOE_TILE
    de = wg.shape[1]
    wmap = lambda i, te, na: (te[i], 0, 0)
    grid_spec = pltpu.PrefetchScalarGridSpec(
        num_scalar_prefetch=2,
        grid=(p // tm,),
        in_specs=[pl.BlockSpec((tm, d), lambda i, te, na: (i, 0)),
                  pl.BlockSpec((None, de, d), wmap), pl.BlockSpec((None, de, d), wmap),
                  pl.BlockSpec((None, 1, de), wmap), pl.BlockSpec((None, 1, de), wmap),
                  pl.BlockSpec((None, de, d), wmap), pl.BlockSpec((None, 1, d), wmap)],
        out_specs=pl.BlockSpec((tm, d), lambda i, te, na: (i, 0)),
    )
    return pl.pallas_call(
        _expert_kernel,
        grid_spec=grid_spec,
        out_shape=jax.ShapeDtypeStruct((p, d), F32),
        compiler_params=_cparams(("arbitrary",)),
        name="expert_ffn",
    )(tile_e, n_active, xs, wg, wl, bg, bl, w2, b2)


def _route_plan(topi, rank, counts, tm):
    n, k = topi.shape
    padded = ((counts + tm - 1) // tm) * tm
    pend = jnp.cumsum(padded)
    pstart = pend - padded
    experts = jnp.arange(N_EXPERTS, dtype=jnp.int32)
    start = jnp.sum(jnp.where(topi[:, :, None] == experts, pstart, 0), axis=-1)
    dest = (start + rank).astype(jnp.int32)
    p = n * k + N_EXPERTS * tm
    tile_start = jnp.arange(p // tm, dtype=jnp.int32) * tm
    tile_e = jnp.minimum(jnp.sum((tile_start[:, None] >= pend[None, :]).astype(jnp.int32), axis=1), N_EXPERTS - 1)
    n_active = (pend[-1:] // tm).astype(jnp.int32)
    return dest, tile_e, n_active, p


def _dispatch_kernel(dest_ref, x_ref, init_ref, out_ref, sem, *, tb):
    del init_ref
    n_rows = tb * TOP_K

    def issue(t, carry):
        for k in range(TOP_K):
            pltpu.make_async_copy(x_ref.at[pl.ds(t, 1)], out_ref.at[pl.ds(dest_ref[0, t * TOP_K + k], 1)],
                                  sem).start()
        return carry

    lax.fori_loop(0, tb, issue, 0, unroll=4)
    pltpu.make_async_copy(out_ref.at[pl.ds(0, n_rows)], out_ref.at[pl.ds(0, n_rows)], sem).wait()


def _dispatch(h2, dest, p_rows, n_tok):
    d = h2.shape[1]
    tb = ROW_TILE
    steps = n_tok // tb
    return pl.pallas_call(
        functools.partial(_dispatch_kernel, tb=tb),
        grid=(steps,),
        in_specs=[pl.BlockSpec((None, 1, tb * TOP_K), lambda i: (i, 0, 0), memory_space=pltpu.SMEM),
                  pl.BlockSpec((tb, d), lambda i: (i, 0)),
                  pl.BlockSpec(memory_space=pl.ANY)],
        out_specs=pl.BlockSpec(memory_space=pl.ANY),
        out_shape=jax.ShapeDtypeStruct((p_rows, d), F32),
        scratch_shapes=[pltpu.SemaphoreType.DMA(())],
        input_output_aliases={2: 0},
        compiler_params=pltpu.CompilerParams(dimension_semantics=("arbitrary",), has_side_effects=True),
        name="moe_dispatch",
    )(dest.reshape(steps, 1, tb * TOP_K), h2, jnp.zeros((p_rows, d), F32))


def _combine_kernel(dest_ref, x_ref, tw_ref, mod_ref, g_ref, ys_ref, o_ref, buf, sem, *, tb):
    n_rows = tb * TOP_K

    def issue(r, carry):
        pltpu.make_async_copy(ys_ref.at[pl.ds(dest_ref[0, r], 1)], buf.at[pl.ds(r, 1)], sem).start()
        return carry

    lax.fori_loop(0, n_rows, issue, 0, unroll=8)
    pltpu.make_async_copy(ys_ref.at[pl.ds(0, n_rows)], buf, sem).wait()
    y = None
    for k in range(TOP_K):
        t = buf[k * tb:(k + 1) * tb, :] * tw_ref[:, k:k + 1]
        y = t if y is None else y + t
    o_ref[...] = x_ref[...] + mod_ref[5:6, :] * (_rms(y) * g_ref[...])


def _combine_residual(n_tiles, x, ys, dest, topw, mod, g, geo):
    nt, d = x.shape
    tb = ROW_TILE
    lat_tiles, per_batch, batch = geo["lat_tiles"], geo["tiles_per_batch"], geo["batch"]

    def mod_map(i):
        return (jnp.where(i < lat_tiles, i // per_batch, batch), 0, 0)

    row = lambda i: (i, 0)
    slot_major = jnp.swapaxes(dest.reshape(n_tiles, tb, TOP_K), 1, 2).reshape(n_tiles, 1, tb * TOP_K)
    return pl.pallas_call(
        functools.partial(_combine_kernel, tb=tb),
        grid=(n_tiles,),
        in_specs=[pl.BlockSpec((None, 1, tb * TOP_K), lambda i: (i, 0, 0), memory_space=pltpu.SMEM),
                  pl.BlockSpec((tb, d), row), pl.BlockSpec((tb, LANES), row),
                  pl.BlockSpec((None, 6, d), mod_map), pl.BlockSpec((1, d), lambda i: (0, 0)),
                  pl.BlockSpec(memory_space=pl.ANY)],
        out_specs=pl.BlockSpec((tb, d), row),
        out_shape=jax.ShapeDtypeStruct((nt, d), F32),
        scratch_shapes=[pltpu.VMEM((tb * TOP_K, d), F32), pltpu.SemaphoreType.DMA(())],
        compiler_params=_cparams(("arbitrary",)),
        name="combine_residual",
    )(slot_major, x, topw, mod, g, ys)


def _rope_partner(dim):
    h, q = dim // 2, dim // 4
    perm = np.zeros(dim, np.int32)
    sign = np.zeros(dim, np.float32)
    for base in (0, h):
        for j in range(q):
            perm[base + j], sign[base + j] = base + j + q, -1.0
            perm[base + q + j], sign[base + q + j] = base + j, 1.0
    return perm, sign


def _rot_cols(w, dim):
    perm, sign = _rope_partner(dim)
    reps = w.shape[1] // dim
    full_perm = np.concatenate([perm + r * dim for r in range(reps)])
    return w[:, full_perm] * jnp.asarray(np.tile(sign, reps))


def _rope_tables(n_tok, dim, reps, pad_rows):
    rows = n_tok // GRID_W
    row = jnp.repeat(jnp.arange(rows, dtype=F32), GRID_W)
    col = jnp.tile(jnp.arange(GRID_W, dtype=F32), rows)
    quarter = dim // 4
    inv = ROPE_BASE ** (-jnp.arange(quarter, dtype=F32) / quarter)
    ar = row[:, None] * inv[None]
    ac = col[:, None] * inv[None]
    ang = jnp.concatenate([ar, ar, ac, ac], axis=-1)
    cos = jnp.concatenate([jnp.tile(jnp.cos(ang), (1, reps)), jnp.ones((pad_rows, dim * reps), F32)])
    sin = jnp.concatenate([jnp.tile(jnp.sin(ang), (1, reps)), jnp.zeros((pad_rows, dim * reps), F32)])
    return cos, sin


def _widen_w_in(w):
    sizes = (256, 256, 256, 768, 256, 256, 256, 256, 128, 128, 256, 256, 32)
    cuts = np.cumsum(sizes)[:-1].tolist()
    aq, ak, av, bu, cq, ck, cv, cg, dq, dk, dv, dr, dl = jnp.split(w, cuts, axis=1)
    dl = jnp.pad(dl, ((0, 0), (0, LANES - dl.shape[1])))
    parts = [aq, _rot_cols(aq, DIFF_DQK), ak, _rot_cols(ak, DIFF_DQK), av, bu,
             cq, _rot_cols(cq, RET_DH), ck, _rot_cols(ck, RET_DH), cv, cg, dq, dk, dv, dr, dl]
    return jnp.concatenate(parts, axis=1).astype(BF16)


def kernel(x, c, ctx, c_ctx, w_mod, b_mod, norm_g, w_in, w_out, diff_lambda, diff_subln_g, hy_short_w, hy_short_b, hy_w1, hy_b1, hy_w2, hy_b2, hy_w3, hy_freq, hy_bias, ret_decay, ret_norm_g, gla_gate_w, gla_gate_b, gla_norm_g, router_w, router_b, exp_w1, exp_b1, exp_w2, exp_b2):
    batch, seq, d = x.shape
    n_ctx = ctx.shape[1]
    depth = w_mod.shape[0]
    n_lat_rows, n_ctx_rows = batch * seq, batch * n_ctx
    nt = n_lat_rows + n_ctx_rows
    assert d == D_MODEL and seq % ROW_TILE == 0 and n_ctx % ROW_TILE == 0 and seq % n_ctx == 0
    geo = dict(batch=batch, seq=seq, ctx=n_ctx, lat_tiles=n_lat_rows // ROW_TILE,
               tiles_per_batch=seq // ROW_TILE)

    xs = jnp.concatenate([x.reshape(n_lat_rows, d), ctx.reshape(n_ctx_rows, d)], axis=0)
    mod_rows = 8
    cc = jnp.zeros((mod_rows, d), F32).at[:batch].set(c).at[batch].set(c_ctx)
    mod_all = _modulation(cc, w_mod, b_mod).reshape(depth, mod_rows, 6, d)

    rope = (*_rope_tables(seq, DIFF_DQK, GROUP_W // DIFF_DQK, ROW_TILE),
            *_rope_tables(seq, RET_DH, GROUP_W // RET_DH, ROW_TILE))
    dft_lat = _dft_tables(seq)
    dft_ctx = _dft_tables(n_ctx)
    seg = jnp.asarray(np.kron(np.eye(N_HEADS, dtype=np.float32), np.ones((RET_DH, RET_DH), np.float32)))

    for l in range(depth):
        need_ctx = l < depth - 1
        lam_init = 0.8 - 0.6 * math.exp(-0.3 * l)
        mod = mod_all[l]
        (aq, ak, av, bu, cq, ck, cv, cg, dq, dk, dv, dr, dl) = _in_projection(
            xs, mod, norm_g[l, 0].reshape(1, d), _widen_w_in(w_in[l]), rope, geo)

        sub_g = diff_subln_g[l].reshape(1, DIFF_DV)
        at = _diff_attention(diff_lambda[l], sub_g, aq, ak, av, geo, lam_init)
        if need_ctx:
            at = _diff_attention(diff_lambda[l], sub_g, aq, ak, av, geo, lam_init, prev=at)

        def hyena(n, row_blk0, tables, out_blk0, prev):
            e, dd, nyqf = _hyena_filter(n, hy_w1[l], hy_b1[l], hy_w2[l], hy_b2[l], hy_w3[l], hy_freq[l])
            zb, x0c, zf, nyqz = _hyena_gate(bu, hy_short_w[l], hy_short_b[l], n, batch, row_blk0)
            return _hyena_conv(tables, zb, e, dd, x0c, zf, nyqz, nyqf, hy_bias[l], n, batch, nt, out_blk0, prev)

        hy = hyena(seq, 0, dft_lat, 0, None)
        if need_ctx:
            hy = hyena(n_ctx, n_lat_rows // n_ctx, dft_ctx, n_lat_rows // min(512, n_ctx), hy)

        dec = jnp.repeat(ret_decay[l], RET_DH, axis=-1).reshape(2, 1, N_HEADS * RET_DH)
        rf, rb = _retention(cq, ck, cv, dec, geo)

        gw = jnp.zeros((2, LANES, N_HEADS * GLA_DK), F32)
        gw = gw.at[0, :GLA_RANK].set(gla_gate_w[l, 0]).at[1, GLA_RANK:2 * GLA_RANK].set(gla_gate_w[l, 1])
        gbias = gla_gate_b[l].reshape(2, 1, N_HEADS * GLA_DK)
        gf, gb = _gla(dq, dk, dv, dl, gw, gbias, geo)

        n_tiles = nt // ROW_TILE if need_ctx else n_lat_rows // ROW_TILE
        rw = jnp.pad(router_w[l], ((0, 0), (0, LANES - N_EXPERTS)))
        rbias = jnp.pad(router_b[l], (0, LANES - N_EXPERTS), constant_values=-jnp.inf).reshape(1, LANES)
        x1, h2, topi, topw, rank, cnt = _mix_out(
            n_tiles, xs, mod, at, hy, rf, rb, cg, gf, gb, dr, seg,
            ret_norm_g[l].reshape(1, GROUP_W), jnp.tile(gla_norm_g[l], N_HEADS).reshape(1, GROUP_W),
            w_out[l].astype(BF16), norm_g[l, 1].reshape(1, d), norm_g[l, 2].reshape(1, d), rw, rbias, geo)

        n_tok = n_tiles * ROW_TILE
        dest, tile_e, n_active, p_rows = _route_plan(
            topi[:n_tok, :TOP_K], rank[:n_tok, :TOP_K], cnt[0, :N_EXPERTS].astype(jnp.int32), MOE_TILE)
        dispatched = _dispatch(h2, dest, p_rows, n_tok)
        w1t = jnp.swapaxes(exp_w1[l], 1, 2).reshape(N_EXPERTS, D_EXPERT, 2, d)
        b1 = exp_b1[l].reshape(N_EXPERTS, 1, D_EXPERT, 2)
        ys = _expert_ffn(tile_e, n_active, dispatched,
                         w1t[:, :, 0, :].astype(BF16), w1t[:, :, 1, :].astype(BF16),
                         b1[..., 0], b1[..., 1], exp_w2[l], exp_b2[l][:, None, :])
        xs = _combine_residual(n_tiles, x1, ys, dest, topw, mod, norm_g[l, 3].reshape(1, d), geo)

    return xs[:n_lat_rows].reshape(batch, seq, d)
```

```python
import functools
import math

import numpy as np
import jax
import jax.numpy as jnp
from jax import lax
from jax.experimental import pallas as pl
from jax.experimental.pallas import tpu as pltpu

F32 = jnp.float32
BF16 = jnp.bfloat16
HIGHEST = lax.Precision.HIGHEST

D_MODEL = 1024
GRID_W = 64
GROUP_W = 256
N_HEADS = 4
DIFF_DQK = 32
DIFF_DV = 64
ROPE_BASE = 10000.0
HY_CH = 256
HY_BANDS = 16
HY_FFN = 64
HY_FAST_DECAY_PCT = 0.3
HY_SLOW_DECAY_PCT = 1.5
HY_DECAY_TARGET = 1e-2
RET_DH = 64
GLA_DK = 32
GLA_DV = 64
GLA_RANK = 16
GLA_GATE_NORM = 16.0
N_EXPERTS = 32
TOP_K = 4
D_EXPERT = 1024
SWIGLU_LIMIT = 7.0
SWIGLU_ALPHA = 1.702
EPS = 1e-6

LANES = 128
ROW_TILE = 256
REC_CHUNK = 64
RET_CHUNK = 256
REC_SUB = 16
EXP_CLAMP = 80.0
MOE_TILE = 256
ATT_KEY_CHUNK = 512
VMEM_LIMIT = 52 * 1024 * 1024

_A_Q, _A_QR, _A_K, _A_KR, _A_V = 0, 256, 512, 768, 1024
_B_U = 1280
_C_Q, _C_QR, _C_K, _C_KR, _C_V, _C_G = 2048, 2304, 2560, 2816, 3072, 3328
_D_Q, _D_K, _D_V, _D_R, _D_L = 3584, 3712, 3840, 4096, 4352
_IN_COLS = 4480


def _cparams(sem):
    return pltpu.CompilerParams(dimension_semantics=sem, vmem_limit_bytes=VMEM_LIMIT)


def _sigmoid(x):
    return 1.0 / (1.0 + jnp.exp(-x))


def _rms(x):
    return x * lax.rsqrt(jnp.mean(x * x, axis=-1, keepdims=True) + EPS)


def _dot(a, b):
    return jnp.dot(a, b, preferred_element_type=F32)


def _dot_exact(a, b):
    return jnp.dot(a, b, preferred_element_type=F32, precision=HIGHEST)


def _dot_nt(a, b):
    return lax.dot_general(a, b, (((1,), (1,)), ((), ())), preferred_element_type=F32)


def _dot_tn(a, b):
    return lax.dot_general(a, b, (((0,), (0,)), ((), ())), preferred_element_type=F32)


def _mod_kernel(c_ref, w_ref, b_ref, o_ref):
    c = c_ref[...]
    o_ref[...] = _dot_exact(c * _sigmoid(c), w_ref[...]) + b_ref[...]


def _modulation(cc, w_mod, b_mod):
    depth, d, n = w_mod.shape
    tn = n // 4
    rows = cc.shape[0]
    return pl.pallas_call(
        _mod_kernel,
        grid=(depth, n // tn),
        in_specs=[pl.BlockSpec((rows, d), lambda l, j: (0, 0)),
                  pl.BlockSpec((None, d, tn), lambda l, j: (l, 0, j)),
                  pl.BlockSpec((None, 1, tn), lambda l, j: (l, 0, j))],
        out_specs=pl.BlockSpec((None, rows, tn), lambda l, j: (l, 0, j)),
        out_shape=jax.ShapeDtypeStruct((depth, rows, n), F32),
        compiler_params=_cparams(("arbitrary", "arbitrary")),
        name="modulation",
    )(cc, w_mod, b_mod.reshape(depth, 1, n))


def _inproj_kernel(x_ref, mod_ref, g_ref, w_ref, cosa_ref, sina_ref, cosc_ref, sinc_ref,
                   aq_ref, ak_ref, av_ref, bu_ref, cq_ref, ck_ref, cv_ref, cg_ref,
                   dq_ref, dk_ref, dv_ref, dr_ref, dl_ref):
    xn = _rms(x_ref[...]) * g_ref[...]
    h = (xn * (1.0 + mod_ref[1:2, :]) + mod_ref[0:1, :]).astype(BF16)

    def proj(a, width):
        return _dot(h, w_ref[:, a:a + width])

    def roped(a, a_rot, cos_ref, sin_ref):
        return proj(a, GROUP_W) * cos_ref[...] + proj(a_rot, GROUP_W) * sin_ref[...]

    aq = roped(_A_Q, _A_QR, cosa_ref, sina_ref) * (DIFF_DQK ** -0.5 * math.log2(math.e))
    ak = roped(_A_K, _A_KR, cosa_ref, sina_ref)
    av = proj(_A_V, GROUP_W)
    ones_col = jnp.where(lax.broadcasted_iota(jnp.int32, (av.shape[0], LANES - DIFF_DV), 1) == 0, 1.0, 0.0)
    for hd in range(N_HEADS):
        sl = slice(hd * DIFF_DV, (hd + 1) * DIFF_DV)
        aq_ref[hd] = aq[:, sl].astype(BF16)
        ak_ref[hd] = ak[:, sl].astype(BF16)
        av_ref[hd] = jnp.concatenate([av[:, sl], ones_col], axis=-1).astype(BF16)
    bu_ref[...] = proj(_B_U, 3 * HY_CH)
    cq_ref[...] = roped(_C_Q, _C_QR, cosc_ref, sinc_ref)
    ck_ref[...] = roped(_C_K, _C_KR, cosc_ref, sinc_ref)
    cv_ref[...] = proj(_C_V, GROUP_W)
    cg_ref[...] = proj(_C_G, GROUP_W)
    dq_ref[...] = proj(_D_Q, LANES)
    dk_ref[...] = proj(_D_K, LANES)
    dv_ref[...] = proj(_D_V, GROUP_W)
    dr_ref[...] = proj(_D_R, GROUP_W)
    dl_ref[...] = proj(_D_L, LANES)


def _in_projection(x, mod, g, w_wide, rope, geo):
    nt, d = x.shape
    tm = ROW_TILE
    lat_tiles, per_batch, batch = geo["lat_tiles"], geo["tiles_per_batch"], geo["batch"]

    def mod_map(i):
        return (jnp.where(i < lat_tiles, i // per_batch, batch), 0, 0)

    def rope_map(i):
        return (jnp.where(i < lat_tiles, i % per_batch, per_batch), 0)

    row = lambda i: (i, 0)
    head = lambda i: (0, i, 0)
    const = lambda i: (0, 0)
    f32_out = lambda w: jax.ShapeDtypeStruct((nt, w), F32)
    head_out = lambda w: jax.ShapeDtypeStruct((N_HEADS, nt, w), BF16)
    head_spec = lambda w: pl.BlockSpec((N_HEADS, tm, w), head)
    widths = [3 * HY_CH, GROUP_W, GROUP_W, GROUP_W, GROUP_W, LANES, LANES, GROUP_W, GROUP_W, LANES]
    return pl.pallas_call(
        _inproj_kernel,
        grid=(nt // tm,),
        in_specs=[pl.BlockSpec((tm, d), row),
                  pl.BlockSpec((None, 6, d), mod_map),
                  pl.BlockSpec((1, d), const),
                  pl.BlockSpec((d, _IN_COLS), const, pipeline_mode=pl.Buffered(1)),
                  pl.BlockSpec((tm, GROUP_W), rope_map), pl.BlockSpec((tm, GROUP_W), rope_map),
                  pl.BlockSpec((tm, GROUP_W), rope_map), pl.BlockSpec((tm, GROUP_W), rope_map)],
        out_specs=[head_spec(DIFF_DV), head_spec(DIFF_DV), head_spec(LANES)]
                  + [pl.BlockSpec((tm, w), row) for w in widths],
        out_shape=[head_out(DIFF_DV), head_out(DIFF_DV), head_out(LANES)] + [f32_out(w) for w in widths],
        compiler_params=_cparams(("arbitrary",)),
        name="in_projection",
    )(x, mod, g, w_wide, *rope)


def _attn_kernel(lam_ref, g_ref, q_ref, *rest, lam_init, has_lat):
    if has_lat:
        kl_ref, vl_ref, kc_ref, vc_ref, o_ref = rest
        keys = [(kl_ref, vl_ref), (kc_ref, vc_ref)]
    else:
        kc_ref, vc_ref, o_ref = rest
        keys = [(kc_ref, vc_ref)]
    lp = lam_ref[...]
    lam = (jnp.exp(jnp.sum(lp[0:1] * lp[1:2], axis=-1, keepdims=True))
           - jnp.exp(jnp.sum(lp[2:3] * lp[3:4], axis=-1, keepdims=True)) + lam_init)
    q = q_ref[...]
    tq = q.shape[0]
    lane = lax.broadcasted_iota(jnp.int32, q.shape, 1)
    chunks = []
    for k_ref, v_ref in keys:
        size = min(ATT_KEY_CHUNK, k_ref.shape[0])
        chunks += [(k_ref, v_ref, s0, size) for s0 in range(0, k_ref.shape[0], size)]

    def lane_groups(t):
        return [t[:, c0:c0 + LANES] for c0 in range(0, t.shape[1], LANES)]

    qm = [jnp.where((lane >= m * DIFF_DQK) & (lane < (m + 1) * DIFF_DQK), q, jnp.zeros_like(q)) for m in range(2)]

    def score(m, j):
        k_ref, _, s0, size = chunks[j]
        return _dot_nt(qm[m], k_ref[s0:s0 + size, :])

    def row_max(scores):
        wide = functools.reduce(jnp.maximum, [g for t in scores for g in lane_groups(t)])
        return jnp.max(wide, axis=-1, keepdims=True)

    def weighted(m, j, s, mx, acc):
        _, v_ref, s0, size = chunks[j]
        return acc + _dot(jnp.exp2(s - mx).astype(BF16), v_ref[s0:s0 + size, :])

    n = len(chunks)
    s1 = [score(0, j) for j in range(n)]
    mx1 = row_max(s1)
    s2 = []
    acc1 = jnp.zeros((tq, LANES), F32)
    for j in range(n):
        s2.append(score(1, j))
        acc1 = weighted(0, j, s1[j], mx1, acc1)
    mx2 = row_max(s2)
    acc2 = jnp.zeros((tq, LANES), F32)
    for j in range(n):
        acc2 = weighted(1, j, s2[j], mx2, acc2)
    o = (acc1[:, :DIFF_DV] * (1.0 / acc1[:, DIFF_DV:DIFF_DV + 1])
         - lam * (acc2[:, :DIFF_DV] * (1.0 / acc2[:, DIFF_DV:DIFF_DV + 1])))
    o_ref[...] = _rms(o) * g_ref[...] * (1.0 - lam_init)


def _diff_attention(lam_p, subln_g, aq, ak, av, geo, lam_init, prev=None):
    batch, seq, ctx = geo["batch"], geo["seq"], geo["ctx"]
    nt = aq.shape[1]
    has_lat = prev is None
    tq = ROW_TILE
    n_q = (seq if has_lat else ctx) // tq
    q_off = 0 if has_lat else (batch * seq) // tq
    ctx_blk0 = (batch * seq) // ctx

    qmap = lambda b, h, i: (h, q_off + b * n_q + i, 0)
    lat_map = lambda b, h, i: (h, b, 0)
    ctx_map = lambda b, h, i: (h, ctx_blk0 + b, 0)
    const = lambda b, h, i: (0, 0)
    in_specs = [pl.BlockSpec((4, DIFF_DQK), const), pl.BlockSpec((1, DIFF_DV), const),
                pl.BlockSpec((None, tq, DIFF_DV), qmap)]
    args = [lam_p, subln_g, aq]
    if has_lat:
        in_specs += [pl.BlockSpec((None, seq, DIFF_DV), lat_map), pl.BlockSpec((None, seq, LANES), lat_map)]
        args += [ak, av]
    in_specs += [pl.BlockSpec((None, ctx, DIFF_DV), ctx_map), pl.BlockSpec((None, ctx, LANES), ctx_map)]
    args += [ak, av]
    aliases = {}
    if not has_lat:
        in_specs.append(pl.BlockSpec(memory_space=pl.ANY))
        args.append(prev)
        aliases = {len(args) - 1: 0}
    kern = functools.partial(_attn_kernel, lam_init=lam_init, has_lat=has_lat)
    if not has_lat:
        kern = _drop_last_input(kern, n_in=len(args))
    return pl.pallas_call(
        kern,
        grid=(batch, N_HEADS, n_q),
        in_specs=in_specs,
        out_specs=pl.BlockSpec((None, tq, DIFF_DV), qmap),
        out_shape=jax.ShapeDtypeStruct((N_HEADS, nt, DIFF_DV), F32),
        input_output_aliases=aliases,
        compiler_params=_cparams(("arbitrary", "arbitrary", "arbitrary")),
        name="diff_attention" if has_lat else "diff_attention_ctx",
    )(*args)


def _drop_last_input(kern, n_in):
    def wrapped(*refs):
        return kern(*refs[:n_in - 1], *refs[n_in:])
    return wrapped


def _dft_tables(n):
    k = jnp.arange(n, dtype=jnp.int32)
    ang = ((k[:, None] * k[None, :]) % (2 * n)).astype(F32) * (math.pi / n)
    return jnp.cos(ang).astype(BF16), jnp.sin(ang).astype(BF16)


def _hy_filter_kernel(w1t_ref, w1c_ref, w1s_ref, b1_ref, w2_ref, b2_ref, w3_ref, fr_ref,
                      bands_ref, deltas_ref, e_ref, d_ref, nyq_ref, *, n):
    pos_i = lax.broadcasted_iota(jnp.int32, (n, 1), 0)
    pos = pos_i.astype(F32)
    t = pos / (n - 1)
    ang = ((2.0 * math.pi) * pos / n) * bands_ref[...]
    pre = t * w1t_ref[...] + _dot_exact(jnp.cos(ang), w1c_ref[...]) - _dot_exact(jnp.sin(ang), w1s_ref[...])
    hdn = jnp.sin(fr_ref[0:1, :] * (pre + b1_ref[...]))
    hdn = jnp.sin(fr_ref[1:2, :] * (_dot_exact(hdn, w2_ref[...]) + b2_ref[...]))
    raw = _dot_exact(hdn, w3_ref[...])
    window = jnp.exp(-t * deltas_ref[...])
    hf = raw[:, :HY_CH] * window
    hb = jnp.where(pos_i > 0, raw[:, HY_CH:] * window, 0.0)
    inv = 1.0 / (jnp.sum(jnp.abs(hf), axis=0, keepdims=True) + jnp.sum(jnp.abs(hb), axis=0, keepdims=True))
    e = (hf + hb) * inv
    e_ref[...] = e.astype(BF16)
    d_ref[...] = ((hb - hf) * inv).astype(BF16)
    sign = (1 - 2 * (pos_i & 1)).astype(F32)
    nyq_ref[...] = jnp.sum(e * sign, axis=0, keepdims=True)


def _hyena_filter(n, w1, b1, w2, b2, w3, freq):
    bands = jnp.linspace(1e-4, HY_BANDS - 1, HY_BANDS, dtype=F32).reshape(1, HY_BANDS)
    max_decay = math.log(HY_DECAY_TARGET) / HY_FAST_DECAY_PCT
    min_decay = math.log(HY_DECAY_TARGET) / HY_SLOW_DECAY_PCT
    deltas = jnp.abs(jnp.linspace(min_decay, max_decay, HY_CH, dtype=F32)).reshape(1, HY_CH)
    args = [w1[0:1], w1[1:1 + HY_BANDS], w1[1 + HY_BANDS:], b1.reshape(1, HY_FFN), w2, b2.reshape(1, HY_FFN),
            w3, freq, bands, deltas]
    return pl.pallas_call(
        functools.partial(_hy_filter_kernel, n=n),
        out_shape=[jax.ShapeDtypeStruct((n, HY_CH), BF16), jax.ShapeDtypeStruct((n, HY_CH), BF16),
                   jax.ShapeDtypeStruct((1, HY_CH), F32)],
        compiler_params=pltpu.CompilerParams(vmem_limit_bytes=VMEM_LIMIT),
        name="hyena_filter",
    )(*args)


def _hy_gate_kernel(x0_ref, x1_ref, v_ref, w0_ref, w1_ref, wv_ref, b0_ref, b1_ref, bv_ref,
                    zb_ref, x0c_ref, zf_ref, nyq_ref, *, n):
    row = lax.broadcasted_iota(jnp.int32, (n, 1), 0)

    def conv(u_ref, w_ref, b_ref):
        u = u_ref[...]
        up = jnp.where(row > 0, pltpu.roll(u, 1, 0), 0.0)
        dn = jnp.where(row < n - 1, pltpu.roll(u, n - 1, 0), 0.0)
        return up * w_ref[0:1, :] + u * w_ref[1:2, :] + dn * w_ref[2:3, :] + b_ref[...]

    z = conv(x1_ref, w1_ref, b1_ref) * conv(v_ref, wv_ref, bv_ref)
    x0c_ref[...] = conv(x0_ref, w0_ref, b0_ref)
    zf_ref[...] = z
    zb_ref[...] = z.astype(BF16)
    sign = (1 - 2 * (row & 1)).astype(F32)
    nyq_ref[...] = jnp.sum(z * sign, axis=0, keepdims=True)


def _hyena_gate(bu, short_w, short_b, n, batch, row_blk0):
    halves = HY_CH // LANES
    sb = short_b.reshape(1, 3 * HY_CH)
    seg = lambda part: pl.BlockSpec((n, LANES), lambda b, j: (row_blk0 + b, part * halves + j))
    wsp = lambda part: pl.BlockSpec((3, LANES), lambda b, j: (0, part * halves + j))
    bsp = lambda part: pl.BlockSpec((1, LANES), lambda b, j: (0, part * halves + j))
    return pl.pallas_call(
        functools.partial(_hy_gate_kernel, n=n),
        grid=(batch, halves),
        in_specs=[seg(0), seg(1), seg(2), wsp(0), wsp(1), wsp(2), bsp(0), bsp(1), bsp(2)],
        out_specs=[pl.BlockSpec((n, LANES), lambda b, j: (0, b * halves + j)),
                   pl.BlockSpec((n, LANES), lambda b, j: (b, j)),
                   pl.BlockSpec((n, LANES), lambda b, j: (b, j)),
                   pl.BlockSpec((None, 1, LANES), lambda b, j: (b, 0, j))],
        out_shape=[jax.ShapeDtypeStruct((n, batch * HY_CH), BF16),
                   jax.ShapeDtypeStruct((batch * n, HY_CH), F32),
                   jax.ShapeDtypeStruct((batch * n, HY_CH), F32),
                   jax.ShapeDtypeStruct((batch, 1, HY_CH), F32)],
        compiler_params=_cparams(("arbitrary", "arbitrary")),
        name="hyena_gate",
    )(bu, bu, bu, short_w, short_w, short_w, sb, sb, sb)


def _hy_spectrum_kernel(c_ref, s_ref, z_ref, e_ref, d_ref, yr_ref, yi_ref, *, n, tk, batch):
    c = c_ref[...]
    s = s_ref[...]
    zr = _dot(c, z_ref[...])
    zs = _dot(s, z_ref[...])
    fr = _dot(c, e_ref[...])
    fi = _dot(s, d_ref[...])
    k = pl.program_id(0) * tk + lax.broadcasted_iota(jnp.int32, (tk, 1), 0)
    wk = jnp.where(k == 0, 1.0, 2.0) * (1.0 / (2 * n))
    for b in range(batch):
        sl = slice(b * HY_CH, (b + 1) * HY_CH)
        yr = zr[:, sl] * fr + zs[:, sl] * fi
        yi = zr[:, sl] * fi - zs[:, sl] * fr
        yr_ref[:, sl] = (yr * wk).astype(BF16)
        yi_ref[:, sl] = (-(yi * wk)).astype(BF16)


def _hy_inverse_kernel(c_ref, s_ref, yr_ref, yi_ref, x0c_ref, zf_ref, nyqz_ref, nyqf_ref, bias_ref,
                       *rest, n, tt, batch):
    o_ref, y_scr = rest[-2], rest[-1]
    b = pl.program_id(1)

    @pl.when(b == 0)
    def _():
        y = _dot(c_ref[...], yr_ref[...]) + _dot(s_ref[...], yi_ref[...])
        for bb in range(batch):
            y_scr[bb] = y[:, bb * HY_CH:(bb + 1) * HY_CH]

    t = pl.program_id(0) * tt + lax.broadcasted_iota(jnp.int32, (tt, 1), 0)
    sign = (1 - 2 * (t & 1)).astype(F32)
    nyq = nyqz_ref[...] * nyqf_ref[...] * (1.0 / (2 * n))
    zf = zf_ref[...]
    o_ref[...] = x0c_ref[...] * (y_scr[b] + sign * nyq + bias_ref[...] * zf)


def _hyena_conv(cs, zb, e, d, x0c, zf, nyqz, nyqf, bias, n, batch, nt, out_blk0, prev=None):
    c_tab, s_tab = cs
    bw = batch * HY_CH
    tk = min(512, n)
    whole = lambda shape: pl.BlockSpec(shape, lambda *_: (0,) * len(shape), pipeline_mode=pl.Buffered(1))
    yr, yi = pl.pallas_call(
        functools.partial(_hy_spectrum_kernel, n=n, tk=tk, batch=batch),
        grid=(n // tk,),
        in_specs=[pl.BlockSpec((tk, n), lambda i: (i, 0)), pl.BlockSpec((tk, n), lambda i: (i, 0)),
                  whole((n, bw)), whole((n, HY_CH)), whole((n, HY_CH))],
        out_specs=[pl.BlockSpec((tk, bw), lambda i: (i, 0))] * 2,
        out_shape=[jax.ShapeDtypeStruct((n, bw), BF16)] * 2,
        compiler_params=_cparams(("arbitrary",)),
        name="hyena_spectrum",
    )(c_tab, s_tab, zb, e, d)

    tt = min(512, n)
    n_t = n // tt
    seg = lambda i, b: (b * n_t + i, 0)
    in_specs = [pl.BlockSpec((tt, n), lambda i, b: (i, 0)), pl.BlockSpec((tt, n), lambda i, b: (i, 0)),
                whole((n, bw)), whole((n, bw)),
                pl.BlockSpec((tt, HY_CH), seg), pl.BlockSpec((tt, HY_CH), seg),
                pl.BlockSpec((None, 1, HY_CH), lambda i, b: (b, 0, 0)),
                pl.BlockSpec((1, HY_CH), lambda i, b: (0, 0)), pl.BlockSpec((1, HY_CH), lambda i, b: (0, 0))]
    args = [c_tab, s_tab, yr, yi, x0c, zf, nyqz, nyqf, bias.reshape(1, HY_CH)]
    aliases = {}
    if prev is not None:
        in_specs.append(pl.BlockSpec(memory_space=pl.ANY))
        args.append(prev)
        aliases = {len(args) - 1: 0}
    return pl.pallas_call(
        functools.partial(_hy_inverse_kernel, n=n, tt=tt, batch=batch),
        grid=(n_t, batch),
        in_specs=in_specs,
        out_specs=pl.BlockSpec((tt, HY_CH), lambda i, b: (out_blk0 + b * n_t + i, 0)),
        out_shape=jax.ShapeDtypeStruct((nt, HY_CH), F32),
        scratch_shapes=[pltpu.VMEM((batch, tt, HY_CH), F32)],
        input_output_aliases=aliases,
        compiler_params=_cparams(("arbitrary", "arbitrary")),
        name="hyena_inverse" if prev is None else "hyena_inverse_ctx",
    )(*args)


def _iota(shape, axis):
    return lax.broadcasted_iota(jnp.int32, shape, axis)


def _scan_maps(geo, cc):
    batch, seq, ctx = geo["batch"], geo["seq"], geo["ctx"]
    n_cc, n_lc = ctx // cc, seq // cc
    ctx0 = (batch * seq) // cc

    def fwd(b, i):
        return (jnp.where(i < n_cc, ctx0 + b * n_cc + i, b * n_lc + i - n_cc), 0)

    def bwd(b, i):
        return (jnp.where(i < n_cc, ctx0 + b * n_cc + (n_cc - 1 - i), b * n_lc + (n_lc - 1 - (i - n_cc))), 0)

    return fwd, bwd, n_cc + n_lc


def _ret_kernel(qf_ref, kf_ref, vf_ref, qb_ref, kb_ref, vb_ref, dec_ref, of_ref, ob_ref, st_ref):
    @pl.when(pl.program_id(1) == 0)
    def _():
        st_ref[...] = jnp.zeros_like(st_ref)

    cc = qf_ref.shape[0]
    r_i, c_i = _iota((cc, cc), 0), _iota((cc, cc), 1)
    pos = _iota((cc, 1), 0).astype(F32)
    for d, (q_ref, k_ref, v_ref, o_ref) in enumerate(((qf_ref, kf_ref, vf_ref, of_ref),
                                                      (qb_ref, kb_ref, vb_ref, ob_ref))):
        reverse = d == 1
        lg = -jnp.exp(dec_ref[d])
        steps_in = (cc - pos) if reverse else (pos + 1.0)
        steps_out = pos if reverse else (cc - 1.0 - pos)
        q = q_ref[...]
        k = k_ref[...] * (RET_DH ** -0.5)
        qd = (q * jnp.exp(steps_in * lg)).astype(BF16)
        kd = (k * jnp.exp(steps_out * lg)).astype(BF16)
        qb, kb, vb = q.astype(BF16), k.astype(BF16), v_ref[...].astype(BF16)
        gain = jnp.exp(cc * lg)
        dist = (c_i - r_i) if reverse else (r_i - c_i)
        keep = dist >= 0
        dist_f = jnp.where(keep, dist, 0).astype(F32)
        for h in range(N_HEADS):
            hs = slice(h * RET_DH, (h + 1) * RET_DH)
            decay = jnp.where(keep, jnp.exp(dist_f * lg[:, h * RET_DH:h * RET_DH + 1]), 0.0)
            sc = (_dot_nt(qb[:, hs], kb[:, hs]) * decay).astype(BF16)
            st = st_ref[d, h]
            o_ref[:, hs] = _dot(sc, vb[:, hs]) + _dot_nt(qd[:, hs], st.astype(BF16))
            st_ref[d, h] = st * gain[:, hs] + _dot_tn(vb[:, hs], kd[:, hs])


def _retention(q, k, v, dec, geo):
    nt, w = q.shape
    cc = RET_CHUNK
    fwd, bwd, steps = _scan_maps(geo, cc)
    blk = lambda m: pl.BlockSpec((cc, w), m)
    return pl.pallas_call(
        _ret_kernel,
        grid=(geo["batch"], steps),
        in_specs=[blk(fwd), blk(fwd), blk(fwd), blk(bwd), blk(bwd), blk(bwd),
                  pl.BlockSpec((2, 1, w), lambda b, i: (0, 0, 0))],
        out_specs=[blk(fwd), blk(bwd)],
        out_shape=[jax.ShapeDtypeStruct((nt, w), F32)] * 2,
        scratch_shapes=[pltpu.VMEM((2, N_HEADS, RET_DH, RET_DH), F32)],
        compiler_params=_cparams(("arbitrary", "arbitrary")),
        name="recurrence_ret",
    )(q, k, v, q, k, v, dec)


def _gla_direction(q, k, v, la, st_ref, o_ref, *, reverse):
    cc, wk = q.shape
    wv = v.shape[1]
    r_i, c_i = _iota((cc, cc), 0), _iota((cc, cc), 1)
    incl = (c_i >= r_i) if reverse else (c_i <= r_i)
    cum = _dot_exact(jnp.where(incl, 1.0, 0.0), la)
    cum_end = cum[0:1] if reverse else cum[cc - 1:cc]

    qd = (q * jnp.exp(cum)).astype(BF16)
    kd = (k * jnp.exp(cum_end - cum)).astype(BF16)
    vb = v.astype(BF16)
    st = st_ref[...]
    inter = _dot_nt(qd, st.astype(BF16))
    same_head = (_iota((wv, wk), 0) // GLA_DV) == (_iota((wv, wk), 1) // GLA_DK)
    st_ref[...] = st * jnp.exp(cum_end) + jnp.where(same_head, _dot_tn(vb, kd), 0.0)

    sub = REC_SUB
    hs = N_HEADS * sub
    q_own = (_iota((hs, wk), 0) // sub) == (_iota((hs, wk), 1) // GLA_DK)
    for j in range(cc // sub):
        r0, r1 = j * sub, (j + 1) * sub
        if reverse:
            ka, kb = r0, cc
            base = cum[r1:r1 + 1] if r1 < cc else jnp.zeros((1, wk), F32)
        else:
            ka, kb = 0, r1
            base = cum[r0 - 1:r0] if r0 > 0 else jnp.zeros((1, wk), F32)
        qj = q[r0:r1] * jnp.exp(cum[r0:r1] - base)
        kj = (k[ka:kb] * jnp.exp(jnp.minimum(base - cum[ka:kb], EXP_CLAMP))).astype(BF16)
        q_stack = jnp.where(q_own, jnp.concatenate([qj] * N_HEADS, axis=0), 0.0).astype(BF16)
        rows = r0 + (_iota((hs, kb - ka), 0) % sub)
        cols = ka + _iota((hs, kb - ka), 1)
        keep = (cols >= rows) if reverse else (cols <= rows)
        sc = jnp.where(keep, _dot_nt(q_stack, kj), 0.0).astype(BF16)
        full = _dot(sc, vb[ka:kb])
        lane_head = _iota((sub, wv), 1) // GLA_DV
        oj = inter[r0:r1]
        for h in range(N_HEADS):
            oj = oj + jnp.where(lane_head == h, full[h * sub:(h + 1) * sub], 0.0)
        o_ref[r0:r1, :] = oj


def _gla_kernel(*refs, batch):
    per_batch = refs[:8 * batch]
    gw_ref, gb_ref, of_ref, ob_ref, st_ref = refs[8 * batch:]

    @pl.when(pl.program_id(0) == 0)
    def _():
        st_ref[...] = jnp.zeros_like(st_ref)

    for b in range(batch):
        qf_ref, kf_ref, vf_ref, lf_ref, qb_ref, kb_ref, vb_ref, lb_ref = per_batch[8 * b:8 * b + 8]
        for d, (q_ref, k_ref, v_ref, l_ref, o_ref) in enumerate(((qf_ref, kf_ref, vf_ref, lf_ref, of_ref),
                                                                 (qb_ref, kb_ref, vb_ref, lb_ref, ob_ref))):
            logit = _dot_exact(l_ref[...], gw_ref[d]) + gb_ref[d]
            la = (jnp.minimum(logit, 0.0) - jnp.log(1.0 + jnp.exp(-jnp.abs(logit)))) * (1.0 / GLA_GATE_NORM)
            _gla_direction(q_ref[...] * (GLA_DK ** -0.5), k_ref[...], v_ref[...], la, st_ref.at[b, d],
                           o_ref.at[b], reverse=(d == 1))


def _gla(q, k, v, glr, gw, gb, geo):
    batch, seq, ctx = geo["batch"], geo["seq"], geo["ctx"]
    wq, wv = q.shape[1], v.shape[1]
    cc = REC_CHUNK
    n_cc, n_lc = ctx // cc, seq // cc
    ctx0 = (batch * seq) // cc
    in_specs, args = [], []
    for b in range(batch):
        fwd = lambda i, b=b: (jnp.where(i < n_cc, ctx0 + b * n_cc + i, b * n_lc + i - n_cc), 0)
        bwd = lambda i, b=b: (jnp.where(i < n_cc, ctx0 + b * n_cc + (n_cc - 1 - i),
                                        b * n_lc + (n_lc - 1 - (i - n_cc))), 0)
        for m in (fwd, bwd):
            in_specs += [pl.BlockSpec((cc, wq), m), pl.BlockSpec((cc, wq), m), pl.BlockSpec((cc, wv), m),
                         pl.BlockSpec((cc, LANES), m)]
            args += [q, k, v, glr]
    in_specs += [pl.BlockSpec((2, LANES, wq), lambda i: (0, 0, 0)), pl.BlockSpec((2, 1, wq), lambda i: (0, 0, 0))]
    args += [gw, gb]
    steps = n_cc + n_lc
    out_blk = lambda m: pl.BlockSpec((batch, cc, wv), m)
    return pl.pallas_call(
        functools.partial(_gla_kernel, batch=batch),
        grid=(steps,),
        in_specs=in_specs,
        out_specs=[out_blk(lambda i: (0, i, 0)),
                   out_blk(lambda i: (0, jnp.where(i < n_cc, n_cc - 1 - i, steps - 1 - (i - n_cc)), 0))],
        out_shape=[jax.ShapeDtypeStruct((batch, ctx + seq, wv), F32)] * 2,
        scratch_shapes=[pltpu.VMEM((batch, 2, wv, wq), F32)],
        compiler_params=_cparams(("arbitrary",)),
        name="recurrence_gla",
    )(*args)


def _mixout_kernel(x_ref, mod_ref, at_ref, hy_ref, rf_ref, rb_ref, cg_ref, gf_ref, gb_ref, dr_ref,
                   seg_ref, rg_ref, gg_ref, wo_ref, g1_ref, g2_ref, rw_ref, rbias_ref,
                   x1_ref, h2_ref, ti_ref, tw_ref, rk_ref, cnt_ref):
    @pl.when(pl.program_id(0) == 0)
    def _():
        cnt_ref[...] = jnp.zeros_like(cnt_ref)

    def head_norm(o):
        ms = _dot_exact(o * o, seg_ref[...]) * (1.0 / RET_DH)
        return o * lax.rsqrt(ms + EPS)

    a = jnp.concatenate([at_ref[h] for h in range(N_HEADS)], axis=-1)
    cg = cg_ref[...]
    dr = dr_ref[...]
    rt = head_norm(rf_ref[...] + rb_ref[...]) * rg_ref[...] * (cg * _sigmoid(cg))
    gl = head_norm(gf_ref[...] + gb_ref[...]) * gg_ref[...] * (dr * _sigmoid(dr))
    cat = jnp.concatenate([a, hy_ref[...], rt, gl], axis=-1).astype(BF16)
    y = _dot(cat, wo_ref[...])
    x1 = x_ref[...] + mod_ref[2:3, :] * (_rms(y) * g1_ref[...])
    x1_ref[...] = x1
    h2 = _rms(x1) * g2_ref[...] * (1.0 + mod_ref[4:5, :]) + mod_ref[3:4, :]
    h2_ref[...] = h2

    vals = _dot_exact(h2, rw_ref[...]) + rbias_ref[...]
    tm = vals.shape[0]
    lane = lax.broadcasted_iota(jnp.int32, vals.shape, 1)
    idx_out = jnp.zeros(vals.shape, jnp.int32)
    w_out = jnp.zeros(vals.shape, F32)
    top = None
    den = 0.0
    picks = []
    for r in range(TOP_K):
        m = jnp.max(vals, axis=-1, keepdims=True)
        idx = jnp.min(jnp.where(vals == m, lane, LANES), axis=-1, keepdims=True)
        hit = lane == idx
        vals = jnp.where(hit, -jnp.inf, vals)
        top = m if top is None else top
        e = jnp.exp(m - top)
        den = den + e
        picks.append((idx, e, hit))
    inv = 1.0 / den
    for r, (idx, e, _) in enumerate(picks):
        idx_out = jnp.where(lane == r, idx, idx_out)
        w_out = jnp.where(lane == r, e * inv, w_out)
    ti_ref[...] = idx_out
    tw_ref[...] = w_out

    chosen = functools.reduce(jnp.logical_or, [hit for _, _, hit in picks])
    chosen_f = jnp.where(chosen, 1.0, 0.0)
    earlier = (lax.broadcasted_iota(jnp.int32, (tm, tm), 1) < lax.broadcasted_iota(jnp.int32, (tm, tm), 0))
    before = _dot(jnp.where(earlier, 1.0, 0.0).astype(BF16), chosen_f.astype(BF16)) + cnt_ref[...]
    rk_out = jnp.zeros(vals.shape, jnp.int32)
    for r, (_, _, hit) in enumerate(picks):
        rank = jnp.sum(jnp.where(hit, before, 0.0), axis=-1, keepdims=True)
        rk_out = jnp.where(lane == r, rank.astype(jnp.int32), rk_out)
    rk_ref[...] = rk_out
    cnt_ref[...] = cnt_ref[...] + jnp.sum(chosen_f, axis=0, keepdims=True)


def _mix_out(n_tiles, x, mod, at, hy, rf, rb, cg, gf, gb, dr, seg, rg, gg, wo, g1, g2, rw, rbias, geo):
    nt, d = x.shape
    tm = ROW_TILE
    lat_tiles, per_batch, batch = geo["lat_tiles"], geo["tiles_per_batch"], geo["batch"]

    def mod_map(i):
        return (jnp.where(i < lat_tiles, i // per_batch, batch), 0, 0)

    row = lambda i: (i, 0)
    const = lambda i: (0, 0)
    g_blk = pl.BlockSpec((tm, GROUP_W), row)
    ctx_tiles = geo["ctx"] // tm

    def scan_map(i):
        c = i - lat_tiles
        return (jnp.where(i < lat_tiles, i // per_batch, c // ctx_tiles),
                jnp.where(i < lat_tiles, ctx_tiles + i % per_batch, c % ctx_tiles), 0)

    s_blk = pl.BlockSpec((None, tm, GROUP_W), scan_map)
    return pl.pallas_call(
        _mixout_kernel,
        grid=(n_tiles,),
        in_specs=[pl.BlockSpec((tm, d), row), pl.BlockSpec((None, 6, d), mod_map),
                  pl.BlockSpec((N_HEADS, tm, DIFF_DV), lambda i: (0, i, 0)),
                  g_blk, g_blk, g_blk, g_blk, s_blk, s_blk, g_blk,
                  pl.BlockSpec((GROUP_W, GROUP_W), const), pl.BlockSpec((1, GROUP_W), const),
                  pl.BlockSpec((1, GROUP_W), const),
                  pl.BlockSpec((d, d), const, pipeline_mode=pl.Buffered(1)),
                  pl.BlockSpec((1, d), const), pl.BlockSpec((1, d), const),
                  pl.BlockSpec((d, LANES), const), pl.BlockSpec((1, LANES), const)],
        out_specs=[pl.BlockSpec((tm, d), row), pl.BlockSpec((tm, d), row),
                   pl.BlockSpec((tm, LANES), row), pl.BlockSpec((tm, LANES), row),
                   pl.BlockSpec((tm, LANES), row), pl.BlockSpec((1, LANES), const)],
        out_shape=[jax.ShapeDtypeStruct((nt, d), F32), jax.ShapeDtypeStruct((nt, d), F32),
                   jax.ShapeDtypeStruct((nt, LANES), jnp.int32), jax.ShapeDtypeStruct((nt, LANES), F32),
                   jax.ShapeDtypeStruct((nt, LANES), jnp.int32), jax.ShapeDtypeStruct((1, LANES), F32)],
        compiler_params=_cparams(("arbitrary",)),
        name="mix_out_router",
    )(x, mod, at, hy, rf, rb, cg, gf, gb, dr, seg, rg, gg, wo, g1, g2, rw, rbias)


def _expert_kernel(te_ref, na_ref, xs_ref, w1_ref, b1_ref, w2_ref, b2_ref, perm_ref, ys_ref, w1s, w2s):
    i = pl.program_id(0)
    active = i < na_ref[0]
    fresh = jnp.logical_or(i == 0, te_ref[i] != te_ref[jnp.maximum(i - 1, 0)])
    n_groups = w1_ref.shape[1] // (2 * LANES)

    @pl.when(jnp.logical_and(active, fresh))
    def _():
        for c in range(n_groups):
            cols = slice(c * 2 * LANES, (c + 1) * 2 * LANES)
            w1s[:, cols] = _dot(w1_ref[:, cols].astype(BF16), perm_ref[...]).astype(BF16)
        w2s[...] = w2_ref[...].astype(BF16)

    @pl.when(active)
    def _():
        u = _dot(xs_ref[...].astype(BF16), w1s[...]) + b1_ref[...]
        acts = []
        for c in range(n_groups):
            glu = jnp.minimum(u[:, c * 2 * LANES:c * 2 * LANES + LANES], SWIGLU_LIMIT)
            lin = jnp.clip(u[:, c * 2 * LANES + LANES:(c + 1) * 2 * LANES], -SWIGLU_LIMIT, SWIGLU_LIMIT)
            acts.append((glu * _sigmoid(SWIGLU_ALPHA * glu) * (lin + 1.0)).astype(BF16))
        ys_ref[...] = _dot(jnp.concatenate(acts, axis=-1), w2s[...]) + b2_ref[...]

    @pl.when(jnp.logical_not(active))
    def _():
        ys_ref[...] = jnp.zeros_like(ys_ref)


def _expert_ffn(tile_e, n_active, xs, w1, b1, w2, b2):
    p, d = xs.shape
    tm = MOE_TILE
    de2 = w1.shape[2]
    de = de2 // 2
    sel = np.zeros((2 * LANES, 2 * LANES), np.float32)
    sel[2 * np.arange(LANES), np.arange(LANES)] = 1.0
    sel[2 * np.arange(LANES) + 1, LANES + np.arange(LANES)] = 1.0
    b1g = b1.reshape(N_EXPERTS, de2 // (2 * LANES), LANES, 2).transpose(0, 1, 3, 2).reshape(N_EXPERTS, 1, de2)
    wmap = lambda i, te, na: (te[i], 0, 0)
    xmap = lambda i, te, na: (jnp.minimum(i, jnp.maximum(na[0] - 1, 0)), 0)
    grid_spec = pltpu.PrefetchScalarGridSpec(
        num_scalar_prefetch=2,
        grid=(p // tm,),
        in_specs=[pl.BlockSpec((tm, d), xmap),
                  pl.BlockSpec((None, d, de2), wmap), pl.BlockSpec((None, 1, de2), wmap),
                  pl.BlockSpec((None, de, d), wmap), pl.BlockSpec((None, 1, d), wmap),
                  pl.BlockSpec((2 * LANES, 2 * LANES), lambda i, te, na: (0, 0))],
        out_specs=pl.BlockSpec((tm, d), lambda i, te, na: (i, 0)),
        scratch_shapes=[pltpu.VMEM((d, de2), BF16), pltpu.VMEM((de, d), BF16)],
    )
    return pl.pallas_call(
        _expert_kernel,
        grid_spec=grid_spec,
        out_shape=jax.ShapeDtypeStruct((p, d), F32),
        compiler_params=_cparams(("arbitrary",)),
        name="expert_ffn",
    )(tile_e, n_active, xs, w1, b1g, w2, b2.reshape(N_EXPERTS, 1, d), jnp.asarray(sel, BF16))


def _route_plan(topi, rank, counts, tm):
    n, k = topi.shape
    padded = ((counts + tm - 1) // tm) * tm
    pend = jnp.cumsum(padded)
    pstart = pend - padded
    experts = jnp.arange(N_EXPERTS, dtype=jnp.int32)
    start = jnp.sum(jnp.where(topi[:, :, None] == experts, pstart, 0), axis=-1)
    dest = (start + rank).astype(jnp.int32)
    p = n * k + N_EXPERTS * tm
    tile_start = jnp.arange(p // tm, dtype=jnp.int32) * tm
    tile_e = jnp.minimum(jnp.sum((tile_start[:, None] >= pend[None, :]).astype(jnp.int32), axis=1), N_EXPERTS - 1)
    n_active = (pend[-1:] // tm).astype(jnp.int32)
    return dest, tile_e, n_active, p, jnp.stack([pstart, pend]).astype(jnp.int32)


def _dispatch_kernel(bounds_ref, dest_ref, x_ref, out_ref, zeros, sem, zsem, *, tb):
    n_rows = tb * TOP_K

    @pl.when(pl.program_id(0) == 0)
    def _():
        zeros[...] = jnp.zeros_like(zeros)

        def fill(e):
            start = pl.multiple_of(bounds_ref[1, e] - MOE_TILE, MOE_TILE)
            return pltpu.make_async_copy(zeros, out_ref.at[pl.ds(start, MOE_TILE)], zsem)

        for e in range(N_EXPERTS):
            @pl.when(bounds_ref[1, e] > bounds_ref[0, e])
            def _():
                fill(e).start()
        for e in range(N_EXPERTS):
            @pl.when(bounds_ref[1, e] > bounds_ref[0, e])
            def _():
                fill(e).wait()

    def issue(t, carry):
        for k in range(TOP_K):
            pltpu.make_async_copy(x_ref.at[pl.ds(t, 1)], out_ref.at[pl.ds(dest_ref[0, t * TOP_K + k], 1)],
                                  sem).start()
        return carry

    lax.fori_loop(0, tb, issue, 0, unroll=4)
    pltpu.make_async_copy(out_ref.at[pl.ds(0, n_rows)], out_ref.at[pl.ds(0, n_rows)], sem).wait()


def _dispatch(h2, dest, bounds, p_rows, n_tok):
    d = h2.shape[1]
    tb = ROW_TILE
    steps = n_tok // tb
    return pl.pallas_call(
        functools.partial(_dispatch_kernel, tb=tb),
        grid=(steps,),
        in_specs=[pl.BlockSpec(memory_space=pltpu.SMEM),
                  pl.BlockSpec((None, 1, tb * TOP_K), lambda i: (i, 0, 0), memory_space=pltpu.SMEM),
                  pl.BlockSpec((tb, d), lambda i: (i, 0))],
        out_specs=pl.BlockSpec(memory_space=pl.ANY),
        out_shape=jax.ShapeDtypeStruct((p_rows, d), F32),
        scratch_shapes=[pltpu.VMEM((MOE_TILE, d), F32), pltpu.SemaphoreType.DMA(()), pltpu.SemaphoreType.DMA(())],
        compiler_params=pltpu.CompilerParams(dimension_semantics=("arbitrary",), has_side_effects=True),
        name="moe_dispatch",
    )(bounds, dest.reshape(steps, 1, tb * TOP_K), h2)


def _combine_kernel(dest_ref, x_ref, tw_ref, mod_ref, g_ref, ys_ref, o_ref, buf, sem, *, tb):
    n_rows = tb * TOP_K

    def issue(r, carry):
        pltpu.make_async_copy(ys_ref.at[pl.ds(dest_ref[0, r], 1)], buf.at[pl.ds(r, 1)], sem).start()
        return carry

    lax.fori_loop(0, n_rows, issue, 0, unroll=8)
    pltpu.make_async_copy(ys_ref.at[pl.ds(0, n_rows)], buf, sem).wait()
    y = None
    for k in range(TOP_K):
        t = buf[k * tb:(k + 1) * tb, :] * tw_ref[:, k:k + 1]
        y = t if y is None else y + t
    o_ref[...] = x_ref[...] + mod_ref[5:6, :] * (_rms(y) * g_ref[...])


def _combine_residual(n_tiles, x, ys, dest, topw, mod, g, geo):
    nt, d = x.shape
    tb = ROW_TILE
    lat_tiles, per_batch, batch = geo["lat_tiles"], geo["tiles_per_batch"], geo["batch"]

    def mod_map(i):
        return (jnp.where(i < lat_tiles, i // per_batch, batch), 0, 0)

    row = lambda i: (i, 0)
    slot_major = jnp.swapaxes(dest.reshape(n_tiles, tb, TOP_K), 1, 2).reshape(n_tiles, 1, tb * TOP_K)
    return pl.pallas_call(
        functools.partial(_combine_kernel, tb=tb),
        grid=(n_tiles,),
        in_specs=[pl.BlockSpec((None, 1, tb * TOP_K), lambda i: (i, 0, 0), memory_space=pltpu.SMEM),
                  pl.BlockSpec((tb, d), row), pl.BlockSpec((tb, LANES), row),
                  pl.BlockSpec((None, 6, d), mod_map), pl.BlockSpec((1, d), lambda i: (0, 0)),
                  pl.BlockSpec(memory_space=pl.ANY)],
        out_specs=pl.BlockSpec((tb, d), row),
        out_shape=jax.ShapeDtypeStruct((nt, d), F32),
        scratch_shapes=[pltpu.VMEM((tb * TOP_K, d), F32), pltpu.SemaphoreType.DMA(())],
        compiler_params=_cparams(("arbitrary",)),
        name="combine_residual",
    )(slot_major, x, topw, mod, g, ys)


def _rope_partner(dim):
    h, q = dim // 2, dim // 4
    perm = np.zeros(dim, np.int32)
    sign = np.zeros(dim, np.float32)
    for base in (0, h):
        for j in range(q):
            perm[base + j], sign[base + j] = base + j + q, -1.0
            perm[base + q + j], sign[base + q + j] = base + j, 1.0
    return perm, sign


def _rot_cols(w, dim):
    perm, sign = _rope_partner(dim)
    reps = w.shape[1] // dim
    full_perm = np.concatenate([perm + r * dim for r in range(reps)])
    return w[:, full_perm] * jnp.asarray(np.tile(sign, reps))


def _rope_tables(n_tok, dim, reps, pad_rows):
    rows = n_tok // GRID_W
    row = jnp.repeat(jnp.arange(rows, dtype=F32), GRID_W)
    col = jnp.tile(jnp.arange(GRID_W, dtype=F32), rows)
    quarter = dim // 4
    inv = ROPE_BASE ** (-jnp.arange(quarter, dtype=F32) / quarter)
    ar = row[:, None] * inv[None]
    ac = col[:, None] * inv[None]
    ang = jnp.concatenate([ar, ar, ac, ac], axis=-1)
    cos = jnp.concatenate([jnp.tile(jnp.cos(ang), (1, reps)), jnp.ones((pad_rows, dim * reps), F32)])
    sin = jnp.concatenate([jnp.tile(jnp.sin(ang), (1, reps)), jnp.zeros((pad_rows, dim * reps), F32)])
    return cos, sin


def _widen_w_in(w):
    sizes = (256, 256, 256, 768, 256, 256, 256, 256, 128, 128, 256, 256, 32)
    cuts = np.cumsum(sizes)[:-1].tolist()
    aq, ak, av, bu, cq, ck, cv, cg, dq, dk, dv, dr, dl = jnp.split(w, cuts, axis=1)
    dl = jnp.pad(dl, ((0, 0), (0, LANES - dl.shape[1])))
    parts = [aq, _rot_cols(aq, DIFF_DQK), ak, _rot_cols(ak, DIFF_DQK), av, bu,
             cq, _rot_cols(cq, RET_DH), ck, _rot_cols(ck, RET_DH), cv, cg, dq, dk, dv, dr, dl]
    return jnp.concatenate(parts, axis=1).astype(BF16)


def kernel(x, c, ctx, c_ctx, w_mod, b_mod, norm_g, w_in, w_out, diff_lambda, diff_subln_g, hy_short_w, hy_short_b, hy_w1, hy_b1, hy_w2, hy_b2, hy_w3, hy_freq, hy_bias, ret_decay, ret_norm_g, gla_gate_w, gla_gate_b, gla_norm_g, router_w, router_b, exp_w1, exp_b1, exp_w2, exp_b2):
    batch, seq, d = x.shape
    n_ctx = ctx.shape[1]
    depth = w_mod.shape[0]
    n_lat_rows, n_ctx_rows = batch * seq, batch * n_ctx
    nt = n_lat_rows + n_ctx_rows
    assert d == D_MODEL and seq % ROW_TILE == 0 and n_ctx % ROW_TILE == 0 and seq % n_ctx == 0
    geo = dict(batch=batch, seq=seq, ctx=n_ctx, lat_tiles=n_lat_rows // ROW_TILE,
               tiles_per_batch=seq // ROW_TILE)

    xs = jnp.concatenate([x.reshape(n_lat_rows, d), ctx.reshape(n_ctx_rows, d)], axis=0)
    mod_rows = 8
    cc = jnp.zeros((mod_rows, d), F32).at[:batch].set(c).at[batch].set(c_ctx)
    mod_all = _modulation(cc, w_mod, b_mod).reshape(depth, mod_rows, 6, d)

    rope = (*_rope_tables(seq, DIFF_DQK, GROUP_W // DIFF_DQK, ROW_TILE),
            *_rope_tables(seq, RET_DH, GROUP_W // RET_DH, ROW_TILE))
    dft_lat = _dft_tables(seq)
    dft_ctx = _dft_tables(n_ctx)
    seg = jnp.asarray(np.kron(np.eye(N_HEADS, dtype=np.float32), np.ones((RET_DH, RET_DH), np.float32)))

    for l in range(depth):
        need_ctx = l < depth - 1
        lam_init = 0.8 - 0.6 * math.exp(-0.3 * l)
        mod = mod_all[l]
        (aq, ak, av, bu, cq, ck, cv, cg, dq, dk, dv, dr, dl) = _in_projection(
            xs, mod, norm_g[l, 0].reshape(1, d), _widen_w_in(w_in[l]), rope, geo)

        sub_g = diff_subln_g[l].reshape(1, DIFF_DV)
        at = _diff_attention(diff_lambda[l], sub_g, aq, ak, av, geo, lam_init)
        if need_ctx:
            at = _diff_attention(diff_lambda[l], sub_g, aq, ak, av, geo, lam_init, prev=at)

        def hyena(n, row_blk0, tables, out_blk0, prev):
            e, dd, nyqf = _hyena_filter(n, hy_w1[l], hy_b1[l], hy_w2[l], hy_b2[l], hy_w3[l], hy_freq[l])
            zb, x0c, zf, nyqz = _hyena_gate(bu, hy_short_w[l], hy_short_b[l], n, batch, row_blk0)
            return _hyena_conv(tables, zb, e, dd, x0c, zf, nyqz, nyqf, hy_bias[l], n, batch, nt, out_blk0, prev)

        hy = hyena(seq, 0, dft_lat, 0, None)
        if need_ctx:
            hy = hyena(n_ctx, n_lat_rows // n_ctx, dft_ctx, n_lat_rows // min(512, n_ctx), hy)

        dec = jnp.repeat(ret_decay[l], RET_DH, axis=-1).reshape(2, 1, N_HEADS * RET_DH)
        rf, rb = _retention(cq, ck, cv, dec, geo)

        gw = jnp.zeros((2, LANES, N_HEADS * GLA_DK), F32)
        gw = gw.at[0, :GLA_RANK].set(gla_gate_w[l, 0]).at[1, GLA_RANK:2 * GLA_RANK].set(gla_gate_w[l, 1])
        gbias = gla_gate_b[l].reshape(2, 1, N_HEADS * GLA_DK)
        gf, gb = _gla(dq, dk, dv, dl, gw, gbias, geo)

        n_tiles = nt // ROW_TILE if need_ctx else n_lat_rows // ROW_TILE
        rw = jnp.pad(router_w[l], ((0, 0), (0, LANES - N_EXPERTS)))
        rbias = jnp.pad(router_b[l], (0, LANES - N_EXPERTS), constant_values=-jnp.inf).reshape(1, LANES)
        x1, h2, topi, topw, rank, cnt = _mix_out(
            n_tiles, xs, mod, at, hy, rf, rb, cg, gf, gb, dr, seg,
            ret_norm_g[l].reshape(1, GROUP_W), jnp.tile(gla_norm_g[l], N_HEADS).reshape(1, GROUP_W),
            w_out[l].astype(BF16), norm_g[l, 1].reshape(1, d), norm_g[l, 2].reshape(1, d), rw, rbias, geo)

        n_tok = n_tiles * ROW_TILE
        dest, tile_e, n_active, p_rows, bounds = _route_plan(
            topi[:n_tok, :TOP_K], rank[:n_tok, :TOP_K], cnt[0, :N_EXPERTS].astype(jnp.int32), MOE_TILE)
        dispatched = _dispatch(h2, dest, bounds, p_rows, n_tok)
        ys = _expert_ffn(tile_e, n_active, dispatched, exp_w1[l], exp_b1[l], exp_w2[l], exp_b2[l])
        xs = _combine_residual(n_tiles, x1, ys, dest, topw, mod, norm_g[l, 3].reshape(1, d), geo)

    return xs[:n_lat_rows].reshape(batch, seq, d)
```

```python
import functools
import math

import numpy as np
import jax
import jax.numpy as jnp
from jax import lax
from jax.experimental import pallas as pl
from jax.experimental.pallas import tpu as pltpu

F32 = jnp.float32
BF16 = jnp.bfloat16
HIGHEST = lax.Precision.HIGHEST

D_MODEL = 1024
GRID_W = 64
GROUP_W = 256
N_HEADS = 4
DIFF_DQK = 32
DIFF_DV = 64
ROPE_BASE = 10000.0
HY_CH = 256
HY_BANDS = 16
HY_FFN = 64
HY_FAST_DECAY_PCT = 0.3
HY_SLOW_DECAY_PCT = 1.5
HY_DECAY_TARGET = 1e-2
RET_DH = 64
GLA_DK = 32
GLA_DV = 64
GLA_RANK = 16
GLA_GATE_NORM = 16.0
N_EXPERTS = 32
TOP_K = 4
D_EXPERT = 1024
SWIGLU_LIMIT = 7.0
SWIGLU_ALPHA = 1.702
EPS = 1e-6

LANES = 128
ROW_TILE = 256
REC_CHUNK = 64
RET_CHUNK = 256
REC_SUB = 16
EXP_CLAMP = 80.0
MOE_TILE = 256
ATT_KEY_CHUNK = 512
DFT_TABLE_ROWS = 128
VMEM_LIMIT = 52 * 1024 * 1024

_A_Q, _A_QR, _A_K, _A_KR, _A_V = 0, 256, 512, 768, 1024
_B_U = 1280
_C_Q, _C_QR, _C_K, _C_KR, _C_V, _C_G = 2048, 2304, 2560, 2816, 3072, 3328
_D_Q, _D_K, _D_V, _D_R, _D_L = 3584, 3712, 3840, 4096, 4352
_IN_COLS = 4480


def _cparams(sem):
    return pltpu.CompilerParams(dimension_semantics=sem, vmem_limit_bytes=VMEM_LIMIT)


def _sigmoid(x):
    return 1.0 / (1.0 + jnp.exp(-x))


def _rms(x):
    return x * lax.rsqrt(jnp.mean(x * x, axis=-1, keepdims=True) + EPS)


def _iota(shape, axis):
    return lax.broadcasted_iota(jnp.int32, shape, axis)


def _dot(a, b):
    return jnp.dot(a, b, preferred_element_type=F32)


def _dot_exact(a, b):
    return jnp.dot(a, b, preferred_element_type=F32, precision=HIGHEST)


def _dot_nt(a, b):
    return lax.dot_general(a, b, (((1,), (1,)), ((), ())), preferred_element_type=F32)


def _dot_tn(a, b):
    return lax.dot_general(a, b, (((0,), (0,)), ((), ())), preferred_element_type=F32)


def _mod_kernel(c_ref, w_ref, b_ref, o_ref):
    c = c_ref[...]
    o_ref[...] = _dot_exact(c * _sigmoid(c), w_ref[...]) + b_ref[...]


def _modulation(cc, w_mod, b_mod):
    depth, d, n = w_mod.shape
    tn = n // 4
    rows = cc.shape[0]
    return pl.pallas_call(
        _mod_kernel,
        grid=(depth, n // tn),
        in_specs=[pl.BlockSpec((rows, d), lambda l, j: (0, 0)),
                  pl.BlockSpec((None, d, tn), lambda l, j: (l, 0, j)),
                  pl.BlockSpec((None, 1, tn), lambda l, j: (l, 0, j))],
        out_specs=pl.BlockSpec((None, rows, tn), lambda l, j: (l, 0, j)),
        out_shape=jax.ShapeDtypeStruct((depth, rows, n), F32),
        compiler_params=_cparams(("arbitrary", "arbitrary")),
        name="modulation",
    )(cc, w_mod, b_mod.reshape(depth, 1, n))


def _inproj_kernel(x_ref, mod_ref, g_ref, w_ref, cosa_ref, sina_ref, cosc_ref, sinc_ref,
                   aq_ref, ak_ref, av_ref, bu_ref, cq_ref, ck_ref, cv_ref, cg_ref,
                   dq_ref, dk_ref, dv_ref, dr_ref, dl_ref):
    xn = _rms(x_ref[...]) * g_ref[...]
    h = (xn * (1.0 + mod_ref[1:2, :]) + mod_ref[0:1, :]).astype(BF16)

    def proj(a, width):
        return _dot(h, w_ref[:, a:a + width])

    def roped(a, a_rot, cos_ref, sin_ref):
        return proj(a, GROUP_W) * cos_ref[...] + proj(a_rot, GROUP_W) * sin_ref[...]

    aq = roped(_A_Q, _A_QR, cosa_ref, sina_ref) * (DIFF_DQK ** -0.5 * math.log2(math.e))
    ak = roped(_A_K, _A_KR, cosa_ref, sina_ref)
    av = proj(_A_V, GROUP_W)
    ones_col = jnp.where(lax.broadcasted_iota(jnp.int32, (av.shape[0], LANES - DIFF_DV), 1) == 0, 1.0, 0.0)
    for hd in range(N_HEADS):
        sl = slice(hd * DIFF_DV, (hd + 1) * DIFF_DV)
        aq_ref[hd] = aq[:, sl].astype(BF16)
        ak_ref[hd] = ak[:, sl].astype(BF16)
        av_ref[hd] = jnp.concatenate([av[:, sl], ones_col], axis=-1).astype(BF16)
    bu_ref[...] = proj(_B_U, 3 * HY_CH)
    cq_ref[...] = roped(_C_Q, _C_QR, cosc_ref, sinc_ref)
    ck_ref[...] = roped(_C_K, _C_KR, cosc_ref, sinc_ref)
    cv_ref[...] = proj(_C_V, GROUP_W)
    cg_ref[...] = proj(_C_G, GROUP_W)
    dq_ref[...] = proj(_D_Q, LANES)
    dk_ref[...] = proj(_D_K, LANES)
    dv_ref[...] = proj(_D_V, GROUP_W)
    dr_ref[...] = proj(_D_R, GROUP_W)
    dl_ref[...] = proj(_D_L, LANES)


def _in_projection(x, mod, g, w_wide, rope, geo):
    nt, d = x.shape
    tm = ROW_TILE
    lat_tiles, per_batch, batch = geo["lat_tiles"], geo["tiles_per_batch"], geo["batch"]

    def mod_map(i):
        return (jnp.where(i < lat_tiles, i // per_batch, batch), 0, 0)

    def rope_map(i):
        return (jnp.where(i < lat_tiles, i % per_batch, per_batch), 0)

    row = lambda i: (i, 0)
    head = lambda i: (0, i, 0)
    const = lambda i: (0, 0)
    f32_out = lambda w: jax.ShapeDtypeStruct((nt, w), F32)
    head_out = lambda w: jax.ShapeDtypeStruct((N_HEADS, nt, w), BF16)
    head_spec = lambda w: pl.BlockSpec((N_HEADS, tm, w), head)
    widths = [3 * HY_CH, GROUP_W, GROUP_W, GROUP_W, GROUP_W, LANES, LANES, GROUP_W, GROUP_W, LANES]
    return pl.pallas_call(
        _inproj_kernel,
        grid=(nt // tm,),
        in_specs=[pl.BlockSpec((tm, d), row),
                  pl.BlockSpec((None, 6, d), mod_map),
                  pl.BlockSpec((1, d), const),
                  pl.BlockSpec((d, _IN_COLS), const, pipeline_mode=pl.Buffered(1)),
                  pl.BlockSpec((tm, GROUP_W), rope_map), pl.BlockSpec((tm, GROUP_W), rope_map),
                  pl.BlockSpec((tm, GROUP_W), rope_map), pl.BlockSpec((tm, GROUP_W), rope_map)],
        out_specs=[head_spec(DIFF_DV), head_spec(DIFF_DV), head_spec(LANES)]
                  + [pl.BlockSpec((tm, w), row) for w in widths],
        out_shape=[head_out(DIFF_DV), head_out(DIFF_DV), head_out(LANES)] + [f32_out(w) for w in widths],
        compiler_params=_cparams(("arbitrary",)),
        name="in_projection",
    )(x, mod, g, w_wide, *rope)


def _attn_kernel(lam_ref, g_ref, q_ref, *rest, lam_init, has_lat):
    if has_lat:
        kl_ref, vl_ref, kc_ref, vc_ref, o_ref = rest
        keys = [(kl_ref, vl_ref), (kc_ref, vc_ref)]
    else:
        kc_ref, vc_ref, o_ref = rest
        keys = [(kc_ref, vc_ref)]
    lp = lam_ref[...]
    lam = (jnp.exp(jnp.sum(lp[0:1] * lp[1:2], axis=-1, keepdims=True))
           - jnp.exp(jnp.sum(lp[2:3] * lp[3:4], axis=-1, keepdims=True)) + lam_init)
    q = q_ref[...]
    tq = q.shape[0]
    lane = lax.broadcasted_iota(jnp.int32, q.shape, 1)
    chunks = []
    for k_ref, v_ref in keys:
        size = min(ATT_KEY_CHUNK, k_ref.shape[0])
        chunks += [(k_ref, v_ref, s0, size) for s0 in range(0, k_ref.shape[0], size)]

    def lane_groups(t):
        return [t[:, c0:c0 + LANES] for c0 in range(0, t.shape[1], LANES)]

    qm = [jnp.where((lane >= m * DIFF_DQK) & (lane < (m + 1) * DIFF_DQK), q, jnp.zeros_like(q)) for m in range(2)]

    def score(m, j):
        k_ref, _, s0, size = chunks[j]
        return _dot_nt(qm[m], k_ref[s0:s0 + size, :])

    def row_max(scores):
        wide = functools.reduce(jnp.maximum, [g for t in scores for g in lane_groups(t)])
        return jnp.max(wide, axis=-1, keepdims=True)

    def weighted(m, j, s, mx, acc):
        _, v_ref, s0, size = chunks[j]
        return acc + _dot(jnp.exp2(s - mx).astype(BF16), v_ref[s0:s0 + size, :])

    n = len(chunks)
    s1 = [score(0, j) for j in range(n)]
    mx1 = row_max(s1)
    s2 = []
    acc1 = jnp.zeros((tq, LANES), F32)
    for j in range(n):
        s2.append(score(1, j))
        acc1 = weighted(0, j, s1[j], mx1, acc1)
    mx2 = row_max(s2)
    acc2 = jnp.zeros((tq, LANES), F32)
    for j in range(n):
        acc2 = weighted(1, j, s2[j], mx2, acc2)
    o = (acc1[:, :DIFF_DV] * (1.0 / acc1[:, DIFF_DV:DIFF_DV + 1])
         - lam * (acc2[:, :DIFF_DV] * (1.0 / acc2[:, DIFF_DV:DIFF_DV + 1])))
    o_ref[...] = _rms(o) * g_ref[...] * (1.0 - lam_init)


def _diff_attention(lam_p, subln_g, aq, ak, av, geo, lam_init, prev=None):
    batch, seq, ctx = geo["batch"], geo["seq"], geo["ctx"]
    nt = aq.shape[1]
    has_lat = prev is None
    tq = ROW_TILE
    n_q = (seq if has_lat else ctx) // tq
    q_off = 0 if has_lat else (batch * seq) // tq
    ctx_blk0 = (batch * seq) // ctx

    qmap = lambda b, h, i: (h, q_off + b * n_q + i, 0)
    lat_map = lambda b, h, i: (h, b, 0)
    ctx_map = lambda b, h, i: (h, ctx_blk0 + b, 0)
    const = lambda b, h, i: (0, 0)
    in_specs = [pl.BlockSpec((4, DIFF_DQK), const), pl.BlockSpec((1, DIFF_DV), const),
                pl.BlockSpec((None, tq, DIFF_DV), qmap)]
    args = [lam_p, subln_g, aq]
    if has_lat:
        in_specs += [pl.BlockSpec((None, seq, DIFF_DV), lat_map), pl.BlockSpec((None, seq, LANES), lat_map)]
        args += [ak, av]
    in_specs += [pl.BlockSpec((None, ctx, DIFF_DV), ctx_map), pl.BlockSpec((None, ctx, LANES), ctx_map)]
    args += [ak, av]
    aliases = {}
    if not has_lat:
        in_specs.append(pl.BlockSpec(memory_space=pl.ANY))
        args.append(prev)
        aliases = {len(args) - 1: 0}
    kern = functools.partial(_attn_kernel, lam_init=lam_init, has_lat=has_lat)
    if not has_lat:
        kern = _drop_last_input(kern, n_in=len(args))
    return pl.pallas_call(
        kern,
        grid=(batch, N_HEADS, n_q),
        in_specs=in_specs,
        out_specs=pl.BlockSpec((None, tq, DIFF_DV), qmap),
        out_shape=jax.ShapeDtypeStruct((N_HEADS, nt, DIFF_DV), F32),
        input_output_aliases=aliases,
        compiler_params=_cparams(("arbitrary", "arbitrary", "arbitrary")),
        name="diff_attention" if has_lat else "diff_attention_ctx",
    )(*args)


def _drop_last_input(kern, n_in):
    def wrapped(*refs):
        return kern(*refs[:n_in - 1], *refs[n_in:])
    return wrapped


def _dft_table_kernel(cx_ref, sx_ref, cy_ref, sy_ref, c_ref, s_ref, *, n, span):
    t = _iota((LANES, n), 1)
    row = _iota((LANES, n), 0)
    pick_a = jnp.where(t // span == row, 1.0, 0.0).astype(BF16)
    pick_b = jnp.where(t % span == row, 1.0, 0.0).astype(BF16)

    def widen(ref, pick):
        x = ref[...]
        hi = x.astype(BF16)
        lo = (x - hi.astype(F32)).astype(BF16)
        return _dot(hi, pick) + _dot(lo, pick)

    cx, sx = widen(cx_ref, pick_a), widen(sx_ref, pick_a)
    cy, sy = widen(cy_ref, pick_b), widen(sy_ref, pick_b)
    c_ref[...] = (cx * cy - sx * sy).astype(BF16)
    s_ref[...] = (sx * cy + cx * sy).astype(BF16)


def _dft_tables(n):
    span = 64
    assert n % span == 0 and n // span <= LANES
    k = jnp.arange(n, dtype=jnp.int32)[:, None]
    j = jnp.arange(LANES, dtype=jnp.int32)[None, :]

    def factor(step, count):
        ang = ((k * (j * step)) % (2 * n)).astype(F32) * (math.pi / n)
        live = j < count
        return jnp.where(live, jnp.cos(ang), 0.0), jnp.where(live, jnp.sin(ang), 0.0)

    cx, sx = factor(span, n // span)
    cy, sy = factor(1, span)
    tk = min(DFT_TABLE_ROWS, n)
    small = pl.BlockSpec((tk, LANES), lambda i: (i, 0))
    big = pl.BlockSpec((tk, n), lambda i: (i, 0))
    return pl.pallas_call(
        functools.partial(_dft_table_kernel, n=n, span=span),
        grid=(n // tk,),
        in_specs=[small] * 4,
        out_specs=[big, big],
        out_shape=[jax.ShapeDtypeStruct((n, n), BF16)] * 2,
        compiler_params=_cparams(("arbitrary",)),
        name="dft_tables",
    )(cx, sx, cy, sy)


def _hy_filter_kernel(w1t_ref, w1c_ref, w1s_ref, b1_ref, w2_ref, b2_ref, w3_ref, fr_ref,
                      bands_ref, deltas_ref, e_ref, d_ref, nyq_ref, *, n):
    pos_i = lax.broadcasted_iota(jnp.int32, (n, 1), 0)
    pos = pos_i.astype(F32)
    t = pos / (n - 1)
    ang = ((2.0 * math.pi) * pos / n) * bands_ref[...]
    pre = t * w1t_ref[...] + _dot_exact(jnp.cos(ang), w1c_ref[...]) - _dot_exact(jnp.sin(ang), w1s_ref[...])
    hdn = jnp.sin(fr_ref[0:1, :] * (pre + b1_ref[...]))
    hdn = jnp.sin(fr_ref[1:2, :] * (_dot_exact(hdn, w2_ref[...]) + b2_ref[...]))
    raw = _dot_exact(hdn, w3_ref[...])
    window = jnp.exp(-t * deltas_ref[...])
    hf = raw[:, :HY_CH] * window
    hb = jnp.where(pos_i > 0, raw[:, HY_CH:] * window, 0.0)
    inv = 1.0 / (jnp.sum(jnp.abs(hf), axis=0, keepdims=True) + jnp.sum(jnp.abs(hb), axis=0, keepdims=True))
    e = (hf + hb) * inv
    e_ref[...] = e.astype(BF16)
    d_ref[...] = ((hb - hf) * inv).astype(BF16)
    sign = (1 - 2 * (pos_i & 1)).astype(F32)
    nyq_ref[...] = jnp.sum(e * sign, axis=0, keepdims=True)


def _hyena_filter(n, w1, b1, w2, b2, w3, freq):
    bands = jnp.linspace(1e-4, HY_BANDS - 1, HY_BANDS, dtype=F32).reshape(1, HY_BANDS)
    max_decay = math.log(HY_DECAY_TARGET) / HY_FAST_DECAY_PCT
    min_decay = math.log(HY_DECAY_TARGET) / HY_SLOW_DECAY_PCT
    deltas = jnp.abs(jnp.linspace(min_decay, max_decay, HY_CH, dtype=F32)).reshape(1, HY_CH)
    args = [w1[0:1], w1[1:1 + HY_BANDS], w1[1 + HY_BANDS:], b1.reshape(1, HY_FFN), w2, b2.reshape(1, HY_FFN),
            w3, freq, bands, deltas]
    return pl.pallas_call(
        functools.partial(_hy_filter_kernel, n=n),
        out_shape=[jax.ShapeDtypeStruct((n, HY_CH), BF16), jax.ShapeDtypeStruct((n, HY_CH), BF16),
                   jax.ShapeDtypeStruct((1, HY_CH), F32)],
        compiler_params=pltpu.CompilerParams(vmem_limit_bytes=VMEM_LIMIT),
        name="hyena_filter",
    )(*args)


def _hy_gate_kernel(x0_ref, x1_ref, v_ref, w0_ref, w1_ref, wv_ref, b0_ref, b1_ref, bv_ref,
                    zb_ref, x0c_ref, zf_ref, nyq_ref, *, n):
    row = lax.broadcasted_iota(jnp.int32, (n, 1), 0)

    def conv(u_ref, w_ref, b_ref):
        u = u_ref[...]
        up = jnp.where(row > 0, pltpu.roll(u, 1, 0), 0.0)
        dn = jnp.where(row < n - 1, pltpu.roll(u, n - 1, 0), 0.0)
        return up * w_ref[0:1, :] + u * w_ref[1:2, :] + dn * w_ref[2:3, :] + b_ref[...]

    z = conv(x1_ref, w1_ref, b1_ref) * conv(v_ref, wv_ref, bv_ref)
    x0c_ref[...] = conv(x0_ref, w0_ref, b0_ref)
    zf_ref[...] = z
    zb_ref[...] = z.astype(BF16)
    sign = (1 - 2 * (row & 1)).astype(F32)
    nyq_ref[...] = jnp.sum(z * sign, axis=0, keepdims=True)


def _hyena_gate(bu, short_w, short_b, n, batch, row_blk0):
    halves = HY_CH // LANES
    sb = short_b.reshape(1, 3 * HY_CH)
    seg = lambda part: pl.BlockSpec((n, LANES), lambda b, j: (row_blk0 + b, part * halves + j))
    wsp = lambda part: pl.BlockSpec((3, LANES), lambda b, j: (0, part * halves + j))
    bsp = lambda part: pl.BlockSpec((1, LANES), lambda b, j: (0, part * halves + j))
    return pl.pallas_call(
        functools.partial(_hy_gate_kernel, n=n),
        grid=(batch, halves),
        in_specs=[seg(0), seg(1), seg(2), wsp(0), wsp(1), wsp(2), bsp(0), bsp(1), bsp(2)],
        out_specs=[pl.BlockSpec((n, LANES), lambda b, j: (0, b * halves + j)),
                   pl.BlockSpec((n, LANES), lambda b, j: (b, j)),
                   pl.BlockSpec((n, LANES), lambda b, j: (b, j)),
                   pl.BlockSpec((None, 1, LANES), lambda b, j: (b, 0, j))],
        out_shape=[jax.ShapeDtypeStruct((n, batch * HY_CH), BF16),
                   jax.ShapeDtypeStruct((batch * n, HY_CH), F32),
                   jax.ShapeDtypeStruct((batch * n, HY_CH), F32),
                   jax.ShapeDtypeStruct((batch, 1, HY_CH), F32)],
        compiler_params=_cparams(("arbitrary", "arbitrary")),
        name="hyena_gate",
    )(bu, bu, bu, short_w, short_w, short_w, sb, sb, sb)


def _hy_spectrum_kernel(c_ref, s_ref, z_ref, e_ref, d_ref, yr_ref, yi_ref, *, n, tk, batch):
    c = c_ref[...]
    s = s_ref[...]
    zr = _dot(c, z_ref[...])
    zs = _dot(s, z_ref[...])
    fr = _dot(c, e_ref[...])
    fi = _dot(s, d_ref[...])
    k = pl.program_id(0) * tk + lax.broadcasted_iota(jnp.int32, (tk, 1), 0)
    wk = jnp.where(k == 0, 1.0, 2.0) * (1.0 / (2 * n))
    for b in range(batch):
        sl = slice(b * HY_CH, (b + 1) * HY_CH)
        yr = zr[:, sl] * fr + zs[:, sl] * fi
        yi = zr[:, sl] * fi - zs[:, sl] * fr
        yr_ref[:, sl] = (yr * wk).astype(BF16)
        yi_ref[:, sl] = (-(yi * wk)).astype(BF16)


def _hy_inverse_kernel(c_ref, s_ref, yr_ref, yi_ref, x0c_ref, zf_ref, nyqz_ref, nyqf_ref, bias_ref,
                       *rest, n, tt, batch):
    o_ref, y_scr = rest[-2], rest[-1]
    b = pl.program_id(1)

    @pl.when(b == 0)
    def _():
        y = _dot(c_ref[...], yr_ref[...]) + _dot(s_ref[...], yi_ref[...])
        for bb in range(batch):
            y_scr[bb] = y[:, bb * HY_CH:(bb + 1) * HY_CH]

    t = pl.program_id(0) * tt + lax.broadcasted_iota(jnp.int32, (tt, 1), 0)
    sign = (1 - 2 * (t & 1)).astype(F32)
    nyq = nyqz_ref[...] * nyqf_ref[...] * (1.0 / (2 * n))
    zf = zf_ref[...]
    o_ref[...] = x0c_ref[...] * (y_scr[b] + sign * nyq + bias_ref[...] * zf)


def _hyena_conv(cs, zb, e, d, x0c, zf, nyqz, nyqf, bias, n, batch, nt, out_blk0, prev=None):
    c_tab, s_tab = cs
    bw = batch * HY_CH
    tk = min(512, n)
    whole = lambda shape: pl.BlockSpec(shape, lambda *_: (0,) * len(shape), pipeline_mode=pl.Buffered(1))
    yr, yi = pl.pallas_call(
        functools.partial(_hy_spectrum_kernel, n=n, tk=tk, batch=batch),
        grid=(n // tk,),
        in_specs=[pl.BlockSpec((tk, n), lambda i: (i, 0)), pl.BlockSpec((tk, n), lambda i: (i, 0)),
                  whole((n, bw)), whole((n, HY_CH)), whole((n, HY_CH))],
        out_specs=[pl.BlockSpec((tk, bw), lambda i: (i, 0))] * 2,
        out_shape=[jax.ShapeDtypeStruct((n, bw), BF16)] * 2,
        compiler_params=_cparams(("arbitrary",)),
        name="hyena_spectrum",
    )(c_tab, s_tab, zb, e, d)

    tt = min(512, n)
    n_t = n // tt
    seg = lambda i, b: (b * n_t + i, 0)
    in_specs = [pl.BlockSpec((tt, n), lambda i, b: (i, 0)), pl.BlockSpec((tt, n), lambda i, b: (i, 0)),
                whole((n, bw)), whole((n, bw)),
                pl.BlockSpec((tt, HY_CH), seg), pl.BlockSpec((tt, HY_CH), seg),
                pl.BlockSpec((None, 1, HY_CH), lambda i, b: (b, 0, 0)),
                pl.BlockSpec((1, HY_CH), lambda i, b: (0, 0)), pl.BlockSpec((1, HY_CH), lambda i, b: (0, 0))]
    args = [c_tab, s_tab, yr, yi, x0c, zf, nyqz, nyqf, bias.reshape(1, HY_CH)]
    aliases = {}
    if prev is not None:
        in_specs.append(pl.BlockSpec(memory_space=pl.ANY))
        args.append(prev)
        aliases = {len(args) - 1: 0}
    return pl.pallas_call(
        functools.partial(_hy_inverse_kernel, n=n, tt=tt, batch=batch),
        grid=(n_t, batch),
        in_specs=in_specs,
        out_specs=pl.BlockSpec((tt, HY_CH), lambda i, b: (out_blk0 + b * n_t + i, 0)),
        out_shape=jax.ShapeDtypeStruct((nt, HY_CH), F32),
        scratch_shapes=[pltpu.VMEM((batch, tt, HY_CH), F32)],
        input_output_aliases=aliases,
        compiler_params=_cparams(("arbitrary", "arbitrary")),
        name="hyena_inverse" if prev is None else "hyena_inverse_ctx",
    )(*args)


def _scan_maps(geo, cc):
    batch, seq, ctx = geo["batch"], geo["seq"], geo["ctx"]
    n_cc, n_lc = ctx // cc, seq // cc
    ctx0 = (batch * seq) // cc

    def fwd(b, i):
        return (jnp.where(i < n_cc, ctx0 + b * n_cc + i, b * n_lc + i - n_cc), 0)

    def bwd(b, i):
        return (jnp.where(i < n_cc, ctx0 + b * n_cc + (n_cc - 1 - i), b * n_lc + (n_lc - 1 - (i - n_cc))), 0)

    return fwd, bwd, n_cc + n_lc


def _ret_kernel(qf_ref, kf_ref, vf_ref, qb_ref, kb_ref, vb_ref, dec_ref, of_ref, ob_ref, st_ref):
    @pl.when(pl.program_id(1) == 0)
    def _():
        st_ref[...] = jnp.zeros_like(st_ref)

    cc = qf_ref.shape[0]
    r_i, c_i = _iota((cc, cc), 0), _iota((cc, cc), 1)
    pos = _iota((cc, 1), 0).astype(F32)
    for d, (q_ref, k_ref, v_ref, o_ref) in enumerate(((qf_ref, kf_ref, vf_ref, of_ref),
                                                      (qb_ref, kb_ref, vb_ref, ob_ref))):
        reverse = d == 1
        lg = -jnp.exp(dec_ref[d])
        steps_in = (cc - pos) if reverse else (pos + 1.0)
        steps_out = pos if reverse else (cc - 1.0 - pos)
        q = q_ref[...]
        k = k_ref[...] * (RET_DH ** -0.5)
        qd = (q * jnp.exp(steps_in * lg)).astype(BF16)
        kd = (k * jnp.exp(steps_out * lg)).astype(BF16)
        qb, kb, vb = q.astype(BF16), k.astype(BF16), v_ref[...].astype(BF16)
        gain = jnp.exp(cc * lg)
        dist = (c_i - r_i) if reverse else (r_i - c_i)
        keep = dist >= 0
        dist_f = jnp.where(keep, dist, 0).astype(F32)
        for h in range(N_HEADS):
            hs = slice(h * RET_DH, (h + 1) * RET_DH)
            decay = jnp.where(keep, jnp.exp(dist_f * lg[:, h * RET_DH:h * RET_DH + 1]), 0.0)
            sc = (_dot_nt(qb[:, hs], kb[:, hs]) * decay).astype(BF16)
            st = st_ref[d, h]
            o_ref[:, hs] = _dot(sc, vb[:, hs]) + _dot_nt(qd[:, hs], st.astype(BF16))
            st_ref[d, h] = st * gain[:, hs] + _dot_tn(vb[:, hs], kd[:, hs])


def _retention(q, k, v, dec, geo):
    nt, w = q.shape
    cc = RET_CHUNK
    fwd, bwd, steps = _scan_maps(geo, cc)
    blk = lambda m: pl.BlockSpec((cc, w), m)
    return pl.pallas_call(
        _ret_kernel,
        grid=(geo["batch"], steps),
        in_specs=[blk(fwd), blk(fwd), blk(fwd), blk(bwd), blk(bwd), blk(bwd),
                  pl.BlockSpec((2, 1, w), lambda b, i: (0, 0, 0))],
        out_specs=[blk(fwd), blk(bwd)],
        out_shape=[jax.ShapeDtypeStruct((nt, w), F32)] * 2,
        scratch_shapes=[pltpu.VMEM((2, N_HEADS, RET_DH, RET_DH), F32)],
        compiler_params=_cparams(("arbitrary", "arbitrary")),
        name="recurrence_ret",
    )(q, k, v, q, k, v, dec)


def _gla_direction(q, k, v, la, st_ref, o_ref, *, reverse):
    cc, wk = q.shape
    wv = v.shape[1]
    r_i, c_i = _iota((cc, cc), 0), _iota((cc, cc), 1)
    incl = (c_i >= r_i) if reverse else (c_i <= r_i)
    cum = _dot_exact(jnp.where(incl, 1.0, 0.0), la)
    cum_end = cum[0:1] if reverse else cum[cc - 1:cc]

    qd = (q * jnp.exp(cum)).astype(BF16)
    kd = (k * jnp.exp(cum_end - cum)).astype(BF16)
    vb = v.astype(BF16)
    st = st_ref[...]
    inter = _dot_nt(qd, st.astype(BF16))
    same_head = (_iota((wv, wk), 0) // GLA_DV) == (_iota((wv, wk), 1) // GLA_DK)
    st_ref[...] = st * jnp.exp(cum_end) + jnp.where(same_head, _dot_tn(vb, kd), 0.0)

    sub = REC_SUB
    hs = N_HEADS * sub
    q_own = (_iota((hs, wk), 0) // sub) == (_iota((hs, wk), 1) // GLA_DK)
    for j in range(cc // sub):
        r0, r1 = j * sub, (j + 1) * sub
        if reverse:
            ka, kb = r0, cc
            base = cum[r1:r1 + 1] if r1 < cc else jnp.zeros((1, wk), F32)
        else:
            ka, kb = 0, r1
            base = cum[r0 - 1:r0] if r0 > 0 else jnp.zeros((1, wk), F32)
        qj = q[r0:r1] * jnp.exp(cum[r0:r1] - base)
        kj = (k[ka:kb] * jnp.exp(jnp.minimum(base - cum[ka:kb], EXP_CLAMP))).astype(BF16)
        q_stack = jnp.where(q_own, jnp.concatenate([qj] * N_HEADS, axis=0), 0.0).astype(BF16)
        rows = r0 + (_iota((hs, kb - ka), 0) % sub)
        cols = ka + _iota((hs, kb - ka), 1)
        keep = (cols >= rows) if reverse else (cols <= rows)
        sc = jnp.where(keep, _dot_nt(q_stack, kj), 0.0).astype(BF16)
        full = _dot(sc, vb[ka:kb])
        lane_head = _iota((sub, wv), 1) // GLA_DV
        oj = inter[r0:r1]
        for h in range(N_HEADS):
            oj = oj + jnp.where(lane_head == h, full[h * sub:(h + 1) * sub], 0.0)
        o_ref[r0:r1, :] = oj


def _gla_kernel(*refs, batch):
    per_batch = refs[:8 * batch]
    gw_ref, gb_ref, of_ref, ob_ref, st_ref = refs[8 * batch:]

    @pl.when(pl.program_id(0) == 0)
    def _():
        st_ref[...] = jnp.zeros_like(st_ref)

    for b in range(batch):
        qf_ref, kf_ref, vf_ref, lf_ref, qb_ref, kb_ref, vb_ref, lb_ref = per_batch[8 * b:8 * b + 8]
        for d, (q_ref, k_ref, v_ref, l_ref, o_ref) in enumerate(((qf_ref, kf_ref, vf_ref, lf_ref, of_ref),
                                                                 (qb_ref, kb_ref, vb_ref, lb_ref, ob_ref))):
            logit = _dot_exact(l_ref[...], gw_ref[d]) + gb_ref[d]
            la = (jnp.minimum(logit, 0.0) - jnp.log(1.0 + jnp.exp(-jnp.abs(logit)))) * (1.0 / GLA_GATE_NORM)
            _gla_direction(q_ref[...] * (GLA_DK ** -0.5), k_ref[...], v_ref[...], la, st_ref.at[b, d],
                           o_ref.at[b], reverse=(d == 1))


def _gla(q, k, v, glr, gw, gb, geo):
    batch, seq, ctx = geo["batch"], geo["seq"], geo["ctx"]
    wq, wv = q.shape[1], v.shape[1]
    cc = REC_CHUNK
    n_cc, n_lc = ctx // cc, seq // cc
    ctx0 = (batch * seq) // cc
    in_specs, args = [], []
    for b in range(batch):
        fwd = lambda i, b=b: (jnp.where(i < n_cc, ctx0 + b * n_cc + i, b * n_lc + i - n_cc), 0)
        bwd = lambda i, b=b: (jnp.where(i < n_cc, ctx0 + b * n_cc + (n_cc - 1 - i),
                                        b * n_lc + (n_lc - 1 - (i - n_cc))), 0)
        for m in (fwd, bwd):
            in_specs += [pl.BlockSpec((cc, wq), m), pl.BlockSpec((cc, wq), m), pl.BlockSpec((cc, wv), m),
                         pl.BlockSpec((cc, LANES), m)]
            args += [q, k, v, glr]
    in_specs += [pl.BlockSpec((2, LANES, wq), lambda i: (0, 0, 0)), pl.BlockSpec((2, 1, wq), lambda i: (0, 0, 0))]
    args += [gw, gb]
    steps = n_cc + n_lc
    out_blk = lambda m: pl.BlockSpec((batch, cc, wv), m)
    return pl.pallas_call(
        functools.partial(_gla_kernel, batch=batch),
        grid=(steps,),
        in_specs=in_specs,
        out_specs=[out_blk(lambda i: (0, i, 0)),
                   out_blk(lambda i: (0, jnp.where(i < n_cc, n_cc - 1 - i, steps - 1 - (i - n_cc)), 0))],
        out_shape=[jax.ShapeDtypeStruct((batch, ctx + seq, wv), F32)] * 2,
        scratch_shapes=[pltpu.VMEM((batch, 2, wv, wq), F32)],
        compiler_params=_cparams(("arbitrary",)),
        name="recurrence_gla",
    )(*args)


def _mixout_kernel(x_ref, mod_ref, at_ref, hy_ref, rf_ref, rb_ref, cg_ref, gf_ref, gb_ref, dr_ref,
                   seg_ref, rg_ref, gg_ref, wo_ref, g1_ref, g2_ref, rwh_ref, rwl_ref, rbias_ref,
                   x1_ref, h2_ref, ti_ref, tw_ref, rk_ref, cnt_ref):
    @pl.when(pl.program_id(0) == 0)
    def _():
        cnt_ref[...] = jnp.zeros_like(cnt_ref)

    def head_norm(o):
        ms = _dot((o * o).astype(BF16), seg_ref[...]) * (1.0 / RET_DH)
        return o * lax.rsqrt(ms + EPS)

    a = jnp.concatenate([at_ref[h] for h in range(N_HEADS)], axis=-1)
    cg = cg_ref[...]
    dr = dr_ref[...]
    rt = head_norm(rf_ref[...] + rb_ref[...]) * rg_ref[...] * (cg * _sigmoid(cg))
    gl = head_norm(gf_ref[...] + gb_ref[...]) * gg_ref[...] * (dr * _sigmoid(dr))
    cat = jnp.concatenate([a, hy_ref[...], rt, gl], axis=-1).astype(BF16)
    y = _dot(cat, wo_ref[...])
    x1 = x_ref[...] + mod_ref[2:3, :] * (_rms(y) * g1_ref[...])
    x1_ref[...] = x1
    h2 = _rms(x1) * g2_ref[...] * (1.0 + mod_ref[4:5, :]) + mod_ref[3:4, :]
    h2_ref[...] = h2

    h_hi = h2.astype(BF16)
    h_lo = (h2 - h_hi.astype(F32)).astype(BF16)
    vals = (_dot(h_hi, rwh_ref[...]) + _dot(h_lo, rwh_ref[...]) + _dot(h_hi, rwl_ref[...])
            + rbias_ref[...])
    tm = vals.shape[0]
    lane = lax.broadcasted_iota(jnp.int32, vals.shape, 1)
    idx_out = jnp.zeros(vals.shape, jnp.int32)
    w_out = jnp.zeros(vals.shape, F32)
    top = None
    den = 0.0
    picks = []
    for r in range(TOP_K):
        m = jnp.max(vals, axis=-1, keepdims=True)
        idx = jnp.min(jnp.where(vals == m, lane, LANES), axis=-1, keepdims=True)
        hit = lane == idx
        vals = jnp.where(hit, -jnp.inf, vals)
        top = m if top is None else top
        e = jnp.exp(m - top)
        den = den + e
        picks.append((idx, e, hit))
    inv = 1.0 / den
    for r, (idx, e, _) in enumerate(picks):
        idx_out = jnp.where(lane == r, idx, idx_out)
        w_out = jnp.where(lane == r, e * inv, w_out)
    ti_ref[...] = idx_out
    tw_ref[...] = w_out

    chosen = functools.reduce(jnp.logical_or, [hit for _, _, hit in picks])
    chosen_f = jnp.where(chosen, 1.0, 0.0)
    earlier = (lax.broadcasted_iota(jnp.int32, (tm, tm), 1) < lax.broadcasted_iota(jnp.int32, (tm, tm), 0))
    before = _dot(jnp.where(earlier, 1.0, 0.0).astype(BF16), chosen_f.astype(BF16)) + cnt_ref[...]
    rk_out = jnp.zeros(vals.shape, jnp.int32)
    for r, (_, _, hit) in enumerate(picks):
        rank = jnp.sum(jnp.where(hit, before, 0.0), axis=-1, keepdims=True)
        rk_out = jnp.where(lane == r, rank.astype(jnp.int32), rk_out)
    rk_ref[...] = rk_out
    cnt_ref[...] = cnt_ref[...] + jnp.sum(chosen_f, axis=0, keepdims=True)


def _mix_out(n_tiles, x, mod, at, hy, rf, rb, cg, gf, gb, dr, seg, rg, gg, wo, g1, g2, rw, rbias, geo):
    rw_hi = rw.astype(BF16)
    rw_lo = (rw - rw_hi.astype(F32)).astype(BF16)
    nt, d = x.shape
    tm = ROW_TILE
    lat_tiles, per_batch, batch = geo["lat_tiles"], geo["tiles_per_batch"], geo["batch"]

    def mod_map(i):
        return (jnp.where(i < lat_tiles, i // per_batch, batch), 0, 0)

    row = lambda i: (i, 0)
    const = lambda i: (0, 0)
    g_blk = pl.BlockSpec((tm, GROUP_W), row)
    ctx_tiles = geo["ctx"] // tm

    def scan_map(i):
        c = i - lat_tiles
        return (jnp.where(i < lat_tiles, i // per_batch, c // ctx_tiles),
                jnp.where(i < lat_tiles, ctx_tiles + i % per_batch, c % ctx_tiles), 0)

    s_blk = pl.BlockSpec((None, tm, GROUP_W), scan_map)
    return pl.pallas_call(
        _mixout_kernel,
        grid=(n_tiles,),
        in_specs=[pl.BlockSpec((tm, d), row), pl.BlockSpec((None, 6, d), mod_map),
                  pl.BlockSpec((N_HEADS, tm, DIFF_DV), lambda i: (0, i, 0)),
                  g_blk, g_blk, g_blk, g_blk, s_blk, s_blk, g_blk,
                  pl.BlockSpec((GROUP_W, GROUP_W), const), pl.BlockSpec((1, GROUP_W), const),
                  pl.BlockSpec((1, GROUP_W), const),
                  pl.BlockSpec((d, d), const, pipeline_mode=pl.Buffered(1)),
                  pl.BlockSpec((1, d), const), pl.BlockSpec((1, d), const),
                  pl.BlockSpec((d, LANES), const), pl.BlockSpec((d, LANES), const), pl.BlockSpec((1, LANES), const)],
        out_specs=[pl.BlockSpec((tm, d), row), pl.BlockSpec((tm, d), row),
                   pl.BlockSpec((tm, LANES), row), pl.BlockSpec((tm, LANES), row),
                   pl.BlockSpec((tm, LANES), row), pl.BlockSpec((1, LANES), const)],
        out_shape=[jax.ShapeDtypeStruct((nt, d), F32), jax.ShapeDtypeStruct((nt, d), F32),
                   jax.ShapeDtypeStruct((nt, LANES), jnp.int32), jax.ShapeDtypeStruct((nt, LANES), F32),
                   jax.ShapeDtypeStruct((nt, LANES), jnp.int32), jax.ShapeDtypeStruct((1, LANES), F32)],
        compiler_params=_cparams(("arbitrary",)),
        name="mix_out_router",
    )(x, mod, at, hy, rf, rb, cg, gf, gb, dr, seg.astype(BF16), rg, gg, wo, g1, g2, rw_hi, rw_lo, rbias)


def _expert_kernel(te_ref, na_ref, xs_ref, w1_ref, b1_ref, w2_ref, b2_ref, perm_ref, ys_ref, w1s, w2s):
    i = pl.program_id(0)
    active = i < na_ref[0]
    fresh = jnp.logical_or(i == 0, te_ref[i] != te_ref[jnp.maximum(i - 1, 0)])
    n_groups = w1_ref.shape[1] // (2 * LANES)

    @pl.when(jnp.logical_and(active, fresh))
    def _():
        for c in range(n_groups):
            cols = slice(c * 2 * LANES, (c + 1) * 2 * LANES)
            w1s[:, cols] = _dot(w1_ref[:, cols].astype(BF16), perm_ref[...]).astype(BF16)
        w2s[...] = w2_ref[...].astype(BF16)

    @pl.when(active)
    def _():
        u = _dot(xs_ref[...].astype(BF16), w1s[...]) + b1_ref[...]
        acts = []
        for c in range(n_groups):
            glu = jnp.minimum(u[:, c * 2 * LANES:c * 2 * LANES + LANES], SWIGLU_LIMIT)
            lin = jnp.clip(u[:, c * 2 * LANES + LANES:(c + 1) * 2 * LANES], -SWIGLU_LIMIT, SWIGLU_LIMIT)
            acts.append((glu * _sigmoid(SWIGLU_ALPHA * glu) * (lin + 1.0)).astype(BF16))
        ys_ref[...] = _dot(jnp.concatenate(acts, axis=-1), w2s[...]) + b2_ref[...]

    @pl.when(jnp.logical_not(active))
    def _():
        ys_ref[...] = jnp.zeros_like(ys_ref)


def _expert_ffn(tile_e, n_active, xs, w1, b1, w2, b2, layer):
    p, d = xs.shape
    tm = MOE_TILE
    de2 = w1.shape[3]
    de = de2 // 2
    sel = np.zeros((2 * LANES, 2 * LANES), np.float32)
    sel[2 * np.arange(LANES), np.arange(LANES)] = 1.0
    sel[2 * np.arange(LANES) + 1, LANES + np.arange(LANES)] = 1.0
    b1g = b1.reshape(N_EXPERTS, de2 // (2 * LANES), LANES, 2).transpose(0, 1, 3, 2).reshape(N_EXPERTS, 1, de2)
    wmap = lambda i, te, na: (te[i], 0, 0)
    lmap = lambda i, te, na: (layer, te[i], 0, 0)
    xmap = lambda i, te, na: (jnp.minimum(i, jnp.maximum(na[0] - 1, 0)), 0)
    grid_spec = pltpu.PrefetchScalarGridSpec(
        num_scalar_prefetch=2,
        grid=(p // tm,),
        in_specs=[pl.BlockSpec((tm, d), xmap),
                  pl.BlockSpec((None, None, d, de2), lmap), pl.BlockSpec((None, 1, de2), wmap),
                  pl.BlockSpec((None, None, de, d), lmap), pl.BlockSpec((None, 1, d), wmap),
                  pl.BlockSpec((2 * LANES, 2 * LANES), lambda i, te, na: (0, 0))],
        out_specs=pl.BlockSpec((tm, d), lambda i, te, na: (i, 0)),
        scratch_shapes=[pltpu.VMEM((d, de2), BF16), pltpu.VMEM((de, d), BF16)],
    )
    return pl.pallas_call(
        _expert_kernel,
        grid_spec=grid_spec,
        out_shape=jax.ShapeDtypeStruct((p, d), F32),
        compiler_params=_cparams(("arbitrary",)),
        name="expert_ffn",
    )(tile_e, n_active, xs, w1, b1g, w2, b2.reshape(N_EXPERTS, 1, d), jnp.asarray(sel, BF16))


def _route_plan(topi, rank, counts, tm):
    n, k = topi.shape
    padded = ((counts + tm - 1) // tm) * tm
    pend = jnp.cumsum(padded)
    pstart = pend - padded
    experts = jnp.arange(N_EXPERTS, dtype=jnp.int32)
    start = jnp.sum(jnp.where(topi[:, :, None] == experts, pstart, 0), axis=-1)
    dest = (start + rank).astype(jnp.int32)
    p = n * k + N_EXPERTS * tm
    tile_start = jnp.arange(p // tm, dtype=jnp.int32) * tm
    tile_e = jnp.minimum(jnp.sum((tile_start[:, None] >= pend[None, :]).astype(jnp.int32), axis=1), N_EXPERTS - 1)
    n_active = (pend[-1:] // tm).astype(jnp.int32)
    return dest, tile_e, n_active, p, jnp.stack([pstart, pend]).astype(jnp.int32)


def _dispatch_kernel(bounds_ref, dest_ref, x_ref, out_ref, zeros, sem, zsem, *, tb):
    n_rows = tb * TOP_K

    @pl.when(pl.program_id(0) == 0)
    def _():
        zeros[...] = jnp.zeros_like(zeros)

        def fill(e):
            start = pl.multiple_of(bounds_ref[1, e] - MOE_TILE, MOE_TILE)
            return pltpu.make_async_copy(zeros, out_ref.at[pl.ds(start, MOE_TILE)], zsem)

        for e in range(N_EXPERTS):
            @pl.when(bounds_ref[1, e] > bounds_ref[0, e])
            def _():
                fill(e).start()
        for e in range(N_EXPERTS):
            @pl.when(bounds_ref[1, e] > bounds_ref[0, e])
            def _():
                fill(e).wait()

    def issue(t, carry):
        for k in range(TOP_K):
            pltpu.make_async_copy(x_ref.at[pl.ds(t, 1)], out_ref.at[pl.ds(dest_ref[0, t * TOP_K + k], 1)],
                                  sem).start()
        return carry

    lax.fori_loop(0, tb, issue, 0, unroll=4)
    pltpu.make_async_copy(out_ref.at[pl.ds(0, n_rows)], out_ref.at[pl.ds(0, n_rows)], sem).wait()


def _dispatch(h2, dest, bounds, p_rows, n_tok):
    d = h2.shape[1]
    tb = ROW_TILE
    steps = n_tok // tb
    return pl.pallas_call(
        functools.partial(_dispatch_kernel, tb=tb),
        grid=(steps,),
        in_specs=[pl.BlockSpec(memory_space=pltpu.SMEM),
                  pl.BlockSpec((None, 1, tb * TOP_K), lambda i: (i, 0, 0), memory_space=pltpu.SMEM),
                  pl.BlockSpec((tb, d), lambda i: (i, 0))],
        out_specs=pl.BlockSpec(memory_space=pl.ANY),
        out_shape=jax.ShapeDtypeStruct((p_rows, d), F32),
        scratch_shapes=[pltpu.VMEM((MOE_TILE, d), F32), pltpu.SemaphoreType.DMA(()), pltpu.SemaphoreType.DMA(())],
        compiler_params=pltpu.CompilerParams(dimension_semantics=("arbitrary",), has_side_effects=True),
        name="moe_dispatch",
    )(bounds, dest.reshape(steps, 1, tb * TOP_K), h2)


def _combine_kernel(dest_ref, x_ref, tw_ref, mod_ref, g_ref, ys_ref, o_ref, buf, sem, *, tb):
    n_rows = tb * TOP_K

    def issue(r, carry):
        pltpu.make_async_copy(ys_ref.at[pl.ds(dest_ref[0, r], 1)], buf.at[pl.ds(r, 1)], sem).start()
        return carry

    lax.fori_loop(0, n_rows, issue, 0, unroll=8)
    pltpu.make_async_copy(ys_ref.at[pl.ds(0, n_rows)], buf, sem).wait()
    y = None
    for k in range(TOP_K):
        t = buf[k * tb:(k + 1) * tb, :] * tw_ref[:, k:k + 1]
        y = t if y is None else y + t
    o_ref[...] = x_ref[...] + mod_ref[5:6, :] * (_rms(y) * g_ref[...])


def _combine_residual(n_tiles, x, ys, dest, topw, mod, g, geo):
    nt, d = x.shape
    tb = ROW_TILE
    lat_tiles, per_batch, batch = geo["lat_tiles"], geo["tiles_per_batch"], geo["batch"]

    def mod_map(i):
        return (jnp.where(i < lat_tiles, i // per_batch, batch), 0, 0)

    row = lambda i: (i, 0)
    slot_major = jnp.swapaxes(dest.reshape(n_tiles, tb, TOP_K), 1, 2).reshape(n_tiles, 1, tb * TOP_K)
    return pl.pallas_call(
        functools.partial(_combine_kernel, tb=tb),
        grid=(n_tiles,),
        in_specs=[pl.BlockSpec((None, 1, tb * TOP_K), lambda i: (i, 0, 0), memory_space=pltpu.SMEM),
                  pl.BlockSpec((tb, d), row), pl.BlockSpec((tb, LANES), row),
                  pl.BlockSpec((None, 6, d), mod_map), pl.BlockSpec((1, d), lambda i: (0, 0)),
                  pl.BlockSpec(memory_space=pl.ANY)],
        out_specs=pl.BlockSpec((tb, d), row),
        out_shape=jax.ShapeDtypeStruct((nt, d), F32),
        scratch_shapes=[pltpu.VMEM((tb * TOP_K, d), F32), pltpu.SemaphoreType.DMA(())],
        compiler_params=_cparams(("arbitrary",)),
        name="combine_residual",
    )(slot_major, x, topw, mod, g, ys)


def _rope_partner(dim):
    h, q = dim // 2, dim // 4
    perm = np.zeros(dim, np.int32)
    sign = np.zeros(dim, np.float32)
    for base in (0, h):
        for j in range(q):
            perm[base + j], sign[base + j] = base + j + q, -1.0
            perm[base + q + j], sign[base + q + j] = base + j, 1.0
    return perm, sign


def _rot_cols(w, dim):
    perm, sign = _rope_partner(dim)
    reps = w.shape[1] // dim
    full_perm = np.concatenate([perm + r * dim for r in range(reps)])
    return w[:, full_perm] * jnp.asarray(np.tile(sign, reps))


def _rope_tables(n_tok, dim, reps, pad_rows):
    rows = n_tok // GRID_W
    row = jnp.repeat(jnp.arange(rows, dtype=F32), GRID_W)
    col = jnp.tile(jnp.arange(GRID_W, dtype=F32), rows)
    quarter = dim // 4
    inv = ROPE_BASE ** (-jnp.arange(quarter, dtype=F32) / quarter)
    ar = row[:, None] * inv[None]
    ac = col[:, None] * inv[None]
    ang = jnp.concatenate([ar, ar, ac, ac], axis=-1)
    cos = jnp.concatenate([jnp.tile(jnp.cos(ang), (1, reps)), jnp.ones((pad_rows, dim * reps), F32)])
    sin = jnp.concatenate([jnp.tile(jnp.sin(ang), (1, reps)), jnp.zeros((pad_rows, dim * reps), F32)])
    return cos, sin


def _widen_w_in(w):
    sizes = (256, 256, 256, 768, 256, 256, 256, 256, 128, 128, 256, 256, 32)
    cuts = np.cumsum(sizes)[:-1].tolist()
    aq, ak, av, bu, cq, ck, cv, cg, dq, dk, dv, dr, dl = jnp.split(w, cuts, axis=1)
    dl = jnp.pad(dl, ((0, 0), (0, LANES - dl.shape[1])))
    parts = [aq, _rot_cols(aq, DIFF_DQK), ak, _rot_cols(ak, DIFF_DQK), av, bu,
             cq, _rot_cols(cq, RET_DH), ck, _rot_cols(ck, RET_DH), cv, cg, dq, dk, dv, dr, dl]
    return jnp.concatenate(parts, axis=1).astype(BF16)


def kernel(x, c, ctx, c_ctx, w_mod, b_mod, norm_g, w_in, w_out, diff_lambda, diff_subln_g, hy_short_w, hy_short_b, hy_w1, hy_b1, hy_w2, hy_b2, hy_w3, hy_freq, hy_bias, ret_decay, ret_norm_g, gla_gate_w, gla_gate_b, gla_norm_g, router_w, router_b, exp_w1, exp_b1, exp_w2, exp_b2):
    batch, seq, d = x.shape
    n_ctx = ctx.shape[1]
    depth = w_mod.shape[0]
    n_lat_rows, n_ctx_rows = batch * seq, batch * n_ctx
    nt = n_lat_rows + n_ctx_rows
    assert d == D_MODEL and seq % ROW_TILE == 0 and n_ctx % ROW_TILE == 0 and seq % n_ctx == 0
    geo = dict(batch=batch, seq=seq, ctx=n_ctx, lat_tiles=n_lat_rows // ROW_TILE,
               tiles_per_batch=seq // ROW_TILE)

    xs = jnp.concatenate([x.reshape(n_lat_rows, d), ctx.reshape(n_ctx_rows, d)], axis=0)
    mod_rows = 8
    cc = jnp.zeros((mod_rows, d), F32).at[:batch].set(c).at[batch].set(c_ctx)
    mod_all = _modulation(cc, w_mod, b_mod).reshape(depth, mod_rows, 6, d)

    rope = (*_rope_tables(seq, DIFF_DQK, GROUP_W // DIFF_DQK, ROW_TILE),
            *_rope_tables(seq, RET_DH, GROUP_W // RET_DH, ROW_TILE))
    dft_lat = _dft_tables(seq)
    dft_ctx = _dft_tables(n_ctx)
    seg = jnp.asarray(np.kron(np.eye(N_HEADS, dtype=np.float32), np.ones((RET_DH, RET_DH), np.float32)))

    for l in range(depth):
        need_ctx = l < depth - 1
        lam_init = 0.8 - 0.6 * math.exp(-0.3 * l)
        mod = mod_all[l]
        (aq, ak, av, bu, cq, ck, cv, cg, dq, dk, dv, dr, dl) = _in_projection(
            xs, mod, norm_g[l, 0].reshape(1, d), _widen_w_in(w_in[l]), rope, geo)

        sub_g = diff_subln_g[l].reshape(1, DIFF_DV)
        at = _diff_attention(diff_lambda[l], sub_g, aq, ak, av, geo, lam_init)
        if need_ctx:
            at = _diff_attention(diff_lambda[l], sub_g, aq, ak, av, geo, lam_init, prev=at)

        def hyena(n, row_blk0, tables, out_blk0, prev):
            e, dd, nyqf = _hyena_filter(n, hy_w1[l], hy_b1[l], hy_w2[l], hy_b2[l], hy_w3[l], hy_freq[l])
            zb, x0c, zf, nyqz = _hyena_gate(bu, hy_short_w[l], hy_short_b[l], n, batch, row_blk0)
            return _hyena_conv(tables, zb, e, dd, x0c, zf, nyqz, nyqf, hy_bias[l], n, batch, nt, out_blk0, prev)

        hy = hyena(seq, 0, dft_lat, 0, None)
        if need_ctx:
            hy = hyena(n_ctx, n_lat_rows // n_ctx, dft_ctx, n_lat_rows // min(512, n_ctx), hy)

        dec = jnp.repeat(ret_decay[l], RET_DH, axis=-1).reshape(2, 1, N_HEADS * RET_DH)
        rf, rb = _retention(cq, ck, cv, dec, geo)

        gw = jnp.zeros((2, LANES, N_HEADS * GLA_DK), F32)
        gw = gw.at[0, :GLA_RANK].set(gla_gate_w[l, 0]).at[1, GLA_RANK:2 * GLA_RANK].set(gla_gate_w[l, 1])
        gbias = gla_gate_b[l].reshape(2, 1, N_HEADS * GLA_DK)
        gf, gb = _gla(dq, dk, dv, dl, gw, gbias, geo)

        n_tiles = nt // ROW_TILE if need_ctx else n_lat_rows // ROW_TILE
        rw = jnp.pad(router_w[l], ((0, 0), (0, LANES - N_EXPERTS)))
        rbias = jnp.pad(router_b[l], (0, LANES - N_EXPERTS), constant_values=-jnp.inf).reshape(1, LANES)
        x1, h2, topi, topw, rank, cnt = _mix_out(
            n_tiles, xs, mod, at, hy, rf, rb, cg, gf, gb, dr, seg,
            ret_norm_g[l].reshape(1, GROUP_W), jnp.tile(gla_norm_g[l], N_HEADS).reshape(1, GROUP_W),
            w_out[l].astype(BF16), norm_g[l, 1].reshape(1, d), norm_g[l, 2].reshape(1, d), rw, rbias, geo)

        n_tok = n_tiles * ROW_TILE
        dest, tile_e, n_active, p_rows, bounds = _route_plan(
            topi[:n_tok, :TOP_K], rank[:n_tok, :TOP_K], cnt[0, :N_EXPERTS].astype(jnp.int32), MOE_TILE)
        dispatched = _dispatch(h2, dest, bounds, p_rows, n_tok)
        ys = _expert_ffn(tile_e, n_active, dispatched, exp_w1, exp_b1[l], exp_w2, exp_b2[l], l)
        xs = _combine_residual(n_tiles, x1, ys, dest, topw, mod, norm_g[l, 3].reshape(1, d), geo)

    return xs[:n_lat_rows].reshape(batch, seq, d)
```

```python
import functools
import math

import numpy as np
import jax
import jax.numpy as jnp
from jax import lax
from jax.experimental import pallas as pl
from jax.experimental.pallas import tpu as pltpu

F32 = jnp.float32
BF16 = jnp.bfloat16
HIGHEST = lax.Precision.HIGHEST

D_MODEL = 1024
GRID_W = 64
GROUP_W = 256
N_HEADS = 4
DIFF_DQK = 32
DIFF_DV = 64
ROPE_BASE = 10000.0
HY_CH = 256
HY_BANDS = 16
HY_FFN = 64
HY_FAST_DECAY_PCT = 0.3
HY_SLOW_DECAY_PCT = 1.5
HY_DECAY_TARGET = 1e-2
RET_DH = 64
GLA_DK = 32
GLA_DV = 64
GLA_RANK = 16
GLA_GATE_NORM = 16.0
N_EXPERTS = 32
TOP_K = 4
D_EXPERT = 1024
SWIGLU_LIMIT = 7.0
SWIGLU_ALPHA = 1.702
EPS = 1e-6

LANES = 128
ROW_TILE = 256
REC_CHUNK = 128
RET_CHUNK = 256
REC_SUB = 32
EXP_CLAMP = 80.0
MOE_TILE = 256
ATT_KEY_CHUNK = 512
DFT_TABLE_ROWS = 128
VMEM_LIMIT = 52 * 1024 * 1024

_A_Q, _A_QR, _A_K, _A_KR, _A_V = 0, 256, 512, 768, 1024
_B_U = 1280
_C_Q, _C_QR, _C_K, _C_KR, _C_V, _C_G = 2048, 2304, 2560, 2816, 3072, 3328
_D_Q, _D_K, _D_V, _D_R, _D_L = 3584, 3712, 3840, 4096, 4352
_IN_COLS = 4480


def _cparams(sem):
    return pltpu.CompilerParams(dimension_semantics=sem, vmem_limit_bytes=VMEM_LIMIT)


def _sigmoid(x):
    return 1.0 / (1.0 + jnp.exp(-x))


def _rms(x):
    return x * lax.rsqrt(jnp.mean(x * x, axis=-1, keepdims=True) + EPS)


def _iota(shape, axis):
    return lax.broadcasted_iota(jnp.int32, shape, axis)


def _dot(a, b):
    return jnp.dot(a, b, preferred_element_type=F32)


def _dot_exact(a, b):
    return jnp.dot(a, b, preferred_element_type=F32, precision=HIGHEST)


def _dot_nt(a, b):
    return lax.dot_general(a, b, (((1,), (1,)), ((), ())), preferred_element_type=F32)


def _dot_tn(a, b):
    return lax.dot_general(a, b, (((0,), (0,)), ((), ())), preferred_element_type=F32)


def _mod_kernel(c_ref, w_ref, b_ref, o_ref):
    c = c_ref[...]
    o_ref[...] = _dot_exact(c * _sigmoid(c), w_ref[...]) + b_ref[...]


def _modulation(cc, w_mod, b_mod):
    depth, d, n = w_mod.shape
    tn = n // 4
    rows = cc.shape[0]
    return pl.pallas_call(
        _mod_kernel,
        grid=(depth, n // tn),
        in_specs=[pl.BlockSpec((rows, d), lambda l, j: (0, 0)),
                  pl.BlockSpec((None, d, tn), lambda l, j: (l, 0, j)),
                  pl.BlockSpec((None, 1, tn), lambda l, j: (l, 0, j))],
        out_specs=pl.BlockSpec((None, rows, tn), lambda l, j: (l, 0, j)),
        out_shape=jax.ShapeDtypeStruct((depth, rows, n), F32),
        compiler_params=_cparams(("arbitrary", "arbitrary")),
        name="modulation",
    )(cc, w_mod, b_mod.reshape(depth, 1, n))


def _inproj_kernel(x_ref, mod_ref, g_ref, w_ref, cosa_ref, sina_ref, cosc_ref, sinc_ref,
                   aq_ref, ak_ref, av_ref, bu_ref, cq_ref, ck_ref, cv_ref, cg_ref,
                   dq_ref, dk_ref, dv_ref, dr_ref, dl_ref):
    xn = _rms(x_ref[...]) * g_ref[...]
    h = (xn * (1.0 + mod_ref[1:2, :]) + mod_ref[0:1, :]).astype(BF16)

    def proj(a, width):
        return _dot(h, w_ref[:, a:a + width])

    def roped(a, a_rot, cos_ref, sin_ref):
        return proj(a, GROUP_W) * cos_ref[...] + proj(a_rot, GROUP_W) * sin_ref[...]

    aq = roped(_A_Q, _A_QR, cosa_ref, sina_ref) * (DIFF_DQK ** -0.5 * math.log2(math.e))
    ak = roped(_A_K, _A_KR, cosa_ref, sina_ref)
    av = proj(_A_V, GROUP_W)
    ones_col = jnp.where(lax.broadcasted_iota(jnp.int32, (av.shape[0], LANES - DIFF_DV), 1) == 0, 1.0, 0.0)
    for hd in range(N_HEADS):
        sl = slice(hd * DIFF_DV, (hd + 1) * DIFF_DV)
        aq_ref[hd] = aq[:, sl].astype(BF16)
        ak_ref[hd] = ak[:, sl].astype(BF16)
        av_ref[hd] = jnp.concatenate([av[:, sl], ones_col], axis=-1).astype(BF16)
    bu_ref[...] = proj(_B_U, 3 * HY_CH)
    cq_ref[...] = roped(_C_Q, _C_QR, cosc_ref, sinc_ref)
    ck_ref[...] = roped(_C_K, _C_KR, cosc_ref, sinc_ref)
    cv_ref[...] = proj(_C_V, GROUP_W)
    cg_ref[...] = proj(_C_G, GROUP_W)
    dq_ref[...] = proj(_D_Q, LANES)
    dk_ref[...] = proj(_D_K, LANES)
    dv_ref[...] = proj(_D_V, GROUP_W)
    dr_ref[...] = proj(_D_R, GROUP_W)
    dl_ref[...] = proj(_D_L, LANES)


def _in_projection(x, mod, g, w_wide, rope, geo):
    nt, d = x.shape
    tm = ROW_TILE
    lat_tiles, per_batch, batch = geo["lat_tiles"], geo["tiles_per_batch"], geo["batch"]

    def mod_map(i):
        return (jnp.where(i < lat_tiles, i // per_batch, batch), 0, 0)

    def rope_map(i):
        return (jnp.where(i < lat_tiles, i % per_batch, per_batch), 0)

    row = lambda i: (i, 0)
    head = lambda i: (0, i, 0)
    const = lambda i: (0, 0)
    f32_out = lambda w: jax.ShapeDtypeStruct((nt, w), F32)
    head_out = lambda w: jax.ShapeDtypeStruct((N_HEADS, nt, w), BF16)
    head_spec = lambda w: pl.BlockSpec((N_HEADS, tm, w), head)
    widths = [3 * HY_CH, GROUP_W, GROUP_W, GROUP_W, GROUP_W, LANES, LANES, GROUP_W, GROUP_W, LANES]
    return pl.pallas_call(
        _inproj_kernel,
        grid=(nt // tm,),
        in_specs=[pl.BlockSpec((tm, d), row),
                  pl.BlockSpec((None, 6, d), mod_map),
                  pl.BlockSpec((1, d), const),
                  pl.BlockSpec((d, _IN_COLS), const, pipeline_mode=pl.Buffered(1)),
                  pl.BlockSpec((tm, GROUP_W), rope_map), pl.BlockSpec((tm, GROUP_W), rope_map),
                  pl.BlockSpec((tm, GROUP_W), rope_map), pl.BlockSpec((tm, GROUP_W), rope_map)],
        out_specs=[head_spec(DIFF_DV), head_spec(DIFF_DV), head_spec(LANES)]
                  + [pl.BlockSpec((tm, w), row) for w in widths],
        out_shape=[head_out(DIFF_DV), head_out(DIFF_DV), head_out(LANES)] + [f32_out(w) for w in widths],
        compiler_params=_cparams(("arbitrary",)),
        name="in_projection",
    )(x, mod, g, w_wide, *rope)


def _attn_kernel(lam_ref, g_ref, q_ref, *rest, lam_init, has_lat):
    if has_lat:
        kl_ref, vl_ref, kc_ref, vc_ref, o_ref = rest
        keys = [(kl_ref, vl_ref), (kc_ref, vc_ref)]
    else:
        kc_ref, vc_ref, o_ref = rest
        keys = [(kc_ref, vc_ref)]
    lp = lam_ref[...]
    lam = (jnp.exp(jnp.sum(lp[0:1] * lp[1:2], axis=-1, keepdims=True))
           - jnp.exp(jnp.sum(lp[2:3] * lp[3:4], axis=-1, keepdims=True)) + lam_init)
    q = q_ref[...]
    tq = q.shape[0]
    lane = lax.broadcasted_iota(jnp.int32, q.shape, 1)
    chunks = []
    for k_ref, v_ref in keys:
        size = min(ATT_KEY_CHUNK, k_ref.shape[0])
        chunks += [(k_ref, v_ref, s0, size) for s0 in range(0, k_ref.shape[0], size)]

    def lane_groups(t):
        return [t[:, c0:c0 + LANES] for c0 in range(0, t.shape[1], LANES)]

    qm = [jnp.where((lane >= m * DIFF_DQK) & (lane < (m + 1) * DIFF_DQK), q, jnp.zeros_like(q)) for m in range(2)]

    def score(m, j):
        k_ref, _, s0, size = chunks[j]
        return _dot_nt(qm[m], k_ref[s0:s0 + size, :])

    def row_max(scores):
        wide = functools.reduce(jnp.maximum, [g for t in scores for g in lane_groups(t)])
        return jnp.max(wide, axis=-1, keepdims=True)

    def weighted(m, j, s, mx, acc):
        _, v_ref, s0, size = chunks[j]
        return acc + _dot(jnp.exp2(s - mx).astype(BF16), v_ref[s0:s0 + size, :])

    n = len(chunks)
    s1 = [score(0, j) for j in range(n)]
    mx1 = row_max(s1)
    s2 = []
    acc1 = jnp.zeros((tq, LANES), F32)
    for j in range(n):
        s2.append(score(1, j))
        acc1 = weighted(0, j, s1[j], mx1, acc1)
    mx2 = row_max(s2)
    acc2 = jnp.zeros((tq, LANES), F32)
    for j in range(n):
        acc2 = weighted(1, j, s2[j], mx2, acc2)
    o = (acc1[:, :DIFF_DV] * (1.0 / acc1[:, DIFF_DV:DIFF_DV + 1])
         - lam * (acc2[:, :DIFF_DV] * (1.0 / acc2[:, DIFF_DV:DIFF_DV + 1])))
    o_ref[...] = _rms(o) * g_ref[...] * (1.0 - lam_init)


def _diff_attention(lam_p, subln_g, aq, ak, av, geo, lam_init, prev=None):
    batch, seq, ctx = geo["batch"], geo["seq"], geo["ctx"]
    nt = aq.shape[1]
    has_lat = prev is None
    tq = ROW_TILE
    n_q = (seq if has_lat else ctx) // tq
    q_off = 0 if has_lat else (batch * seq) // tq
    ctx_blk0 = (batch * seq) // ctx

    qmap = lambda b, h, i: (h, q_off + b * n_q + i, 0)
    lat_map = lambda b, h, i: (h, b, 0)
    ctx_map = lambda b, h, i: (h, ctx_blk0 + b, 0)
    const = lambda b, h, i: (0, 0)
    in_specs = [pl.BlockSpec((4, DIFF_DQK), const), pl.BlockSpec((1, DIFF_DV), const),
                pl.BlockSpec((None, tq, DIFF_DV), qmap)]
    args = [lam_p, subln_g, aq]
    if has_lat:
        in_specs += [pl.BlockSpec((None, seq, DIFF_DV), lat_map), pl.BlockSpec((None, seq, LANES), lat_map)]
        args += [ak, av]
    in_specs += [pl.BlockSpec((None, ctx, DIFF_DV), ctx_map), pl.BlockSpec((None, ctx, LANES), ctx_map)]
    args += [ak, av]
    aliases = {}
    if not has_lat:
        in_specs.append(pl.BlockSpec(memory_space=pl.ANY))
        args.append(prev)
        aliases = {len(args) - 1: 0}
    kern = functools.partial(_attn_kernel, lam_init=lam_init, has_lat=has_lat)
    if not has_lat:
        kern = _drop_last_input(kern, n_in=len(args))
    return pl.pallas_call(
        kern,
        grid=(batch, N_HEADS, n_q),
        in_specs=in_specs,
        out_specs=pl.BlockSpec((None, tq, DIFF_DV), qmap),
        out_shape=jax.ShapeDtypeStruct((N_HEADS, nt, DIFF_DV), F32),
        input_output_aliases=aliases,
        compiler_params=_cparams(("arbitrary", "arbitrary", "arbitrary")),
        name="diff_attention" if has_lat else "diff_attention_ctx",
    )(*args)


def _drop_last_input(kern, n_in):
    def wrapped(*refs):
        return kern(*refs[:n_in - 1], *refs[n_in:])
    return wrapped


def _dft_table_kernel(cx_ref, sx_ref, cy_ref, sy_ref, c_ref, s_ref, *, n, span):
    t = _iota((LANES, n), 1)
    row = _iota((LANES, n), 0)
    pick_a = jnp.where(t // span == row, 1.0, 0.0).astype(BF16)
    pick_b = jnp.where(t % span == row, 1.0, 0.0).astype(BF16)

    def widen(ref, pick):
        x = ref[...]
        hi = x.astype(BF16)
        lo = (x - hi.astype(F32)).astype(BF16)
        return _dot(hi, pick) + _dot(lo, pick)

    cx, sx = widen(cx_ref, pick_a), widen(sx_ref, pick_a)
    cy, sy = widen(cy_ref, pick_b), widen(sy_ref, pick_b)
    c_ref[...] = (cx * cy - sx * sy).astype(BF16)
    s_ref[...] = (sx * cy + cx * sy).astype(BF16)


def _dft_tables(n):
    span = 64
    assert n % span == 0 and n // span <= LANES
    k = jnp.arange(n, dtype=jnp.int32)[:, None]
    j = jnp.arange(LANES, dtype=jnp.int32)[None, :]

    def factor(step, count):
        ang = ((k * (j * step)) % (2 * n)).astype(F32) * (math.pi / n)
        live = j < count
        return jnp.where(live, jnp.cos(ang), 0.0), jnp.where(live, jnp.sin(ang), 0.0)

    cx, sx = factor(span, n // span)
    cy, sy = factor(1, span)
    tk = min(DFT_TABLE_ROWS, n)
    small = pl.BlockSpec((tk, LANES), lambda i: (i, 0))
    big = pl.BlockSpec((tk, n), lambda i: (i, 0))
    return pl.pallas_call(
        functools.partial(_dft_table_kernel, n=n, span=span),
        grid=(n // tk,),
        in_specs=[small] * 4,
        out_specs=[big, big],
        out_shape=[jax.ShapeDtypeStruct((n, n), BF16)] * 2,
        compiler_params=_cparams(("arbitrary",)),
        name="dft_tables",
    )(cx, sx, cy, sy)


def _hy_filter_kernel(w1t_ref, w1c_ref, w1s_ref, b1_ref, w2_ref, b2_ref, w3_ref, fr_ref,
                      bands_ref, deltas_ref, e_ref, d_ref, nyq_ref, *, n):
    pos_i = lax.broadcasted_iota(jnp.int32, (n, 1), 0)
    pos = pos_i.astype(F32)
    t = pos / (n - 1)
    ang = ((2.0 * math.pi) * pos / n) * bands_ref[...]
    pre = t * w1t_ref[...] + _dot_exact(jnp.cos(ang), w1c_ref[...]) - _dot_exact(jnp.sin(ang), w1s_ref[...])
    hdn = jnp.sin(fr_ref[0:1, :] * (pre + b1_ref[...]))
    hdn = jnp.sin(fr_ref[1:2, :] * (_dot_exact(hdn, w2_ref[...]) + b2_ref[...]))
    raw = _dot_exact(hdn, w3_ref[...])
    window = jnp.exp(-t * deltas_ref[...])
    hf = raw[:, :HY_CH] * window
    hb = jnp.where(pos_i > 0, raw[:, HY_CH:] * window, 0.0)
    inv = 1.0 / (jnp.sum(jnp.abs(hf), axis=0, keepdims=True) + jnp.sum(jnp.abs(hb), axis=0, keepdims=True))
    e = (hf + hb) * inv
    e_ref[...] = e.astype(BF16)
    d_ref[...] = ((hb - hf) * inv).astype(BF16)
    sign = (1 - 2 * (pos_i & 1)).astype(F32)
    nyq_ref[...] = jnp.sum(e * sign, axis=0, keepdims=True)


def _hyena_filter(n, w1, b1, w2, b2, w3, freq):
    bands = jnp.linspace(1e-4, HY_BANDS - 1, HY_BANDS, dtype=F32).reshape(1, HY_BANDS)
    max_decay = math.log(HY_DECAY_TARGET) / HY_FAST_DECAY_PCT
    min_decay = math.log(HY_DECAY_TARGET) / HY_SLOW_DECAY_PCT
    deltas = jnp.abs(jnp.linspace(min_decay, max_decay, HY_CH, dtype=F32)).reshape(1, HY_CH)
    args = [w1[0:1], w1[1:1 + HY_BANDS], w1[1 + HY_BANDS:], b1.reshape(1, HY_FFN), w2, b2.reshape(1, HY_FFN),
            w3, freq, bands, deltas]
    return pl.pallas_call(
        functools.partial(_hy_filter_kernel, n=n),
        out_shape=[jax.ShapeDtypeStruct((n, HY_CH), BF16), jax.ShapeDtypeStruct((n, HY_CH), BF16),
                   jax.ShapeDtypeStruct((1, HY_CH), F32)],
        compiler_params=pltpu.CompilerParams(vmem_limit_bytes=VMEM_LIMIT),
        name="hyena_filter",
    )(*args)


def _hy_gate_kernel(x0_ref, x1_ref, v_ref, w0_ref, w1_ref, wv_ref, b0_ref, b1_ref, bv_ref,
                    zb_ref, x0c_ref, zf_ref, nyq_ref, *, n):
    row = lax.broadcasted_iota(jnp.int32, (n, 1), 0)

    def conv(u_ref, w_ref, b_ref):
        u = u_ref[...]
        up = jnp.where(row > 0, pltpu.roll(u, 1, 0), 0.0)
        dn = jnp.where(row < n - 1, pltpu.roll(u, n - 1, 0), 0.0)
        return up * w_ref[0:1, :] + u * w_ref[1:2, :] + dn * w_ref[2:3, :] + b_ref[...]

    z = conv(x1_ref, w1_ref, b1_ref) * conv(v_ref, wv_ref, bv_ref)
    x0c_ref[...] = conv(x0_ref, w0_ref, b0_ref)
    zf_ref[...] = z
    zb_ref[...] = z.astype(BF16)
    sign = (1 - 2 * (row & 1)).astype(F32)
    nyq_ref[...] = jnp.sum(z * sign, axis=0, keepdims=True)


def _hyena_gate(bu, short_w, short_b, n, batch, row_blk0):
    halves = HY_CH // LANES
    sb = short_b.reshape(1, 3 * HY_CH)
    seg = lambda part: pl.BlockSpec((n, LANES), lambda b, j: (row_blk0 + b, part * halves + j))
    wsp = lambda part: pl.BlockSpec((3, LANES), lambda b, j: (0, part * halves + j))
    bsp = lambda part: pl.BlockSpec((1, LANES), lambda b, j: (0, part * halves + j))
    return pl.pallas_call(
        functools.partial(_hy_gate_kernel, n=n),
        grid=(batch, halves),
        in_specs=[seg(0), seg(1), seg(2), wsp(0), wsp(1), wsp(2), bsp(0), bsp(1), bsp(2)],
        out_specs=[pl.BlockSpec((n, LANES), lambda b, j: (0, b * halves + j)),
                   pl.BlockSpec((n, LANES), lambda b, j: (b, j)),
                   pl.BlockSpec((n, LANES), lambda b, j: (b, j)),
                   pl.BlockSpec((None, 1, LANES), lambda b, j: (b, 0, j))],
        out_shape=[jax.ShapeDtypeStruct((n, batch * HY_CH), BF16),
                   jax.ShapeDtypeStruct((batch * n, HY_CH), F32),
                   jax.ShapeDtypeStruct((batch * n, HY_CH), F32),
                   jax.ShapeDtypeStruct((batch, 1, HY_CH), F32)],
        compiler_params=_cparams(("arbitrary", "arbitrary")),
        name="hyena_gate",
    )(bu, bu, bu, short_w, short_w, short_w, sb, sb, sb)


def _hy_spectrum_kernel(c_ref, s_ref, z_ref, e_ref, d_ref, yr_ref, yi_ref, *, n, tk, batch):
    c = c_ref[...]
    s = s_ref[...]
    zr = _dot(c, z_ref[...])
    zs = _dot(s, z_ref[...])
    fr = _dot(c, e_ref[...])
    fi = _dot(s, d_ref[...])
    k = pl.program_id(0) * tk + lax.broadcasted_iota(jnp.int32, (tk, 1), 0)
    wk = jnp.where(k == 0, 1.0, 2.0) * (1.0 / (2 * n))
    for b in range(batch):
        sl = slice(b * HY_CH, (b + 1) * HY_CH)
        yr = zr[:, sl] * fr + zs[:, sl] * fi
        yi = zr[:, sl] * fi - zs[:, sl] * fr
        yr_ref[:, sl] = (yr * wk).astype(BF16)
        yi_ref[:, sl] = (-(yi * wk)).astype(BF16)


def _hy_inverse_kernel(c_ref, s_ref, yr_ref, yi_ref, x0c_ref, zf_ref, nyqz_ref, nyqf_ref, bias_ref,
                       *rest, n, tt, batch):
    o_ref, y_scr = rest[-2], rest[-1]
    b = pl.program_id(1)

    @pl.when(b == 0)
    def _():
        y = _dot(c_ref[...], yr_ref[...]) + _dot(s_ref[...], yi_ref[...])
        for bb in range(batch):
            y_scr[bb] = y[:, bb * HY_CH:(bb + 1) * HY_CH]

    t = pl.program_id(0) * tt + lax.broadcasted_iota(jnp.int32, (tt, 1), 0)
    sign = (1 - 2 * (t & 1)).astype(F32)
    nyq = nyqz_ref[...] * nyqf_ref[...] * (1.0 / (2 * n))
    zf = zf_ref[...]
    o_ref[...] = x0c_ref[...] * (y_scr[b] + sign * nyq + bias_ref[...] * zf)


def _hyena_conv(cs, zb, e, d, x0c, zf, nyqz, nyqf, bias, n, batch, nt, out_blk0, prev=None):
    c_tab, s_tab = cs
    bw = batch * HY_CH
    tk = min(512, n)
    whole = lambda shape: pl.BlockSpec(shape, lambda *_: (0,) * len(shape), pipeline_mode=pl.Buffered(1))
    yr, yi = pl.pallas_call(
        functools.partial(_hy_spectrum_kernel, n=n, tk=tk, batch=batch),
        grid=(n // tk,),
        in_specs=[pl.BlockSpec((tk, n), lambda i: (i, 0)), pl.BlockSpec((tk, n), lambda i: (i, 0)),
                  whole((n, bw)), whole((n, HY_CH)), whole((n, HY_CH))],
        out_specs=[pl.BlockSpec((tk, bw), lambda i: (i, 0))] * 2,
        out_shape=[jax.ShapeDtypeStruct((n, bw), BF16)] * 2,
        compiler_params=_cparams(("arbitrary",)),
        name="hyena_spectrum",
    )(c_tab, s_tab, zb, e, d)

    tt = min(512, n)
    n_t = n // tt
    seg = lambda i, b: (b * n_t + i, 0)
    in_specs = [pl.BlockSpec((tt, n), lambda i, b: (i, 0)), pl.BlockSpec((tt, n), lambda i, b: (i, 0)),
                whole((n, bw)), whole((n, bw)),
                pl.BlockSpec((tt, HY_CH), seg), pl.BlockSpec((tt, HY_CH), seg),
                pl.BlockSpec((None, 1, HY_CH), lambda i, b: (b, 0, 0)),
                pl.BlockSpec((1, HY_CH), lambda i, b: (0, 0)), pl.BlockSpec((1, HY_CH), lambda i, b: (0, 0))]
    args = [c_tab, s_tab, yr, yi, x0c, zf, nyqz, nyqf, bias.reshape(1, HY_CH)]
    aliases = {}
    if prev is not None:
        in_specs.append(pl.BlockSpec(memory_space=pl.ANY))
        args.append(prev)
        aliases = {len(args) - 1: 0}
    return pl.pallas_call(
        functools.partial(_hy_inverse_kernel, n=n, tt=tt, batch=batch),
        grid=(n_t, batch),
        in_specs=in_specs,
        out_specs=pl.BlockSpec((tt, HY_CH), lambda i, b: (out_blk0 + b * n_t + i, 0)),
        out_shape=jax.ShapeDtypeStruct((nt, HY_CH), F32),
        scratch_shapes=[pltpu.VMEM((batch, tt, HY_CH), F32)],
        input_output_aliases=aliases,
        compiler_params=_cparams(("arbitrary", "arbitrary")),
        name="hyena_inverse" if prev is None else "hyena_inverse_ctx",
    )(*args)


def _scan_maps(geo, cc):
    batch, seq, ctx = geo["batch"], geo["seq"], geo["ctx"]
    n_cc, n_lc = ctx // cc, seq // cc
    ctx0 = (batch * seq) // cc

    def fwd(b, i):
        return (jnp.where(i < n_cc, ctx0 + b * n_cc + i, b * n_lc + i - n_cc), 0)

    def bwd(b, i):
        return (jnp.where(i < n_cc, ctx0 + b * n_cc + (n_cc - 1 - i), b * n_lc + (n_lc - 1 - (i - n_cc))), 0)

    return fwd, bwd, n_cc + n_lc


def _ret_kernel(qf_ref, kf_ref, vf_ref, qb_ref, kb_ref, vb_ref, dec_ref, of_ref, ob_ref, st_ref):
    @pl.when(pl.program_id(1) == 0)
    def _():
        st_ref[...] = jnp.zeros_like(st_ref)

    cc = qf_ref.shape[0]
    r_i, c_i = _iota((cc, cc), 0), _iota((cc, cc), 1)
    pos = _iota((cc, 1), 0).astype(F32)
    for d, (q_ref, k_ref, v_ref, o_ref) in enumerate(((qf_ref, kf_ref, vf_ref, of_ref),
                                                      (qb_ref, kb_ref, vb_ref, ob_ref))):
        reverse = d == 1
        lg = -jnp.exp(dec_ref[d])
        steps_in = (cc - pos) if reverse else (pos + 1.0)
        steps_out = pos if reverse else (cc - 1.0 - pos)
        q = q_ref[...]
        k = k_ref[...] * (RET_DH ** -0.5)
        qd = (q * jnp.exp(steps_in * lg)).astype(BF16)
        kd = (k * jnp.exp(steps_out * lg)).astype(BF16)
        qb, kb, vb = q.astype(BF16), k.astype(BF16), v_ref[...].astype(BF16)
        gain = jnp.exp(cc * lg)
        dist = (c_i - r_i) if reverse else (r_i - c_i)
        keep = dist >= 0
        dist_f = jnp.where(keep, dist, 0).astype(F32)
        for h in range(N_HEADS):
            hs = slice(h * RET_DH, (h + 1) * RET_DH)
            decay = jnp.where(keep, jnp.exp(dist_f * lg[:, h * RET_DH:h * RET_DH + 1]), 0.0)
            sc = (_dot_nt(qb[:, hs], kb[:, hs]) * decay).astype(BF16)
            st = st_ref[d, h]
            o_ref[:, hs] = _dot(sc, vb[:, hs]) + _dot_nt(qd[:, hs], st.astype(BF16))
            st_ref[d, h] = st * gain[:, hs] + _dot_tn(vb[:, hs], kd[:, hs])


def _retention(q, k, v, dec, geo):
    nt, w = q.shape
    cc = RET_CHUNK
    fwd, bwd, steps = _scan_maps(geo, cc)
    blk = lambda m: pl.BlockSpec((cc, w), m)
    return pl.pallas_call(
        _ret_kernel,
        grid=(geo["batch"], steps),
        in_specs=[blk(fwd), blk(fwd), blk(fwd), blk(bwd), blk(bwd), blk(bwd),
                  pl.BlockSpec((2, 1, w), lambda b, i: (0, 0, 0))],
        out_specs=[blk(fwd), blk(bwd)],
        out_shape=[jax.ShapeDtypeStruct((nt, w), F32)] * 2,
        scratch_shapes=[pltpu.VMEM((2, N_HEADS, RET_DH, RET_DH), F32)],
        compiler_params=_cparams(("arbitrary", "arbitrary")),
        name="recurrence_ret",
    )(q, k, v, q, k, v, dec)


def _gla_direction(q, k, v, la, st_ref, o_ref, *, reverse):
    cc, wk = q.shape
    wv = v.shape[1]
    r_i, c_i = _iota((cc, cc), 0), _iota((cc, cc), 1)
    incl = (c_i >= r_i) if reverse else (c_i <= r_i)
    cum = _dot_exact(jnp.where(incl, 1.0, 0.0), la)
    cum_end = cum[0:1] if reverse else cum[cc - 1:cc]

    qd = (q * jnp.exp(cum)).astype(BF16)
    kd = (k * jnp.exp(cum_end - cum)).astype(BF16)
    vb = v.astype(BF16)
    st = st_ref[...]
    inter = _dot_nt(qd, st.astype(BF16))
    same_head = (_iota((wv, wk), 0) // GLA_DV) == (_iota((wv, wk), 1) // GLA_DK)
    st_ref[...] = st * jnp.exp(cum_end) + jnp.where(same_head, _dot_tn(vb, kd), 0.0)

    sub = REC_SUB
    hs = N_HEADS * sub
    q_own = (_iota((hs, wk), 0) // sub) == (_iota((hs, wk), 1) // GLA_DK)
    for j in range(cc // sub):
        r0, r1 = j * sub, (j + 1) * sub
        ka, kb = (r0, cc) if reverse else (0, r1)
        mid = cum[r0 + sub // 2:r0 + sub // 2 + 1]
        qj = q[r0:r1] * jnp.exp(jnp.minimum(cum[r0:r1] - mid, EXP_CLAMP))
        kj = (k[ka:kb] * jnp.exp(jnp.minimum(mid - cum[ka:kb], EXP_CLAMP))).astype(BF16)
        q_stack = jnp.where(q_own, jnp.concatenate([qj] * N_HEADS, axis=0), 0.0).astype(BF16)
        rows = r0 + (_iota((hs, kb - ka), 0) % sub)
        cols = ka + _iota((hs, kb - ka), 1)
        keep = (cols >= rows) if reverse else (cols <= rows)
        sc = jnp.where(keep, _dot_nt(q_stack, kj), 0.0).astype(BF16)
        full = _dot(sc, vb[ka:kb])
        lane_head = _iota((sub, wv), 1) // GLA_DV
        oj = inter[r0:r1]
        for h in range(N_HEADS):
            oj = oj + jnp.where(lane_head == h, full[h * sub:(h + 1) * sub], 0.0)
        o_ref[r0:r1, :] = oj


def _gla_kernel(*refs, batch):
    per_batch = refs[:8 * batch]
    gw_ref, gb_ref, of_ref, ob_ref, st_ref = refs[8 * batch:]

    @pl.when(pl.program_id(0) == 0)
    def _():
        st_ref[...] = jnp.zeros_like(st_ref)

    for b in range(batch):
        qf_ref, kf_ref, vf_ref, lf_ref, qb_ref, kb_ref, vb_ref, lb_ref = per_batch[8 * b:8 * b + 8]
        for d, (q_ref, k_ref, v_ref, l_ref, o_ref) in enumerate(((qf_ref, kf_ref, vf_ref, lf_ref, of_ref),
                                                                 (qb_ref, kb_ref, vb_ref, lb_ref, ob_ref))):
            logit = _dot_exact(l_ref[...], gw_ref[d]) + gb_ref[d]
            la = (jnp.minimum(logit, 0.0) - jnp.log(1.0 + jnp.exp(-jnp.abs(logit)))) * (1.0 / GLA_GATE_NORM)
            _gla_direction(q_ref[...] * (GLA_DK ** -0.5), k_ref[...], v_ref[...], la, st_ref.at[b, d],
                           o_ref.at[b], reverse=(d == 1))


def _gla(q, k, v, glr, gw, gb, geo):
    batch, seq, ctx = geo["batch"], geo["seq"], geo["ctx"]
    wq, wv = q.shape[1], v.shape[1]
    cc = REC_CHUNK
    n_cc, n_lc = ctx // cc, seq // cc
    ctx0 = (batch * seq) // cc
    in_specs, args = [], []
    for b in range(batch):
        fwd = lambda i, b=b: (jnp.where(i < n_cc, ctx0 + b * n_cc + i, b * n_lc + i - n_cc), 0)
        bwd = lambda i, b=b: (jnp.where(i < n_cc, ctx0 + b * n_cc + (n_cc - 1 - i),
                                        b * n_lc + (n_lc - 1 - (i - n_cc))), 0)
        for m in (fwd, bwd):
            in_specs += [pl.BlockSpec((cc, wq), m), pl.BlockSpec((cc, wq), m), pl.BlockSpec((cc, wv), m),
                         pl.BlockSpec((cc, LANES), m)]
            args += [q, k, v, glr]
    in_specs += [pl.BlockSpec((2, LANES, wq), lambda i: (0, 0, 0)), pl.BlockSpec((2, 1, wq), lambda i: (0, 0, 0))]
    args += [gw, gb]
    steps = n_cc + n_lc
    out_blk = lambda m: pl.BlockSpec((batch, cc, wv), m)
    return pl.pallas_call(
        functools.partial(_gla_kernel, batch=batch),
        grid=(steps,),
        in_specs=in_specs,
        out_specs=[out_blk(lambda i: (0, i, 0)),
                   out_blk(lambda i: (0, jnp.where(i < n_cc, n_cc - 1 - i, steps - 1 - (i - n_cc)), 0))],
        out_shape=[jax.ShapeDtypeStruct((batch, ctx + seq, wv), F32)] * 2,
        scratch_shapes=[pltpu.VMEM((batch, 2, wv, wq), F32)],
        compiler_params=_cparams(("arbitrary",)),
        name="recurrence_gla",
    )(*args)


def _mixout_kernel(x_ref, mod_ref, at_ref, hy_ref, rf_ref, rb_ref, cg_ref, gf_ref, gb_ref, dr_ref,
                   seg_ref, rg_ref, gg_ref, wo_ref, g1_ref, g2_ref, rwh_ref, rwl_ref, rbias_ref,
                   x1_ref, h2_ref, ti_ref, tw_ref, rk_ref, cnt_ref):
    @pl.when(pl.program_id(0) == 0)
    def _():
        cnt_ref[...] = jnp.zeros_like(cnt_ref)

    def head_norm(o):
        ms = _dot((o * o).astype(BF16), seg_ref[...]) * (1.0 / RET_DH)
        return o * lax.rsqrt(ms + EPS)

    a = jnp.concatenate([at_ref[h] for h in range(N_HEADS)], axis=-1)
    cg = cg_ref[...]
    dr = dr_ref[...]
    rt = head_norm(rf_ref[...] + rb_ref[...]) * rg_ref[...] * (cg * _sigmoid(cg))
    gl = head_norm(gf_ref[...] + gb_ref[...]) * gg_ref[...] * (dr * _sigmoid(dr))
    cat = jnp.concatenate([a, hy_ref[...], rt, gl], axis=-1).astype(BF16)
    y = _dot(cat, wo_ref[...])
    x1 = x_ref[...] + mod_ref[2:3, :] * (_rms(y) * g1_ref[...])
    x1_ref[...] = x1
    h2 = _rms(x1) * g2_ref[...] * (1.0 + mod_ref[4:5, :]) + mod_ref[3:4, :]
    h2_ref[...] = h2

    h_hi = h2.astype(BF16)
    h_lo = (h2 - h_hi.astype(F32)).astype(BF16)
    vals = (_dot(h_hi, rwh_ref[...]) + _dot(h_lo, rwh_ref[...]) + _dot(h_hi, rwl_ref[...])
            + rbias_ref[...])
    tm = vals.shape[0]
    lane = lax.broadcasted_iota(jnp.int32, vals.shape, 1)
    idx_out = jnp.zeros(vals.shape, jnp.int32)
    w_out = jnp.zeros(vals.shape, F32)
    top = None
    den = 0.0
    picks = []
    for r in range(TOP_K):
        m = jnp.max(vals, axis=-1, keepdims=True)
        idx = jnp.min(jnp.where(vals == m, lane, LANES), axis=-1, keepdims=True)
        hit = lane == idx
        vals = jnp.where(hit, -jnp.inf, vals)
        top = m if top is None else top
        e = jnp.exp(m - top)
        den = den + e
        picks.append((idx, e, hit))
    inv = 1.0 / den
    for r, (idx, e, _) in enumerate(picks):
        idx_out = jnp.where(lane == r, idx, idx_out)
        w_out = jnp.where(lane == r, e * inv, w_out)
    ti_ref[...] = idx_out
    tw_ref[...] = w_out

    chosen = functools.reduce(jnp.logical_or, [hit for _, _, hit in picks])
    chosen_f = jnp.where(chosen, 1.0, 0.0)
    earlier = (lax.broadcasted_iota(jnp.int32, (tm, tm), 1) < lax.broadcasted_iota(jnp.int32, (tm, tm), 0))
    before = _dot(jnp.where(earlier, 1.0, 0.0).astype(BF16), chosen_f.astype(BF16)) + cnt_ref[...]
    rk_out = jnp.zeros(vals.shape, jnp.int32)
    for r, (_, _, hit) in enumerate(picks):
        rank = jnp.sum(jnp.where(hit, before, 0.0), axis=-1, keepdims=True)
        rk_out = jnp.where(lane == r, rank.astype(jnp.int32), rk_out)
    rk_ref[...] = rk_out
    cnt_ref[...] = cnt_ref[...] + jnp.sum(chosen_f, axis=0, keepdims=True)


def _mix_out(n_tiles, x, mod, at, hy, rf, rb, cg, gf, gb, dr, seg, rg, gg, wo, g1, g2, rw, rbias, geo):
    rw_hi = rw.astype(BF16)
    rw_lo = (rw - rw_hi.astype(F32)).astype(BF16)
    nt, d = x.shape
    tm = ROW_TILE
    lat_tiles, per_batch, batch = geo["lat_tiles"], geo["tiles_per_batch"], geo["batch"]

    def mod_map(i):
        return (jnp.where(i < lat_tiles, i // per_batch, batch), 0, 0)

    row = lambda i: (i, 0)
    const = lambda i: (0, 0)
    g_blk = pl.BlockSpec((tm, GROUP_W), row)
    ctx_tiles = geo["ctx"] // tm

    def scan_map(i):
        c = i - lat_tiles
        return (jnp.where(i < lat_tiles, i // per_batch, c // ctx_tiles),
                jnp.where(i < lat_tiles, ctx_tiles + i % per_batch, c % ctx_tiles), 0)

    s_blk = pl.BlockSpec((None, tm, GROUP_W), scan_map)
    return pl.pallas_call(
        _mixout_kernel,
        grid=(n_tiles,),
        in_specs=[pl.BlockSpec((tm, d), row), pl.BlockSpec((None, 6, d), mod_map),
                  pl.BlockSpec((N_HEADS, tm, DIFF_DV), lambda i: (0, i, 0)),
                  g_blk, g_blk, g_blk, g_blk, s_blk, s_blk, g_blk,
                  pl.BlockSpec((GROUP_W, GROUP_W), const), pl.BlockSpec((1, GROUP_W), const),
                  pl.BlockSpec((1, GROUP_W), const),
                  pl.BlockSpec((d, d), const, pipeline_mode=pl.Buffered(1)),
                  pl.BlockSpec((1, d), const), pl.BlockSpec((1, d), const),
                  pl.BlockSpec((d, LANES), const), pl.BlockSpec((d, LANES), const), pl.BlockSpec((1, LANES), const)],
        out_specs=[pl.BlockSpec((tm, d), row), pl.BlockSpec((tm, d), row),
                   pl.BlockSpec((tm, LANES), row), pl.BlockSpec((tm, LANES), row),
                   pl.BlockSpec((tm, LANES), row), pl.BlockSpec((1, LANES), const)],
        out_shape=[jax.ShapeDtypeStruct((nt, d), F32), jax.ShapeDtypeStruct((nt, d), F32),
                   jax.ShapeDtypeStruct((nt, LANES), jnp.int32), jax.ShapeDtypeStruct((nt, LANES), F32),
                   jax.ShapeDtypeStruct((nt, LANES), jnp.int32), jax.ShapeDtypeStruct((1, LANES), F32)],
        compiler_params=_cparams(("arbitrary",)),
        name="mix_out_router",
    )(x, mod, at, hy, rf, rb, cg, gf, gb, dr, seg.astype(BF16), rg, gg, wo, g1, g2, rw_hi, rw_lo, rbias)


def _expert_kernel(te_ref, na_ref, wres_ref, xs_ref, w1_ref, b1_ref, w2_ref, b2_ref, perm_ref, ys_ref, w1s, w2s):
    i = pl.program_id(0)
    active = i < na_ref[0]
    fresh = jnp.logical_or(i == 0, te_ref[i] != te_ref[jnp.maximum(i - 1, 0)])
    n_groups = w1_ref.shape[1] // (2 * LANES)

    @pl.when(jnp.logical_and(active, fresh))
    def _():
        for c in range(n_groups):
            cols = slice(c * 2 * LANES, (c + 1) * 2 * LANES)
            w1s[:, cols] = _dot(w1_ref[:, cols].astype(BF16), perm_ref[...]).astype(BF16)
        w2s[...] = w2_ref[...].astype(BF16)

    @pl.when(active)
    def _():
        u = _dot(xs_ref[...].astype(BF16), w1s[...]) + b1_ref[...]
        acts = []
        for c in range(n_groups):
            glu = jnp.minimum(u[:, c * 2 * LANES:c * 2 * LANES + LANES], SWIGLU_LIMIT)
            lin = jnp.clip(u[:, c * 2 * LANES + LANES:(c + 1) * 2 * LANES], -SWIGLU_LIMIT, SWIGLU_LIMIT)
            acts.append((glu * _sigmoid(SWIGLU_ALPHA * glu) * (lin + 1.0)).astype(BF16))
        ys_ref[...] = _dot(jnp.concatenate(acts, axis=-1), w2s[...]) + b2_ref[...]

    @pl.when(jnp.logical_not(active))
    def _():
        ys_ref[...] = jnp.zeros_like(ys_ref)


def _expert_ffn(tile_e, n_active, xs, w1, b1, w2, b2, layer):
    p, d = xs.shape
    tm = MOE_TILE
    de2 = w1.shape[3]
    de = de2 // 2
    sel = np.zeros((2 * LANES, 2 * LANES), np.float32)
    sel[2 * np.arange(LANES), np.arange(LANES)] = 1.0
    sel[2 * np.arange(LANES) + 1, LANES + np.arange(LANES)] = 1.0
    b1g = b1.reshape(N_EXPERTS, de2 // (2 * LANES), LANES, 2).transpose(0, 1, 3, 2).reshape(N_EXPERTS, 1, de2)
    n_tiles = p // tm
    tiles = jnp.arange(n_tiles, dtype=jnp.int32)
    live = tiles < n_active[0]
    te_live = jnp.where(live, tile_e, jnp.max(jnp.where(live, tile_e, 0)))
    first = jnp.concatenate([jnp.ones((1,), bool), te_live[1:] != te_live[:-1]])
    later = jnp.where(te_live[None, :] > te_live[:, None], te_live[None, :], N_EXPERTS)
    following = jnp.min(later, axis=1)
    w_res = jnp.where(first | (following == N_EXPERTS), te_live, following).astype(jnp.int32)
    wmap = lambda i, te, na, wr: (te[i], 0, 0)
    lmap = lambda i, te, na, wr: (layer, wr[i], 0, 0)
    xmap = lambda i, te, na, wr: (jnp.minimum(i, jnp.maximum(na[0] - 1, 0)), 0)
    grid_spec = pltpu.PrefetchScalarGridSpec(
        num_scalar_prefetch=3,
        grid=(p // tm,),
        in_specs=[pl.BlockSpec((tm, d), xmap),
                  pl.BlockSpec((None, None, d, de2), lmap), pl.BlockSpec((None, 1, de2), wmap),
                  pl.BlockSpec((None, None, de, d), lmap), pl.BlockSpec((None, 1, d), wmap),
                  pl.BlockSpec((2 * LANES, 2 * LANES), lambda i, te, na, wr: (0, 0))],
        out_specs=pl.BlockSpec((tm, d), lambda i, te, na, wr: (i, 0)),
        scratch_shapes=[pltpu.VMEM((d, de2), BF16), pltpu.VMEM((de, d), BF16)],
    )
    return pl.pallas_call(
        _expert_kernel,
        grid_spec=grid_spec,
        out_shape=jax.ShapeDtypeStruct((p, d), F32),
        compiler_params=_cparams(("arbitrary",)),
        name="expert_ffn",
    )(te_live, n_active, w_res, xs, w1, b1g, w2, b2.reshape(N_EXPERTS, 1, d), jnp.asarray(sel, BF16))


def _route_plan(topi, rank, counts, tm):
    n, k = topi.shape
    padded = ((counts + tm - 1) // tm) * tm
    pend = jnp.cumsum(padded)
    pstart = pend - padded
    experts = jnp.arange(N_EXPERTS, dtype=jnp.int32)
    start = jnp.sum(jnp.where(topi[:, :, None] == experts, pstart, 0), axis=-1)
    dest = (start + rank).astype(jnp.int32)
    p = n * k + N_EXPERTS * tm
    tile_start = jnp.arange(p // tm, dtype=jnp.int32) * tm
    tile_e = jnp.minimum(jnp.sum((tile_start[:, None] >= pend[None, :]).astype(jnp.int32), axis=1), N_EXPERTS - 1)
    n_active = (pend[-1:] // tm).astype(jnp.int32)
    return dest, tile_e, n_active, p, jnp.stack([pstart, pend]).astype(jnp.int32)


def _dispatch_kernel(bounds_ref, dest_ref, x_ref, out_ref, zeros, sem, zsem, *, tb):
    n_rows = tb * TOP_K

    @pl.when(pl.program_id(0) == 0)
    def _():
        zeros[...] = jnp.zeros_like(zeros)

        def fill(e):
            start = pl.multiple_of(bounds_ref[1, e] - MOE_TILE, MOE_TILE)
            return pltpu.make_async_copy(zeros, out_ref.at[pl.ds(start, MOE_TILE)], zsem)

        for e in range(N_EXPERTS):
            @pl.when(bounds_ref[1, e] > bounds_ref[0, e])
            def _():
                fill(e).start()
        for e in range(N_EXPERTS):
            @pl.when(bounds_ref[1, e] > bounds_ref[0, e])
            def _():
                fill(e).wait()

    def issue(t, carry):
        for k in range(TOP_K):
            pltpu.make_async_copy(x_ref.at[pl.ds(t, 1)], out_ref.at[pl.ds(dest_ref[0, t * TOP_K + k], 1)],
                                  sem).start()
        return carry

    lax.fori_loop(0, tb, issue, 0, unroll=4)
    pltpu.make_async_copy(out_ref.at[pl.ds(0, n_rows)], out_ref.at[pl.ds(0, n_rows)], sem).wait()


def _dispatch(h2, dest, bounds, p_rows, n_tok):
    d = h2.shape[1]
    tb = ROW_TILE
    steps = n_tok // tb
    return pl.pallas_call(
        functools.partial(_dispatch_kernel, tb=tb),
        grid=(steps,),
        in_specs=[pl.BlockSpec(memory_space=pltpu.SMEM),
                  pl.BlockSpec((None, 1, tb * TOP_K), lambda i: (i, 0, 0), memory_space=pltpu.SMEM),
                  pl.BlockSpec((tb, d), lambda i: (i, 0))],
        out_specs=pl.BlockSpec(memory_space=pl.ANY),
        out_shape=jax.ShapeDtypeStruct((p_rows, d), F32),
        scratch_shapes=[pltpu.VMEM((MOE_TILE, d), F32), pltpu.SemaphoreType.DMA(()), pltpu.SemaphoreType.DMA(())],
        compiler_params=pltpu.CompilerParams(dimension_semantics=("arbitrary",), has_side_effects=True),
        name="moe_dispatch",
    )(bounds, dest.reshape(steps, 1, tb * TOP_K), h2)


def _combine_kernel(dest_ref, x_ref, tw_ref, mod_ref, g_ref, ys_ref, o_ref, buf, sem, *, tb):
    n_rows = tb * TOP_K

    def issue(r, carry):
        pltpu.make_async_copy(ys_ref.at[pl.ds(dest_ref[0, r], 1)], buf.at[pl.ds(r, 1)], sem).start()
        return carry

    lax.fori_loop(0, n_rows, issue, 0, unroll=8)
    pltpu.make_async_copy(ys_ref.at[pl.ds(0, n_rows)], buf, sem).wait()
    y = None
    for k in range(TOP_K):
        t = buf[k * tb:(k + 1) * tb, :] * tw_ref[:, k:k + 1]
        y = t if y is None else y + t
    o_ref[...] = x_ref[...] + mod_ref[5:6, :] * (_rms(y) * g_ref[...])


def _combine_residual(n_tiles, x, ys, dest, topw, mod, g, geo):
    nt, d = x.shape
    tb = ROW_TILE
    lat_tiles, per_batch, batch = geo["lat_tiles"], geo["tiles_per_batch"], geo["batch"]

    def mod_map(i):
        return (jnp.where(i < lat_tiles, i // per_batch, batch), 0, 0)

    row = lambda i: (i, 0)
    slot_major = jnp.swapaxes(dest.reshape(n_tiles, tb, TOP_K), 1, 2).reshape(n_tiles, 1, tb * TOP_K)
    return pl.pallas_call(
        functools.partial(_combine_kernel, tb=tb),
        grid=(n_tiles,),
        in_specs=[pl.BlockSpec((None, 1, tb * TOP_K), lambda i: (i, 0, 0), memory_space=pltpu.SMEM),
                  pl.BlockSpec((tb, d), row), pl.BlockSpec((tb, LANES), row),
                  pl.BlockSpec((None, 6, d), mod_map), pl.BlockSpec((1, d), lambda i: (0, 0)),
                  pl.BlockSpec(memory_space=pl.ANY)],
        out_specs=pl.BlockSpec((tb, d), row),
        out_shape=jax.ShapeDtypeStruct((nt, d), F32),
        scratch_shapes=[pltpu.VMEM((tb * TOP_K, d), F32), pltpu.SemaphoreType.DMA(())],
        compiler_params=_cparams(("arbitrary",)),
        name="combine_residual",
    )(slot_major, x, topw, mod, g, ys)


def _rope_partner(dim):
    h, q = dim // 2, dim // 4
    perm = np.zeros(dim, np.int32)
    sign = np.zeros(dim, np.float32)
    for base in (0, h):
        for j in range(q):
            perm[base + j], sign[base + j] = base + j + q, -1.0
            perm[base + q + j], sign[base + q + j] = base + j, 1.0
    return perm, sign


def _rot_cols(w, dim):
    perm, sign = _rope_partner(dim)
    reps = w.shape[1] // dim
    full_perm = np.concatenate([perm + r * dim for r in range(reps)])
    return w[:, full_perm] * jnp.asarray(np.tile(sign, reps))


def _rope_tables(n_tok, dim, reps, pad_rows):
    rows = n_tok // GRID_W
    row = jnp.repeat(jnp.arange(rows, dtype=F32), GRID_W)
    col = jnp.tile(jnp.arange(GRID_W, dtype=F32), rows)
    quarter = dim // 4
    inv = ROPE_BASE ** (-jnp.arange(quarter, dtype=F32) / quarter)
    ar = row[:, None] * inv[None]
    ac = col[:, None] * inv[None]
    ang = jnp.concatenate([ar, ar, ac, ac], axis=-1)
    cos = jnp.concatenate([jnp.tile(jnp.cos(ang), (1, reps)), jnp.ones((pad_rows, dim * reps), F32)])
    sin = jnp.concatenate([jnp.tile(jnp.sin(ang), (1, reps)), jnp.zeros((pad_rows, dim * reps), F32)])
    return cos, sin


def _widen_w_in(w):
    sizes = (256, 256, 256, 768, 256, 256, 256, 256, 128, 128, 256, 256, 32)
    cuts = np.cumsum(sizes)[:-1].tolist()
    aq, ak, av, bu, cq, ck, cv, cg, dq, dk, dv, dr, dl = jnp.split(w, cuts, axis=1)
    dl = jnp.pad(dl, ((0, 0), (0, LANES - dl.shape[1])))
    parts = [aq, _rot_cols(aq, DIFF_DQK), ak, _rot_cols(ak, DIFF_DQK), av, bu,
             cq, _rot_cols(cq, RET_DH), ck, _rot_cols(ck, RET_DH), cv, cg, dq, dk, dv, dr, dl]
    return jnp.concatenate(parts, axis=1).astype(BF16)


def kernel(x, c, ctx, c_ctx, w_mod, b_mod, norm_g, w_in, w_out, diff_lambda, diff_subln_g, hy_short_w, hy_short_b, hy_w1, hy_b1, hy_w2, hy_b2, hy_w3, hy_freq, hy_bias, ret_decay, ret_norm_g, gla_gate_w, gla_gate_b, gla_norm_g, router_w, router_b, exp_w1, exp_b1, exp_w2, exp_b2):
    batch, seq, d = x.shape
    n_ctx = ctx.shape[1]
    depth = w_mod.shape[0]
    n_lat_rows, n_ctx_rows = batch * seq, batch * n_ctx
    nt = n_lat_rows + n_ctx_rows
    assert d == D_MODEL and seq % ROW_TILE == 0 and n_ctx % ROW_TILE == 0 and seq % n_ctx == 0
    geo = dict(batch=batch, seq=seq, ctx=n_ctx, lat_tiles=n_lat_rows // ROW_TILE,
               tiles_per_batch=seq // ROW_TILE)

    xs = jnp.concatenate([x.reshape(n_lat_rows, d), ctx.reshape(n_ctx_rows, d)], axis=0)
    mod_rows = 8
    cc = jnp.zeros((mod_rows, d), F32).at[:batch].set(c).at[batch].set(c_ctx)
    mod_all = _modulation(cc, w_mod, b_mod).reshape(depth, mod_rows, 6, d)

    rope = (*_rope_tables(seq, DIFF_DQK, GROUP_W // DIFF_DQK, ROW_TILE),
            *_rope_tables(seq, RET_DH, GROUP_W // RET_DH, ROW_TILE))
    dft_lat = _dft_tables(seq)
    dft_ctx = _dft_tables(n_ctx)
    seg = jnp.asarray(np.kron(np.eye(N_HEADS, dtype=np.float32), np.ones((RET_DH, RET_DH), np.float32)))

    for l in range(depth):
        need_ctx = l < depth - 1
        lam_init = 0.8 - 0.6 * math.exp(-0.3 * l)
        mod = mod_all[l]
        (aq, ak, av, bu, cq, ck, cv, cg, dq, dk, dv, dr, dl) = _in_projection(
            xs, mod, norm_g[l, 0].reshape(1, d), _widen_w_in(w_in[l]), rope, geo)

        sub_g = diff_subln_g[l].reshape(1, DIFF_DV)
        at = _diff_attention(diff_lambda[l], sub_g, aq, ak, av, geo, lam_init)
        if need_ctx:
            at = _diff_attention(diff_lambda[l], sub_g, aq, ak, av, geo, lam_init, prev=at)

        def hyena(n, row_blk0, tables, out_blk0, prev):
            e, dd, nyqf = _hyena_filter(n, hy_w1[l], hy_b1[l], hy_w2[l], hy_b2[l], hy_w3[l], hy_freq[l])
            zb, x0c, zf, nyqz = _hyena_gate(bu, hy_short_w[l], hy_short_b[l], n, batch, row_blk0)
            return _hyena_conv(tables, zb, e, dd, x0c, zf, nyqz, nyqf, hy_bias[l], n, batch, nt, out_blk0, prev)

        hy = hyena(seq, 0, dft_lat, 0, None)
        if need_ctx:
            hy = hyena(n_ctx, n_lat_rows // n_ctx, dft_ctx, n_lat_rows // min(512, n_ctx), hy)

        dec = jnp.repeat(ret_decay[l], RET_DH, axis=-1).reshape(2, 1, N_HEADS * RET_DH)
        rf, rb = _retention(cq, ck, cv, dec, geo)

        gw = jnp.zeros((2, LANES, N_HEADS * GLA_DK), F32)
        gw = gw.at[0, :GLA_RANK].set(gla_gate_w[l, 0]).at[1, GLA_RANK:2 * GLA_RANK].set(gla_gate_w[l, 1])
        gbias = gla_gate_b[l].reshape(2, 1, N_HEADS * GLA_DK)
        gf, gb = _gla(dq, dk, dv, dl, gw, gbias, geo)

        n_tiles = nt // ROW_TILE if need_ctx else n_lat_rows // ROW_TILE
        rw = jnp.pad(router_w[l], ((0, 0), (0, LANES - N_EXPERTS)))
        rbias = jnp.pad(router_b[l], (0, LANES - N_EXPERTS), constant_values=-jnp.inf).reshape(1, LANES)
        x1, h2, topi, topw, rank, cnt = _mix_out(
            n_tiles, xs, mod, at, hy, rf, rb, cg, gf, gb, dr, seg,
            ret_norm_g[l].reshape(1, GROUP_W), jnp.tile(gla_norm_g[l], N_HEADS).reshape(1, GROUP_W),
            w_out[l].astype(BF16), norm_g[l, 1].reshape(1, d), norm_g[l, 2].reshape(1, d), rw, rbias, geo)

        n_tok = n_tiles * ROW_TILE
        dest, tile_e, n_active, p_rows, bounds = _route_plan(
            topi[:n_tok, :TOP_K], rank[:n_tok, :TOP_K], cnt[0, :N_EXPERTS].astype(jnp.int32), MOE_TILE)
        dispatched = _dispatch(h2, dest, bounds, p_rows, n_tok)
        ys = _expert_ffn(tile_e, n_active, dispatched, exp_w1, exp_b1[l], exp_w2, exp_b2[l], l)
        xs = _combine_residual(n_tiles, x1, ys, dest, topw, mod, norm_g[l, 3].reshape(1, d), geo)

    return xs[:n_lat_rows].reshape(batch, seq, d)
```

```python
import functools
import math

import numpy as np
import jax
import jax.numpy as jnp
from jax import lax
from jax.experimental import pallas as pl
from jax.experimental.pallas import tpu as pltpu

F32 = jnp.float32
BF16 = jnp.bfloat16
HIGHEST = lax.Precision.HIGHEST

D_MODEL = 1024
GRID_W = 64
GROUP_W = 256
N_HEADS = 4
DIFF_DQK = 32
DIFF_DV = 64
ROPE_BASE = 10000.0
HY_CH = 256
HY_BANDS = 16
HY_FFN = 64
HY_FAST_DECAY_PCT = 0.3
HY_SLOW_DECAY_PCT = 1.5
HY_DECAY_TARGET = 1e-2
RET_DH = 64
GLA_DK = 32
GLA_DV = 64
GLA_RANK = 16
GLA_GATE_NORM = 16.0
N_EXPERTS = 32
TOP_K = 4
D_EXPERT = 1024
SWIGLU_LIMIT = 7.0
SWIGLU_ALPHA = 1.702
EPS = 1e-6

LANES = 128
ROW_TILE = 256
REC_CHUNK = 128
RET_CHUNK = 256
REC_SUB = 32
EXP_CLAMP = 80.0
MOE_TILE = 256
ATT_KEY_CHUNK = 512
ATT_Q_TILE = 512
DFT_TABLE_ROWS = 128
VMEM_LIMIT = 52 * 1024 * 1024

_A_Q, _A_QR, _A_K, _A_KR, _A_V = 0, 256, 512, 768, 1024
_B_U = 1280
_C_Q, _C_QR, _C_K, _C_KR, _C_V, _C_G = 2048, 2304, 2560, 2816, 3072, 3328
_D_Q, _D_K, _D_V, _D_R, _D_L = 3584, 3712, 3840, 4096, 4352
_IN_COLS = 4480


def _cparams(sem):
    return pltpu.CompilerParams(dimension_semantics=sem, vmem_limit_bytes=VMEM_LIMIT)


def _sigmoid(x):
    return 1.0 / (1.0 + jnp.exp(-x))


def _rms(x):
    return x * lax.rsqrt(jnp.mean(x * x, axis=-1, keepdims=True) + EPS)


def _iota(shape, axis):
    return lax.broadcasted_iota(jnp.int32, shape, axis)


def _dot(a, b):
    return jnp.dot(a, b, preferred_element_type=F32)


def _dot_exact(a, b):
    return jnp.dot(a, b, preferred_element_type=F32, precision=HIGHEST)


def _dot_nt(a, b):
    return lax.dot_general(a, b, (((1,), (1,)), ((), ())), preferred_element_type=F32)


def _dot_tn(a, b):
    return lax.dot_general(a, b, (((0,), (0,)), ((), ())), preferred_element_type=F32)


def _mod_kernel(c_ref, w_ref, b_ref, o_ref):
    c = c_ref[...]
    o_ref[...] = _dot_exact(c * _sigmoid(c), w_ref[...]) + b_ref[...]


def _modulation(cc, w_mod, b_mod):
    depth, d, n = w_mod.shape
    tn = n // 4
    rows = cc.shape[0]
    return pl.pallas_call(
        _mod_kernel,
        grid=(depth, n // tn),
        in_specs=[pl.BlockSpec((rows, d), lambda l, j: (0, 0)),
                  pl.BlockSpec((None, d, tn), lambda l, j: (l, 0, j)),
                  pl.BlockSpec((None, 1, tn), lambda l, j: (l, 0, j))],
        out_specs=pl.BlockSpec((None, rows, tn), lambda l, j: (l, 0, j)),
        out_shape=jax.ShapeDtypeStruct((depth, rows, n), F32),
        compiler_params=_cparams(("arbitrary", "arbitrary")),
        name="modulation",
    )(cc, w_mod, b_mod.reshape(depth, 1, n))


def _inproj_kernel(x_ref, mod_ref, g_ref, w_ref, cosa_ref, sina_ref, cosc_ref, sinc_ref,
                   aq_ref, ak_ref, av_ref, bu_ref, cq_ref, ck_ref, cv_ref, cg_ref,
                   dq_ref, dk_ref, dv_ref, dr_ref, dl_ref):
    xn = _rms(x_ref[...]) * g_ref[...]
    h = (xn * (1.0 + mod_ref[1:2, :]) + mod_ref[0:1, :]).astype(BF16)

    def proj(a, width):
        return _dot(h, w_ref[:, a:a + width])

    def roped(a, a_rot, cos_ref, sin_ref):
        return proj(a, GROUP_W) * cos_ref[...] + proj(a_rot, GROUP_W) * sin_ref[...]

    aq = roped(_A_Q, _A_QR, cosa_ref, sina_ref) * (DIFF_DQK ** -0.5 * math.log2(math.e))
    ak = roped(_A_K, _A_KR, cosa_ref, sina_ref)
    av = proj(_A_V, GROUP_W)
    ones_col = jnp.where(lax.broadcasted_iota(jnp.int32, (av.shape[0], LANES - DIFF_DV), 1) == 0, 1.0, 0.0)
    for hd in range(N_HEADS):
        sl = slice(hd * DIFF_DV, (hd + 1) * DIFF_DV)
        aq_ref[hd] = aq[:, sl].astype(BF16)
        ak_ref[hd] = ak[:, sl].astype(BF16)
        av_ref[hd] = jnp.concatenate([av[:, sl], ones_col], axis=-1).astype(BF16)
    bu_ref[...] = proj(_B_U, 3 * HY_CH)
    cq_ref[...] = roped(_C_Q, _C_QR, cosc_ref, sinc_ref)
    ck_ref[...] = roped(_C_K, _C_KR, cosc_ref, sinc_ref)
    cv_ref[...] = proj(_C_V, GROUP_W)
    cg_ref[...] = proj(_C_G, GROUP_W)
    dq_ref[...] = proj(_D_Q, LANES)
    dk_ref[...] = proj(_D_K, LANES)
    dv_ref[...] = proj(_D_V, GROUP_W)
    dr_ref[...] = proj(_D_R, GROUP_W)
    dl_ref[...] = proj(_D_L, LANES)


def _in_projection(x, mod, g, w_wide, rope, geo):
    nt, d = x.shape
    tm = ROW_TILE
    lat_tiles, per_batch, batch = geo["lat_tiles"], geo["tiles_per_batch"], geo["batch"]

    def mod_map(i):
        return (jnp.where(i < lat_tiles, i // per_batch, batch), 0, 0)

    def rope_map(i):
        return (jnp.where(i < lat_tiles, i % per_batch, per_batch), 0)

    row = lambda i: (i, 0)
    head = lambda i: (0, i, 0)
    const = lambda i: (0, 0)
    f32_out = lambda w: jax.ShapeDtypeStruct((nt, w), F32)
    head_out = lambda w: jax.ShapeDtypeStruct((N_HEADS, nt, w), BF16)
    head_spec = lambda w: pl.BlockSpec((N_HEADS, tm, w), head)
    widths = [3 * HY_CH, GROUP_W, GROUP_W, GROUP_W, GROUP_W, LANES, LANES, GROUP_W, GROUP_W, LANES]
    return pl.pallas_call(
        _inproj_kernel,
        grid=(nt // tm,),
        in_specs=[pl.BlockSpec((tm, d), row),
                  pl.BlockSpec((None, 6, d), mod_map),
                  pl.BlockSpec((1, d), const),
                  pl.BlockSpec((d, _IN_COLS), const, pipeline_mode=pl.Buffered(1)),
                  pl.BlockSpec((tm, GROUP_W), rope_map), pl.BlockSpec((tm, GROUP_W), rope_map),
                  pl.BlockSpec((tm, GROUP_W), rope_map), pl.BlockSpec((tm, GROUP_W), rope_map)],
        out_specs=[head_spec(DIFF_DV), head_spec(DIFF_DV), head_spec(LANES)]
                  + [pl.BlockSpec((tm, w), row) for w in widths],
        out_shape=[head_out(DIFF_DV), head_out(DIFF_DV), head_out(LANES)] + [f32_out(w) for w in widths],
        compiler_params=_cparams(("arbitrary",)),
        name="in_projection",
    )(x, mod, g, w_wide, *rope)


def _attn_kernel(lam_ref, g_ref, q_ref, *rest, lam_init, has_lat):
    if has_lat:
        kl_ref, vl_ref, kc_ref, vc_ref, o_ref = rest
        keys = [(kl_ref, vl_ref), (kc_ref, vc_ref)]
    else:
        kc_ref, vc_ref, o_ref = rest
        keys = [(kc_ref, vc_ref)]
    lp = lam_ref[...]
    lam = (jnp.exp(jnp.sum(lp[0:1] * lp[1:2], axis=-1, keepdims=True))
           - jnp.exp(jnp.sum(lp[2:3] * lp[3:4], axis=-1, keepdims=True)) + lam_init)
    q = q_ref[...]
    tq = q.shape[0]
    lane = lax.broadcasted_iota(jnp.int32, q.shape, 1)
    chunks = []
    for k_ref, v_ref in keys:
        size = min(ATT_KEY_CHUNK, k_ref.shape[0])
        chunks += [(k_ref, v_ref, s0, size) for s0 in range(0, k_ref.shape[0], size)]

    def lane_groups(t):
        return [t[:, c0:c0 + LANES] for c0 in range(0, t.shape[1], LANES)]

    qm = [jnp.where((lane >= m * DIFF_DQK) & (lane < (m + 1) * DIFF_DQK), q, jnp.zeros_like(q)) for m in range(2)]

    def score(m, j):
        k_ref, _, s0, size = chunks[j]
        return _dot_nt(qm[m], k_ref[s0:s0 + size, :])

    def row_max(scores):
        wide = functools.reduce(jnp.maximum, [g for t in scores for g in lane_groups(t)])
        return jnp.max(wide, axis=-1, keepdims=True)

    def weighted(m, j, s, mx, acc):
        _, v_ref, s0, size = chunks[j]
        return acc + _dot(jnp.exp2(s - mx).astype(BF16), v_ref[s0:s0 + size, :])

    n = len(chunks)
    s1 = [score(0, j) for j in range(n)]
    mx1 = row_max(s1)
    s2 = []
    acc1 = jnp.zeros((tq, LANES), F32)
    for j in range(n):
        s2.append(score(1, j))
        acc1 = weighted(0, j, s1[j], mx1, acc1)
    mx2 = row_max(s2)
    acc2 = jnp.zeros((tq, LANES), F32)
    for j in range(n):
        acc2 = weighted(1, j, s2[j], mx2, acc2)
    o = (acc1[:, :DIFF_DV] * (1.0 / acc1[:, DIFF_DV:DIFF_DV + 1])
         - lam * (acc2[:, :DIFF_DV] * (1.0 / acc2[:, DIFF_DV:DIFF_DV + 1])))
    o_ref[...] = _rms(o) * g_ref[...] * (1.0 - lam_init)


def _diff_attention(lam_p, subln_g, aq, ak, av, geo, lam_init, prev=None):
    batch, seq, ctx = geo["batch"], geo["seq"], geo["ctx"]
    nt = aq.shape[1]
    has_lat = prev is None
    tq = ATT_Q_TILE if has_lat else ROW_TILE
    n_q = (seq if has_lat else ctx) // tq
    q_off = 0 if has_lat else (batch * seq) // tq
    ctx_blk0 = (batch * seq) // ctx

    qmap = lambda b, h, i: (h, q_off + b * n_q + i, 0)
    lat_map = lambda b, h, i: (h, b, 0)
    ctx_map = lambda b, h, i: (h, ctx_blk0 + b, 0)
    const = lambda b, h, i: (0, 0)
    in_specs = [pl.BlockSpec((4, DIFF_DQK), const), pl.BlockSpec((1, DIFF_DV), const),
                pl.BlockSpec((None, tq, DIFF_DV), qmap)]
    args = [lam_p, subln_g, aq]
    if has_lat:
        in_specs += [pl.BlockSpec((None, seq, DIFF_DV), lat_map), pl.BlockSpec((None, seq, LANES), lat_map)]
        args += [ak, av]
    in_specs += [pl.BlockSpec((None, ctx, DIFF_DV), ctx_map), pl.BlockSpec((None, ctx, LANES), ctx_map)]
    args += [ak, av]
    aliases = {}
    if not has_lat:
        in_specs.append(pl.BlockSpec(memory_space=pl.ANY))
        args.append(prev)
        aliases = {len(args) - 1: 0}
    kern = functools.partial(_attn_kernel, lam_init=lam_init, has_lat=has_lat)
    if not has_lat:
        kern = _drop_last_input(kern, n_in=len(args))
    return pl.pallas_call(
        kern,
        grid=(batch, N_HEADS, n_q),
        in_specs=in_specs,
        out_specs=pl.BlockSpec((None, tq, DIFF_DV), qmap),
        out_shape=jax.ShapeDtypeStruct((N_HEADS, nt, DIFF_DV), F32),
        input_output_aliases=aliases,
        compiler_params=_cparams(("arbitrary", "arbitrary", "arbitrary")),
        name="diff_attention" if has_lat else "diff_attention_ctx",
    )(*args)


def _drop_last_input(kern, n_in):
    def wrapped(*refs):
        return kern(*refs[:n_in - 1], *refs[n_in:])
    return wrapped


def _dft_table_kernel(cx_ref, sx_ref, cy_ref, sy_ref, c_ref, s_ref, *, n, span):
    t = _iota((LANES, n), 1)
    row = _iota((LANES, n), 0)
    pick_a = jnp.where(t // span == row, 1.0, 0.0).astype(BF16)
    pick_b = jnp.where(t % span == row, 1.0, 0.0).astype(BF16)

    def widen(ref, pick):
        x = ref[...]
        hi = x.astype(BF16)
        lo = (x - hi.astype(F32)).astype(BF16)
        return _dot(hi, pick) + _dot(lo, pick)

    cx, sx = widen(cx_ref, pick_a), widen(sx_ref, pick_a)
    cy, sy = widen(cy_ref, pick_b), widen(sy_ref, pick_b)
    c_ref[...] = (cx * cy - sx * sy).astype(BF16)
    s_ref[...] = (sx * cy + cx * sy).astype(BF16)


def _dft_tables(n):
    span = 64
    assert n % span == 0 and n // span <= LANES
    k = jnp.arange(n, dtype=jnp.int32)[:, None]
    j = jnp.arange(LANES, dtype=jnp.int32)[None, :]

    def factor(step, count):
        ang = ((k * (j * step)) % (2 * n)).astype(F32) * (math.pi / n)
        live = j < count
        return jnp.where(live, jnp.cos(ang), 0.0), jnp.where(live, jnp.sin(ang), 0.0)

    cx, sx = factor(span, n // span)
    cy, sy = factor(1, span)
    tk = min(DFT_TABLE_ROWS, n)
    small = pl.BlockSpec((tk, LANES), lambda i: (i, 0))
    big = pl.BlockSpec((tk, n), lambda i: (i, 0))
    return pl.pallas_call(
        functools.partial(_dft_table_kernel, n=n, span=span),
        grid=(n // tk,),
        in_specs=[small] * 4,
        out_specs=[big, big],
        out_shape=[jax.ShapeDtypeStruct((n, n), BF16)] * 2,
        compiler_params=_cparams(("arbitrary",)),
        name="dft_tables",
    )(cx, sx, cy, sy)


def _hy_filter_kernel(w1t_ref, w1c_ref, w1s_ref, b1_ref, w2_ref, b2_ref, w3_ref, fr_ref,
                      bands_ref, deltas_ref, e_ref, d_ref, nyq_ref, *, n):
    pos_i = lax.broadcasted_iota(jnp.int32, (n, 1), 0)
    pos = pos_i.astype(F32)
    t = pos / (n - 1)
    ang = ((2.0 * math.pi) * pos / n) * bands_ref[...]
    pre = t * w1t_ref[...] + _dot_exact(jnp.cos(ang), w1c_ref[...]) - _dot_exact(jnp.sin(ang), w1s_ref[...])
    hdn = jnp.sin(fr_ref[0:1, :] * (pre + b1_ref[...]))
    hdn = jnp.sin(fr_ref[1:2, :] * (_dot_exact(hdn, w2_ref[...]) + b2_ref[...]))
    raw = _dot_exact(hdn, w3_ref[...])
    window = jnp.exp(-t * deltas_ref[...])
    hf = raw[:, :HY_CH] * window
    hb = jnp.where(pos_i > 0, raw[:, HY_CH:] * window, 0.0)
    inv = 1.0 / (jnp.sum(jnp.abs(hf), axis=0, keepdims=True) + jnp.sum(jnp.abs(hb), axis=0, keepdims=True))
    e = (hf + hb) * inv
    e_ref[...] = e.astype(BF16)
    d_ref[...] = ((hb - hf) * inv).astype(BF16)
    sign = (1 - 2 * (pos_i & 1)).astype(F32)
    nyq_ref[...] = jnp.sum(e * sign, axis=0, keepdims=True)


def _hyena_filter(n, w1, b1, w2, b2, w3, freq):
    bands = jnp.linspace(1e-4, HY_BANDS - 1, HY_BANDS, dtype=F32).reshape(1, HY_BANDS)
    max_decay = math.log(HY_DECAY_TARGET) / HY_FAST_DECAY_PCT
    min_decay = math.log(HY_DECAY_TARGET) / HY_SLOW_DECAY_PCT
    deltas = jnp.abs(jnp.linspace(min_decay, max_decay, HY_CH, dtype=F32)).reshape(1, HY_CH)
    args = [w1[0:1], w1[1:1 + HY_BANDS], w1[1 + HY_BANDS:], b1.reshape(1, HY_FFN), w2, b2.reshape(1, HY_FFN),
            w3, freq, bands, deltas]
    return pl.pallas_call(
        functools.partial(_hy_filter_kernel, n=n),
        out_shape=[jax.ShapeDtypeStruct((n, HY_CH), BF16), jax.ShapeDtypeStruct((n, HY_CH), BF16),
                   jax.ShapeDtypeStruct((1, HY_CH), F32)],
        compiler_params=pltpu.CompilerParams(vmem_limit_bytes=VMEM_LIMIT),
        name="hyena_filter",
    )(*args)


def _hy_gate_kernel(x0_ref, x1_ref, v_ref, w0_ref, w1_ref, wv_ref, b0_ref, b1_ref, bv_ref,
                    zb_ref, x0c_ref, zf_ref, nyq_ref, *, n):
    row = lax.broadcasted_iota(jnp.int32, (n, 1), 0)

    def conv(u_ref, w_ref, b_ref):
        u = u_ref[...]
        up = jnp.where(row > 0, pltpu.roll(u, 1, 0), 0.0)
        dn = jnp.where(row < n - 1, pltpu.roll(u, n - 1, 0), 0.0)
        return up * w_ref[0:1, :] + u * w_ref[1:2, :] + dn * w_ref[2:3, :] + b_ref[...]

    z = conv(x1_ref, w1_ref, b1_ref) * conv(v_ref, wv_ref, bv_ref)
    x0c_ref[...] = conv(x0_ref, w0_ref, b0_ref)
    zf_ref[...] = z
    zb_ref[...] = z.astype(BF16)
    sign = (1 - 2 * (row & 1)).astype(F32)
    nyq_ref[...] = jnp.sum(z * sign, axis=0, keepdims=True)


def _hyena_gate(bu, short_w, short_b, n, batch, row_blk0):
    halves = HY_CH // LANES
    sb = short_b.reshape(1, 3 * HY_CH)
    seg = lambda part: pl.BlockSpec((n, LANES), lambda b, j: (row_blk0 + b, part * halves + j))
    wsp = lambda part: pl.BlockSpec((3, LANES), lambda b, j: (0, part * halves + j))
    bsp = lambda part: pl.BlockSpec((1, LANES), lambda b, j: (0, part * halves + j))
    return pl.pallas_call(
        functools.partial(_hy_gate_kernel, n=n),
        grid=(batch, halves),
        in_specs=[seg(0), seg(1), seg(2), wsp(0), wsp(1), wsp(2), bsp(0), bsp(1), bsp(2)],
        out_specs=[pl.BlockSpec((n, LANES), lambda b, j: (0, b * halves + j)),
                   pl.BlockSpec((n, LANES), lambda b, j: (b, j)),
                   pl.BlockSpec((n, LANES), lambda b, j: (b, j)),
                   pl.BlockSpec((None, 1, LANES), lambda b, j: (b, 0, j))],
        out_shape=[jax.ShapeDtypeStruct((n, batch * HY_CH), BF16),
                   jax.ShapeDtypeStruct((batch * n, HY_CH), F32),
                   jax.ShapeDtypeStruct((batch * n, HY_CH), F32),
                   jax.ShapeDtypeStruct((batch, 1, HY_CH), F32)],
        compiler_params=_cparams(("arbitrary", "arbitrary")),
        name="hyena_gate",
    )(bu, bu, bu, short_w, short_w, short_w, sb, sb, sb)


def _hy_spectrum_kernel(c_ref, s_ref, z_ref, e_ref, d_ref, yr_ref, yi_ref, *, n, tk, batch):
    c = c_ref[...]
    s = s_ref[...]
    zr = _dot(c, z_ref[...])
    zs = _dot(s, z_ref[...])
    fr = _dot(c, e_ref[...])
    fi = _dot(s, d_ref[...])
    k = pl.program_id(0) * tk + lax.broadcasted_iota(jnp.int32, (tk, 1), 0)
    wk = jnp.where(k == 0, 1.0, 2.0) * (1.0 / (2 * n))
    for b in range(batch):
        sl = slice(b * HY_CH, (b + 1) * HY_CH)
        yr = zr[:, sl] * fr + zs[:, sl] * fi
        yi = zr[:, sl] * fi - zs[:, sl] * fr
        yr_ref[:, sl] = (yr * wk).astype(BF16)
        yi_ref[:, sl] = (-(yi * wk)).astype(BF16)


def _hy_inverse_kernel(c_ref, s_ref, yr_ref, yi_ref, x0c_ref, zf_ref, nyqz_ref, nyqf_ref, bias_ref,
                       *rest, n, tt, batch):
    o_ref, y_scr = rest[-2], rest[-1]
    b = pl.program_id(1)

    @pl.when(b == 0)
    def _():
        y = _dot(c_ref[...], yr_ref[...]) + _dot(s_ref[...], yi_ref[...])
        for bb in range(batch):
            y_scr[bb] = y[:, bb * HY_CH:(bb + 1) * HY_CH]

    t = pl.program_id(0) * tt + lax.broadcasted_iota(jnp.int32, (tt, 1), 0)
    sign = (1 - 2 * (t & 1)).astype(F32)
    nyq = nyqz_ref[...] * nyqf_ref[...] * (1.0 / (2 * n))
    zf = zf_ref[...]
    o_ref[...] = x0c_ref[...] * (y_scr[b] + sign * nyq + bias_ref[...] * zf)


def _hyena_conv(cs, zb, e, d, x0c, zf, nyqz, nyqf, bias, n, batch, nt, out_blk0, prev=None):
    c_tab, s_tab = cs
    bw = batch * HY_CH
    tk = min(512, n)
    whole = lambda shape: pl.BlockSpec(shape, lambda *_: (0,) * len(shape), pipeline_mode=pl.Buffered(1))
    yr, yi = pl.pallas_call(
        functools.partial(_hy_spectrum_kernel, n=n, tk=tk, batch=batch),
        grid=(n // tk,),
        in_specs=[pl.BlockSpec((tk, n), lambda i: (i, 0)), pl.BlockSpec((tk, n), lambda i: (i, 0)),
                  whole((n, bw)), whole((n, HY_CH)), whole((n, HY_CH))],
        out_specs=[pl.BlockSpec((tk, bw), lambda i: (i, 0))] * 2,
        out_shape=[jax.ShapeDtypeStruct((n, bw), BF16)] * 2,
        compiler_params=_cparams(("arbitrary",)),
        name="hyena_spectrum",
    )(c_tab, s_tab, zb, e, d)

    tt = min(512, n)
    n_t = n // tt
    seg = lambda i, b: (b * n_t + i, 0)
    in_specs = [pl.BlockSpec((tt, n), lambda i, b: (i, 0)), pl.BlockSpec((tt, n), lambda i, b: (i, 0)),
                whole((n, bw)), whole((n, bw)),
                pl.BlockSpec((tt, HY_CH), seg), pl.BlockSpec((tt, HY_CH), seg),
                pl.BlockSpec((None, 1, HY_CH), lambda i, b: (b, 0, 0)),
                pl.BlockSpec((1, HY_CH), lambda i, b: (0, 0)), pl.BlockSpec((1, HY_CH), lambda i, b: (0, 0))]
    args = [c_tab, s_tab, yr, yi, x0c, zf, nyqz, nyqf, bias.reshape(1, HY_CH)]
    aliases = {}
    if prev is not None:
        in_specs.append(pl.BlockSpec(memory_space=pl.ANY))
        args.append(prev)
        aliases = {len(args) - 1: 0}
    return pl.pallas_call(
        functools.partial(_hy_inverse_kernel, n=n, tt=tt, batch=batch),
        grid=(n_t, batch),
        in_specs=in_specs,
        out_specs=pl.BlockSpec((tt, HY_CH), lambda i, b: (out_blk0 + b * n_t + i, 0)),
        out_shape=jax.ShapeDtypeStruct((nt, HY_CH), F32),
        scratch_shapes=[pltpu.VMEM((batch, tt, HY_CH), F32)],
        input_output_aliases=aliases,
        compiler_params=_cparams(("arbitrary", "arbitrary")),
        name="hyena_inverse" if prev is None else "hyena_inverse_ctx",
    )(*args)


def _scan_maps(geo, cc):
    batch, seq, ctx = geo["batch"], geo["seq"], geo["ctx"]
    n_cc, n_lc = ctx // cc, seq // cc
    ctx0 = (batch * seq) // cc

    def fwd(b, i):
        return (jnp.where(i < n_cc, ctx0 + b * n_cc + i, b * n_lc + i - n_cc), 0)

    def bwd(b, i):
        return (jnp.where(i < n_cc, ctx0 + b * n_cc + (n_cc - 1 - i), b * n_lc + (n_lc - 1 - (i - n_cc))), 0)

    return fwd, bwd, n_cc + n_lc


def _ret_kernel(qf_ref, kf_ref, vf_ref, qb_ref, kb_ref, vb_ref, dec_ref, of_ref, ob_ref, st_ref):
    @pl.when(pl.program_id(1) == 0)
    def _():
        st_ref[...] = jnp.zeros_like(st_ref)

    cc = qf_ref.shape[0]
    r_i, c_i = _iota((cc, cc), 0), _iota((cc, cc), 1)
    pos = _iota((cc, 1), 0).astype(F32)
    for d, (q_ref, k_ref, v_ref, o_ref) in enumerate(((qf_ref, kf_ref, vf_ref, of_ref),
                                                      (qb_ref, kb_ref, vb_ref, ob_ref))):
        reverse = d == 1
        lg = -jnp.exp(dec_ref[d])
        steps_in = (cc - pos) if reverse else (pos + 1.0)
        steps_out = pos if reverse else (cc - 1.0 - pos)
        q = q_ref[...]
        k = k_ref[...] * (RET_DH ** -0.5)
        qd = (q * jnp.exp(steps_in * lg)).astype(BF16)
        kd = (k * jnp.exp(steps_out * lg)).astype(BF16)
        qb, kb, vb = q.astype(BF16), k.astype(BF16), v_ref[...].astype(BF16)
        gain = jnp.exp(cc * lg)
        dist = (c_i - r_i) if reverse else (r_i - c_i)
        keep = dist >= 0
        dist_f = jnp.where(keep, dist, 0).astype(F32)
        for h in range(N_HEADS):
            hs = slice(h * RET_DH, (h + 1) * RET_DH)
            decay = jnp.where(keep, jnp.exp(dist_f * lg[:, h * RET_DH:h * RET_DH + 1]), 0.0)
            sc = (_dot_nt(qb[:, hs], kb[:, hs]) * decay).astype(BF16)
            st = st_ref[d, h]
            o_ref[:, hs] = _dot(sc, vb[:, hs]) + _dot_nt(qd[:, hs], st.astype(BF16))
            st_ref[d, h] = st * gain[:, hs] + _dot_tn(vb[:, hs], kd[:, hs])


def _retention(q, k, v, dec, geo):
    nt, w = q.shape
    cc = RET_CHUNK
    fwd, bwd, steps = _scan_maps(geo, cc)
    blk = lambda m: pl.BlockSpec((cc, w), m)
    return pl.pallas_call(
        _ret_kernel,
        grid=(geo["batch"], steps),
        in_specs=[blk(fwd), blk(fwd), blk(fwd), blk(bwd), blk(bwd), blk(bwd),
                  pl.BlockSpec((2, 1, w), lambda b, i: (0, 0, 0))],
        out_specs=[blk(fwd), blk(bwd)],
        out_shape=[jax.ShapeDtypeStruct((nt, w), F32)] * 2,
        scratch_shapes=[pltpu.VMEM((2, N_HEADS, RET_DH, RET_DH), F32)],
        compiler_params=_cparams(("arbitrary", "arbitrary")),
        name="recurrence_ret",
    )(q, k, v, q, k, v, dec)


def _gla_direction(q, k, v, la, st_ref, o_ref, *, reverse):
    cc, wk = q.shape
    wv = v.shape[1]
    r_i, c_i = _iota((cc, cc), 0), _iota((cc, cc), 1)
    incl = (c_i >= r_i) if reverse else (c_i <= r_i)
    cum = _dot_exact(jnp.where(incl, 1.0, 0.0), la)
    cum_end = cum[0:1] if reverse else cum[cc - 1:cc]

    qd = (q * jnp.exp(cum)).astype(BF16)
    kd = (k * jnp.exp(cum_end - cum)).astype(BF16)
    vb = v.astype(BF16)
    st = st_ref[...]
    inter = _dot_nt(qd, st.astype(BF16))
    same_head = (_iota((wv, wk), 0) // GLA_DV) == (_iota((wv, wk), 1) // GLA_DK)
    st_ref[...] = st * jnp.exp(cum_end) + jnp.where(same_head, _dot_tn(vb, kd), 0.0)

    sub = REC_SUB
    hs = N_HEADS * sub
    q_own = (_iota((hs, wk), 0) // sub) == (_iota((hs, wk), 1) // GLA_DK)
    for j in range(cc // sub):
        r0, r1 = j * sub, (j + 1) * sub
        ka, kb = (r0, cc) if reverse else (0, r1)
        mid = cum[r0 + sub // 2:r0 + sub // 2 + 1]
        qj = q[r0:r1] * jnp.exp(jnp.minimum(cum[r0:r1] - mid, EXP_CLAMP))
        kj = (k[ka:kb] * jnp.exp(jnp.minimum(mid - cum[ka:kb], EXP_CLAMP))).astype(BF16)
        q_stack = jnp.where(q_own, jnp.concatenate([qj] * N_HEADS, axis=0), 0.0).astype(BF16)
        rows = r0 + (_iota((hs, kb - ka), 0) % sub)
        cols = ka + _iota((hs, kb - ka), 1)
        keep = (cols >= rows) if reverse else (cols <= rows)
        sc = jnp.where(keep, _dot_nt(q_stack, kj), 0.0).astype(BF16)
        full = _dot(sc, vb[ka:kb])
        lane_head = _iota((sub, wv), 1) // GLA_DV
        oj = inter[r0:r1]
        for h in range(N_HEADS):
            oj = oj + jnp.where(lane_head == h, full[h * sub:(h + 1) * sub], 0.0)
        o_ref[r0:r1, :] = oj


def _gla_kernel(*refs, batch):
    per_batch = refs[:8 * batch]
    gw_ref, gb_ref, of_ref, ob_ref, st_ref = refs[8 * batch:]

    @pl.when(pl.program_id(0) == 0)
    def _():
        st_ref[...] = jnp.zeros_like(st_ref)

    for b in range(batch):
        qf_ref, kf_ref, vf_ref, lf_ref, qb_ref, kb_ref, vb_ref, lb_ref = per_batch[8 * b:8 * b + 8]
        for d, (q_ref, k_ref, v_ref, l_ref, o_ref) in enumerate(((qf_ref, kf_ref, vf_ref, lf_ref, of_ref),
                                                                 (qb_ref, kb_ref, vb_ref, lb_ref, ob_ref))):
            logit = _dot_exact(l_ref[...], gw_ref[d]) + gb_ref[d]
            la = (jnp.minimum(logit, 0.0) - jnp.log(1.0 + jnp.exp(-jnp.abs(logit)))) * (1.0 / GLA_GATE_NORM)
            _gla_direction(q_ref[...] * (GLA_DK ** -0.5), k_ref[...], v_ref[...], la, st_ref.at[b, d],
                           o_ref.at[b], reverse=(d == 1))


def _gla(q, k, v, glr, gw, gb, geo):
    batch, seq, ctx = geo["batch"], geo["seq"], geo["ctx"]
    wq, wv = q.shape[1], v.shape[1]
    cc = REC_CHUNK
    n_cc, n_lc = ctx // cc, seq // cc
    ctx0 = (batch * seq) // cc
    in_specs, args = [], []
    for b in range(batch):
        fwd = lambda i, b=b: (jnp.where(i < n_cc, ctx0 + b * n_cc + i, b * n_lc + i - n_cc), 0)
        bwd = lambda i, b=b: (jnp.where(i < n_cc, ctx0 + b * n_cc + (n_cc - 1 - i),
                                        b * n_lc + (n_lc - 1 - (i - n_cc))), 0)
        for m in (fwd, bwd):
            in_specs += [pl.BlockSpec((cc, wq), m), pl.BlockSpec((cc, wq), m), pl.BlockSpec((cc, wv), m),
                         pl.BlockSpec((cc, LANES), m)]
            args += [q, k, v, glr]
    in_specs += [pl.BlockSpec((2, LANES, wq), lambda i: (0, 0, 0)), pl.BlockSpec((2, 1, wq), lambda i: (0, 0, 0))]
    args += [gw, gb]
    steps = n_cc + n_lc
    out_blk = lambda m: pl.BlockSpec((batch, cc, wv), m)
    return pl.pallas_call(
        functools.partial(_gla_kernel, batch=batch),
        grid=(steps,),
        in_specs=in_specs,
        out_specs=[out_blk(lambda i: (0, i, 0)),
                   out_blk(lambda i: (0, jnp.where(i < n_cc, n_cc - 1 - i, steps - 1 - (i - n_cc)), 0))],
        out_shape=[jax.ShapeDtypeStruct((batch, ctx + seq, wv), F32)] * 2,
        scratch_shapes=[pltpu.VMEM((batch, 2, wv, wq), F32)],
        compiler_params=_cparams(("arbitrary",)),
        name="recurrence_gla",
    )(*args)


def _mixout_kernel(x_ref, mod_ref, at_ref, hy_ref, rf_ref, rb_ref, cg_ref, gf_ref, gb_ref, dr_ref,
                   seg_ref, rg_ref, gg_ref, wo_ref, g1_ref, g2_ref, rwh_ref, rwl_ref, rbias_ref,
                   x1_ref, h2_ref, ti_ref, tw_ref, rk_ref, cnt_ref):
    @pl.when(pl.program_id(0) == 0)
    def _():
        cnt_ref[...] = jnp.zeros_like(cnt_ref)

    def head_norm(o):
        ms = _dot((o * o).astype(BF16), seg_ref[...]) * (1.0 / RET_DH)
        return o * lax.rsqrt(ms + EPS)

    a = jnp.concatenate([at_ref[h] for h in range(N_HEADS)], axis=-1)
    cg = cg_ref[...]
    dr = dr_ref[...]
    rt = head_norm(rf_ref[...] + rb_ref[...]) * rg_ref[...] * (cg * _sigmoid(cg))
    gl = head_norm(gf_ref[...] + gb_ref[...]) * gg_ref[...] * (dr * _sigmoid(dr))
    cat = jnp.concatenate([a, hy_ref[...], rt, gl], axis=-1).astype(BF16)
    y = _dot(cat, wo_ref[...])
    x1 = x_ref[...] + mod_ref[2:3, :] * (_rms(y) * g1_ref[...])
    x1_ref[...] = x1
    h2 = _rms(x1) * g2_ref[...] * (1.0 + mod_ref[4:5, :]) + mod_ref[3:4, :]
    h2_ref[...] = h2

    h_hi = h2.astype(BF16)
    h_lo = (h2 - h_hi.astype(F32)).astype(BF16)
    vals = (_dot(h_hi, rwh_ref[...]) + _dot(h_lo, rwh_ref[...]) + _dot(h_hi, rwl_ref[...])
            + rbias_ref[...])
    tm = vals.shape[0]
    lane = lax.broadcasted_iota(jnp.int32, vals.shape, 1)
    idx_out = jnp.zeros(vals.shape, jnp.int32)
    w_out = jnp.zeros(vals.shape, F32)
    top = None
    den = 0.0
    picks = []
    for r in range(TOP_K):
        m = jnp.max(vals, axis=-1, keepdims=True)
        idx = jnp.min(jnp.where(vals == m, lane, LANES), axis=-1, keepdims=True)
        hit = lane == idx
        vals = jnp.where(hit, -jnp.inf, vals)
        top = m if top is None else top
        e = jnp.exp(m - top)
        den = den + e
        picks.append((idx, e, hit))
    inv = 1.0 / den
    for r, (idx, e, _) in enumerate(picks):
        idx_out = jnp.where(lane == r, idx, idx_out)
        w_out = jnp.where(lane == r, e * inv, w_out)
    ti_ref[...] = idx_out
    tw_ref[...] = w_out

    chosen = functools.reduce(jnp.logical_or, [hit for _, _, hit in picks])
    chosen_f = jnp.where(chosen, 1.0, 0.0)
    earlier = (lax.broadcasted_iota(jnp.int32, (tm, tm), 1) < lax.broadcasted_iota(jnp.int32, (tm, tm), 0))
    before = _dot(jnp.where(earlier, 1.0, 0.0).astype(BF16), chosen_f.astype(BF16)) + cnt_ref[...]
    rk_out = jnp.zeros(vals.shape, jnp.int32)
    for r, (_, _, hit) in enumerate(picks):
        rank = jnp.sum(jnp.where(hit, before, 0.0), axis=-1, keepdims=True)
        rk_out = jnp.where(lane == r, rank.astype(jnp.int32), rk_out)
    rk_ref[...] = rk_out
    cnt_ref[...] = cnt_ref[...] + jnp.sum(chosen_f, axis=0, keepdims=True)


def _mix_out(n_tiles, x, mod, at, hy, rf, rb, cg, gf, gb, dr, seg, rg, gg, wo, g1, g2, rw, rbias, geo):
    rw_hi = rw.astype(BF16)
    rw_lo = (rw - rw_hi.astype(F32)).astype(BF16)
    nt, d = x.shape
    tm = ROW_TILE
    lat_tiles, per_batch, batch = geo["lat_tiles"], geo["tiles_per_batch"], geo["batch"]

    def mod_map(i):
        return (jnp.where(i < lat_tiles, i // per_batch, batch), 0, 0)

    row = lambda i: (i, 0)
    const = lambda i: (0, 0)
    g_blk = pl.BlockSpec((tm, GROUP_W), row)
    ctx_tiles = geo["ctx"] // tm

    def scan_map(i):
        c = i - lat_tiles
        return (jnp.where(i < lat_tiles, i // per_batch, c // ctx_tiles),
                jnp.where(i < lat_tiles, ctx_tiles + i % per_batch, c % ctx_tiles), 0)

    s_blk = pl.BlockSpec((None, tm, GROUP_W), scan_map)
    return pl.pallas_call(
        _mixout_kernel,
        grid=(n_tiles,),
        in_specs=[pl.BlockSpec((tm, d), row), pl.BlockSpec((None, 6, d), mod_map),
                  pl.BlockSpec((N_HEADS, tm, DIFF_DV), lambda i: (0, i, 0)),
                  g_blk, g_blk, g_blk, g_blk, s_blk, s_blk, g_blk,
                  pl.BlockSpec((GROUP_W, GROUP_W), const), pl.BlockSpec((1, GROUP_W), const),
                  pl.BlockSpec((1, GROUP_W), const),
                  pl.BlockSpec((d, d), const, pipeline_mode=pl.Buffered(1)),
                  pl.BlockSpec((1, d), const), pl.BlockSpec((1, d), const),
                  pl.BlockSpec((d, LANES), const), pl.BlockSpec((d, LANES), const), pl.BlockSpec((1, LANES), const)],
        out_specs=[pl.BlockSpec((tm, d), row), pl.BlockSpec((tm, d), row),
                   pl.BlockSpec((tm, LANES), row), pl.BlockSpec((tm, LANES), row),
                   pl.BlockSpec((tm, LANES), row), pl.BlockSpec((1, LANES), const)],
        out_shape=[jax.ShapeDtypeStruct((nt, d), F32), jax.ShapeDtypeStruct((nt, d), F32),
                   jax.ShapeDtypeStruct((nt, LANES), jnp.int32), jax.ShapeDtypeStruct((nt, LANES), F32),
                   jax.ShapeDtypeStruct((nt, LANES), jnp.int32), jax.ShapeDtypeStruct((1, LANES), F32)],
        compiler_params=_cparams(("arbitrary",)),
        name="mix_out_router",
    )(x, mod, at, hy, rf, rb, cg, gf, gb, dr, seg.astype(BF16), rg, gg, wo, g1, g2, rw_hi, rw_lo, rbias)


def _expert_kernel(te_ref, na_ref, wres_ref, xs_ref, w1_ref, b1_ref, w2_ref, b2_ref, perm_ref, ys_ref, w1s, w2s):
    i = pl.program_id(0)
    active = i < na_ref[0]
    fresh = jnp.logical_or(i == 0, te_ref[i] != te_ref[jnp.maximum(i - 1, 0)])
    n_groups = w1_ref.shape[1] // (2 * LANES)

    @pl.when(jnp.logical_and(active, fresh))
    def _():
        for c in range(n_groups):
            cols = slice(c * 2 * LANES, (c + 1) * 2 * LANES)
            w1s[:, cols] = _dot(w1_ref[:, cols].astype(BF16), perm_ref[...]).astype(BF16)
        w2s[...] = w2_ref[...].astype(BF16)

    @pl.when(active)
    def _():
        u = _dot(xs_ref[...].astype(BF16), w1s[...]) + b1_ref[...]
        acts = []
        for c in range(n_groups):
            glu = jnp.minimum(u[:, c * 2 * LANES:c * 2 * LANES + LANES], SWIGLU_LIMIT)
            lin = jnp.clip(u[:, c * 2 * LANES + LANES:(c + 1) * 2 * LANES], -SWIGLU_LIMIT, SWIGLU_LIMIT)
            acts.append((glu * _sigmoid(SWIGLU_ALPHA * glu) * (lin + 1.0)).astype(BF16))
        ys_ref[...] = _dot(jnp.concatenate(acts, axis=-1), w2s[...]) + b2_ref[...]

    @pl.when(jnp.logical_not(active))
    def _():
        ys_ref[...] = jnp.zeros_like(ys_ref)


def _expert_ffn(tile_e, n_active, xs, w1, b1, w2, b2, layer):
    p, d = xs.shape
    tm = MOE_TILE
    de2 = w1.shape[3]
    de = de2 // 2
    sel = np.zeros((2 * LANES, 2 * LANES), np.float32)
    sel[2 * np.arange(LANES), np.arange(LANES)] = 1.0
    sel[2 * np.arange(LANES) + 1, LANES + np.arange(LANES)] = 1.0
    b1g = b1.reshape(N_EXPERTS, de2 // (2 * LANES), LANES, 2).transpose(0, 1, 3, 2).reshape(N_EXPERTS, 1, de2)
    n_tiles = p // tm
    tiles = jnp.arange(n_tiles, dtype=jnp.int32)
    live = tiles < n_active[0]
    te_live = jnp.where(live, tile_e, jnp.max(jnp.where(live, tile_e, 0)))
    first = jnp.concatenate([jnp.ones((1,), bool), te_live[1:] != te_live[:-1]])
    later = jnp.where(te_live[None, :] > te_live[:, None], te_live[None, :], N_EXPERTS)
    following = jnp.min(later, axis=1)
    w_res = jnp.where(first | (following == N_EXPERTS), te_live, following).astype(jnp.int32)
    wmap = lambda i, te, na, wr: (te[i], 0, 0)
    lmap = lambda i, te, na, wr: (layer, wr[i], 0, 0)
    xmap = lambda i, te, na, wr: (jnp.minimum(i, jnp.maximum(na[0] - 1, 0)), 0)
    grid_spec = pltpu.PrefetchScalarGridSpec(
        num_scalar_prefetch=3,
        grid=(p // tm,),
        in_specs=[pl.BlockSpec((tm, d), xmap),
                  pl.BlockSpec((None, None, d, de2), lmap), pl.BlockSpec((None, 1, de2), wmap),
                  pl.BlockSpec((None, None, de, d), lmap), pl.BlockSpec((None, 1, d), wmap),
                  pl.BlockSpec((2 * LANES, 2 * LANES), lambda i, te, na, wr: (0, 0))],
        out_specs=pl.BlockSpec((tm, d), lambda i, te, na, wr: (i, 0)),
        scratch_shapes=[pltpu.VMEM((d, de2), BF16), pltpu.VMEM((de, d), BF16)],
    )
    return pl.pallas_call(
        _expert_kernel,
        grid_spec=grid_spec,
        out_shape=jax.ShapeDtypeStruct((p, d), F32),
        compiler_params=_cparams(("arbitrary",)),
        name="expert_ffn",
    )(te_live, n_active, w_res, xs, w1, b1g, w2, b2.reshape(N_EXPERTS, 1, d), jnp.asarray(sel, BF16))


def _route_plan(topi, rank, counts, tm):
    n, k = topi.shape
    padded = ((counts + tm - 1) // tm) * tm
    pend = jnp.cumsum(padded)
    pstart = pend - padded
    experts = jnp.arange(N_EXPERTS, dtype=jnp.int32)
    start = jnp.sum(jnp.where(topi[:, :, None] == experts, pstart, 0), axis=-1)
    dest = (start + rank).astype(jnp.int32)
    p = n * k + N_EXPERTS * tm
    tile_start = jnp.arange(p // tm, dtype=jnp.int32) * tm
    tile_e = jnp.minimum(jnp.sum((tile_start[:, None] >= pend[None, :]).astype(jnp.int32), axis=1), N_EXPERTS - 1)
    n_active = (pend[-1:] // tm).astype(jnp.int32)
    return dest, tile_e, n_active, p, jnp.stack([pstart, pend]).astype(jnp.int32)


def _dispatch_kernel(bounds_ref, dest_ref, x_ref, out_ref, zeros, sem, zsem, *, tb):
    n_rows = tb * TOP_K

    @pl.when(pl.program_id(0) == 0)
    def _():
        zeros[...] = jnp.zeros_like(zeros)

        def fill(e):
            start = pl.multiple_of(bounds_ref[1, e] - MOE_TILE, MOE_TILE)
            return pltpu.make_async_copy(zeros, out_ref.at[pl.ds(start, MOE_TILE)], zsem)

        for e in range(N_EXPERTS):
            @pl.when(bounds_ref[1, e] > bounds_ref[0, e])
            def _():
                fill(e).start()
        for e in range(N_EXPERTS):
            @pl.when(bounds_ref[1, e] > bounds_ref[0, e])
            def _():
                fill(e).wait()

    def issue(t, carry):
        for k in range(TOP_K):
            pltpu.make_async_copy(x_ref.at[pl.ds(t, 1)], out_ref.at[pl.ds(dest_ref[0, t * TOP_K + k], 1)],
                                  sem).start()
        return carry

    lax.fori_loop(0, tb, issue, 0, unroll=4)
    pltpu.make_async_copy(out_ref.at[pl.ds(0, n_rows)], out_ref.at[pl.ds(0, n_rows)], sem).wait()


def _dispatch(h2, dest, bounds, p_rows, n_tok):
    d = h2.shape[1]
    tb = ROW_TILE
    steps = n_tok // tb
    return pl.pallas_call(
        functools.partial(_dispatch_kernel, tb=tb),
        grid=(steps,),
        in_specs=[pl.BlockSpec(memory_space=pltpu.SMEM),
                  pl.BlockSpec((None, 1, tb * TOP_K), lambda i: (i, 0, 0), memory_space=pltpu.SMEM),
                  pl.BlockSpec((tb, d), lambda i: (i, 0))],
        out_specs=pl.BlockSpec(memory_space=pl.ANY),
        out_shape=jax.ShapeDtypeStruct((p_rows, d), F32),
        scratch_shapes=[pltpu.VMEM((MOE_TILE, d), F32), pltpu.SemaphoreType.DMA(()), pltpu.SemaphoreType.DMA(())],
        compiler_params=pltpu.CompilerParams(dimension_semantics=("arbitrary",), has_side_effects=True),
        name="moe_dispatch",
    )(bounds, dest.reshape(steps, 1, tb * TOP_K), h2)


def _combine_kernel(dest_ref, x_ref, tw_ref, mod_ref, g_ref, ys_ref, o_ref, buf, sem, *, tb):
    n_rows = tb * TOP_K

    def issue(r, carry):
        pltpu.make_async_copy(ys_ref.at[pl.ds(dest_ref[0, r], 1)], buf.at[pl.ds(r, 1)], sem).start()
        return carry

    lax.fori_loop(0, n_rows, issue, 0, unroll=8)
    pltpu.make_async_copy(ys_ref.at[pl.ds(0, n_rows)], buf, sem).wait()
    y = None
    for k in range(TOP_K):
        t = buf[k * tb:(k + 1) * tb, :] * tw_ref[:, k:k + 1]
        y = t if y is None else y + t
    o_ref[...] = x_ref[...] + mod_ref[5:6, :] * (_rms(y) * g_ref[...])


def _combine_residual(n_tiles, x, ys, dest, topw, mod, g, geo):
    nt, d = x.shape
    tb = ROW_TILE
    lat_tiles, per_batch, batch = geo["lat_tiles"], geo["tiles_per_batch"], geo["batch"]

    def mod_map(i):
        return (jnp.where(i < lat_tiles, i // per_batch, batch), 0, 0)

    row = lambda i: (i, 0)
    slot_major = jnp.swapaxes(dest.reshape(n_tiles, tb, TOP_K), 1, 2).reshape(n_tiles, 1, tb * TOP_K)
    return pl.pallas_call(
        functools.partial(_combine_kernel, tb=tb),
        grid=(n_tiles,),
        in_specs=[pl.BlockSpec((None, 1, tb * TOP_K), lambda i: (i, 0, 0), memory_space=pltpu.SMEM),
                  pl.BlockSpec((tb, d), row), pl.BlockSpec((tb, LANES), row),
                  pl.BlockSpec((None, 6, d), mod_map), pl.BlockSpec((1, d), lambda i: (0, 0)),
                  pl.BlockSpec(memory_space=pl.ANY)],
        out_specs=pl.BlockSpec((tb, d), row),
        out_shape=jax.ShapeDtypeStruct((n_tiles * tb, d), F32),
        scratch_shapes=[pltpu.VMEM((tb * TOP_K, d), F32), pltpu.SemaphoreType.DMA(())],
        compiler_params=_cparams(("arbitrary",)),
        name="combine_residual",
    )(slot_major, x, topw, mod, g, ys)


def _rope_partner(dim):
    h, q = dim // 2, dim // 4
    perm = np.zeros(dim, np.int32)
    sign = np.zeros(dim, np.float32)
    for base in (0, h):
        for j in range(q):
            perm[base + j], sign[base + j] = base + j + q, -1.0
            perm[base + q + j], sign[base + q + j] = base + j, 1.0
    return perm, sign


def _rot_cols(w, dim):
    perm, sign = _rope_partner(dim)
    reps = w.shape[1] // dim
    full_perm = np.concatenate([perm + r * dim for r in range(reps)])
    return w[:, full_perm] * jnp.asarray(np.tile(sign, reps))


def _rope_tables(n_tok, dim, reps, pad_rows):
    rows = n_tok // GRID_W
    row = jnp.repeat(jnp.arange(rows, dtype=F32), GRID_W)
    col = jnp.tile(jnp.arange(GRID_W, dtype=F32), rows)
    quarter = dim // 4
    inv = ROPE_BASE ** (-jnp.arange(quarter, dtype=F32) / quarter)
    ar = row[:, None] * inv[None]
    ac = col[:, None] * inv[None]
    ang = jnp.concatenate([ar, ar, ac, ac], axis=-1)
    cos = jnp.concatenate([jnp.tile(jnp.cos(ang), (1, reps)), jnp.ones((pad_rows, dim * reps), F32)])
    sin = jnp.concatenate([jnp.tile(jnp.sin(ang), (1, reps)), jnp.zeros((pad_rows, dim * reps), F32)])
    return cos, sin


def _widen_w_in(w):
    sizes = (256, 256, 256, 768, 256, 256, 256, 256, 128, 128, 256, 256, 32)
    cuts = np.cumsum(sizes)[:-1].tolist()
    aq, ak, av, bu, cq, ck, cv, cg, dq, dk, dv, dr, dl = jnp.split(w, cuts, axis=1)
    dl = jnp.pad(dl, ((0, 0), (0, LANES - dl.shape[1])))
    parts = [aq, _rot_cols(aq, DIFF_DQK), ak, _rot_cols(ak, DIFF_DQK), av, bu,
             cq, _rot_cols(cq, RET_DH), ck, _rot_cols(ck, RET_DH), cv, cg, dq, dk, dv, dr, dl]
    return jnp.concatenate(parts, axis=1).astype(BF16)


def kernel(x, c, ctx, c_ctx, w_mod, b_mod, norm_g, w_in, w_out, diff_lambda, diff_subln_g, hy_short_w, hy_short_b, hy_w1, hy_b1, hy_w2, hy_b2, hy_w3, hy_freq, hy_bias, ret_decay, ret_norm_g, gla_gate_w, gla_gate_b, gla_norm_g, router_w, router_b, exp_w1, exp_b1, exp_w2, exp_b2):
    batch, seq, d = x.shape
    n_ctx = ctx.shape[1]
    depth = w_mod.shape[0]
    n_lat_rows, n_ctx_rows = batch * seq, batch * n_ctx
    nt = n_lat_rows + n_ctx_rows
    assert d == D_MODEL and seq % ATT_Q_TILE == 0 and n_ctx % ROW_TILE == 0 and seq % n_ctx == 0
    geo = dict(batch=batch, seq=seq, ctx=n_ctx, lat_tiles=n_lat_rows // ROW_TILE,
               tiles_per_batch=seq // ROW_TILE)

    xs = jnp.concatenate([x.reshape(n_lat_rows, d), ctx.reshape(n_ctx_rows, d)], axis=0)
    mod_rows = 8
    cc = jnp.zeros((mod_rows, d), F32).at[:batch].set(c).at[batch].set(c_ctx)
    mod_all = _modulation(cc, w_mod, b_mod).reshape(depth, mod_rows, 6, d)

    rope = (*_rope_tables(seq, DIFF_DQK, GROUP_W // DIFF_DQK, ROW_TILE),
            *_rope_tables(seq, RET_DH, GROUP_W // RET_DH, ROW_TILE))
    dft_lat = _dft_tables(seq)
    dft_ctx = _dft_tables(n_ctx)
    seg = jnp.asarray(np.kron(np.eye(N_HEADS, dtype=np.float32), np.ones((RET_DH, RET_DH), np.float32)))

    for l in range(depth):
        need_ctx = l < depth - 1
        lam_init = 0.8 - 0.6 * math.exp(-0.3 * l)
        mod = mod_all[l]
        (aq, ak, av, bu, cq, ck, cv, cg, dq, dk, dv, dr, dl) = _in_projection(
            xs, mod, norm_g[l, 0].reshape(1, d), _widen_w_in(w_in[l]), rope, geo)

        sub_g = diff_subln_g[l].reshape(1, DIFF_DV)
        at = _diff_attention(diff_lambda[l], sub_g, aq, ak, av, geo, lam_init)
        if need_ctx:
            at = _diff_attention(diff_lambda[l], sub_g, aq, ak, av, geo, lam_init, prev=at)

        def hyena(n, row_blk0, tables, out_blk0, prev):
            e, dd, nyqf = _hyena_filter(n, hy_w1[l], hy_b1[l], hy_w2[l], hy_b2[l], hy_w3[l], hy_freq[l])
            zb, x0c, zf, nyqz = _hyena_gate(bu, hy_short_w[l], hy_short_b[l], n, batch, row_blk0)
            return _hyena_conv(tables, zb, e, dd, x0c, zf, nyqz, nyqf, hy_bias[l], n, batch, nt, out_blk0, prev)

        hy = hyena(seq, 0, dft_lat, 0, None)
        if need_ctx:
            hy = hyena(n_ctx, n_lat_rows // n_ctx, dft_ctx, n_lat_rows // min(512, n_ctx), hy)

        dec = jnp.repeat(ret_decay[l], RET_DH, axis=-1).reshape(2, 1, N_HEADS * RET_DH)
        rf, rb = _retention(cq, ck, cv, dec, geo)

        gw = jnp.zeros((2, LANES, N_HEADS * GLA_DK), F32)
        gw = gw.at[0, :GLA_RANK].set(gla_gate_w[l, 0]).at[1, GLA_RANK:2 * GLA_RANK].set(gla_gate_w[l, 1])
        gbias = gla_gate_b[l].reshape(2, 1, N_HEADS * GLA_DK)
        gf, gb = _gla(dq, dk, dv, dl, gw, gbias, geo)

        n_tiles = nt // ROW_TILE if need_ctx else n_lat_rows // ROW_TILE
        rw = jnp.pad(router_w[l], ((0, 0), (0, LANES - N_EXPERTS)))
        rbias = jnp.pad(router_b[l], (0, LANES - N_EXPERTS), constant_values=-jnp.inf).reshape(1, LANES)
        x1, h2, topi, topw, rank, cnt = _mix_out(
            n_tiles, xs, mod, at, hy, rf, rb, cg, gf, gb, dr, seg,
            ret_norm_g[l].reshape(1, GROUP_W), jnp.tile(gla_norm_g[l], N_HEADS).reshape(1, GROUP_W),
            w_out[l].astype(BF16), norm_g[l, 1].reshape(1, d), norm_g[l, 2].reshape(1, d), rw, rbias, geo)

        n_tok = n_tiles * ROW_TILE
        dest, tile_e, n_active, p_rows, bounds = _route_plan(
            topi[:n_tok, :TOP_K], rank[:n_tok, :TOP_K], cnt[0, :N_EXPERTS].astype(jnp.int32), MOE_TILE)
        dispatched = _dispatch(h2, dest, bounds, p_rows, n_tok)
        ys = _expert_ffn(tile_e, n_active, dispatched, exp_w1, exp_b1[l], exp_w2, exp_b2[l], l)
        xs = _combine_residual(n_tiles, x1, ys, dest, topw, mod, norm_g[l, 3].reshape(1, d), geo)

    return xs[:n_lat_rows].reshape(batch, seq, d)
```

```python
import functools
import math

import numpy as np
import jax
import jax.numpy as jnp
from jax import lax
from jax.experimental import pallas as pl
from jax.experimental.pallas import tpu as pltpu

F32 = jnp.float32
BF16 = jnp.bfloat16
HIGHEST = lax.Precision.HIGHEST

D_MODEL = 1024
GRID_W = 64
GROUP_W = 256
N_HEADS = 4
DIFF_DQK = 32
DIFF_DV = 64
ROPE_BASE = 10000.0
HY_CH = 256
HY_BANDS = 16
HY_FFN = 64
HY_FAST_DECAY_PCT = 0.3
HY_SLOW_DECAY_PCT = 1.5
HY_DECAY_TARGET = 1e-2
RET_DH = 64
GLA_DK = 32
GLA_DV = 64
GLA_RANK = 16
GLA_GATE_NORM = 16.0
N_EXPERTS = 32
TOP_K = 4
D_EXPERT = 1024
SWIGLU_LIMIT = 7.0
SWIGLU_ALPHA = 1.702
EPS = 1e-6

LANES = 128
ROW_TILE = 256
REC_CHUNK = 128
RET_CHUNK = 256
REC_SUB = 32
EXP_CLAMP = 80.0
MOE_TILE = 256
ATT_KEY_CHUNK = 512
ATT_Q_TILE = 512
DFT_TABLE_ROWS = 128
VMEM_LIMIT = 52 * 1024 * 1024

_A_Q, _A_QR, _A_K, _A_KR, _A_V = 0, 256, 512, 768, 1024
_B_U = 1280
_C_Q, _C_QR, _C_K, _C_KR, _C_V, _C_G = 2048, 2304, 2560, 2816, 3072, 3328
_D_Q, _D_K, _D_V, _D_R, _D_L = 3584, 3712, 3840, 4096, 4352
_IN_COLS = 4480


def _cparams(sem):
    return pltpu.CompilerParams(dimension_semantics=sem, vmem_limit_bytes=VMEM_LIMIT)


def _sigmoid(x):
    return 1.0 / (1.0 + jnp.exp(-x))


def _rms(x):
    return x * lax.rsqrt(jnp.mean(x * x, axis=-1, keepdims=True) + EPS)


def _iota(shape, axis):
    return lax.broadcasted_iota(jnp.int32, shape, axis)


def _dot(a, b):
    return jnp.dot(a, b, preferred_element_type=F32)


def _dot_exact(a, b):
    return jnp.dot(a, b, preferred_element_type=F32, precision=HIGHEST)


def _dot_nt(a, b):
    return lax.dot_general(a, b, (((1,), (1,)), ((), ())), preferred_element_type=F32)


def _dot_tn(a, b):
    return lax.dot_general(a, b, (((0,), (0,)), ((), ())), preferred_element_type=F32)


def _mod_kernel(c_ref, w_ref, b_ref, o_ref):
    c = c_ref[...]
    o_ref[...] = _dot_exact(c * _sigmoid(c), w_ref[...]) + b_ref[...]


def _modulation(cc, w_mod, b_mod):
    depth, d, n = w_mod.shape
    tn = n // 4
    rows = cc.shape[0]
    return pl.pallas_call(
        _mod_kernel,
        grid=(depth, n // tn),
        in_specs=[pl.BlockSpec((rows, d), lambda l, j: (0, 0)),
                  pl.BlockSpec((None, d, tn), lambda l, j: (l, 0, j)),
                  pl.BlockSpec((None, 1, tn), lambda l, j: (l, 0, j))],
        out_specs=pl.BlockSpec((None, rows, tn), lambda l, j: (l, 0, j)),
        out_shape=jax.ShapeDtypeStruct((depth, rows, n), F32),
        compiler_params=_cparams(("arbitrary", "arbitrary")),
        name="modulation",
    )(cc, w_mod, b_mod.reshape(depth, 1, n))


def _inproj_kernel(x_ref, mod_ref, g_ref, w_ref, cosa_ref, sina_ref, cosc_ref, sinc_ref,
                   aq_ref, ak_ref, av_ref, bu_ref, cq_ref, ck_ref, cv_ref, cg_ref,
                   dq_ref, dk_ref, dv_ref, dr_ref, dl_ref):
    xn = _rms(x_ref[...]) * g_ref[...]
    h = (xn * (1.0 + mod_ref[1:2, :]) + mod_ref[0:1, :]).astype(BF16)

    def proj(a, width):
        return _dot(h, w_ref[:, a:a + width])

    def roped(a, a_rot, cos_ref, sin_ref):
        return proj(a, GROUP_W) * cos_ref[...] + proj(a_rot, GROUP_W) * sin_ref[...]

    aq = roped(_A_Q, _A_QR, cosa_ref, sina_ref) * (DIFF_DQK ** -0.5 * math.log2(math.e))
    ak = roped(_A_K, _A_KR, cosa_ref, sina_ref)
    av = proj(_A_V, GROUP_W)
    ones_col = jnp.where(lax.broadcasted_iota(jnp.int32, (av.shape[0], LANES - DIFF_DV), 1) == 0, 1.0, 0.0)
    for hd in range(N_HEADS):
        sl = slice(hd * DIFF_DV, (hd + 1) * DIFF_DV)
        aq_ref[hd] = aq[:, sl].astype(BF16)
        ak_ref[hd] = ak[:, sl].astype(BF16)
        av_ref[hd] = jnp.concatenate([av[:, sl], ones_col], axis=-1).astype(BF16)
    bu_ref[...] = proj(_B_U, 3 * HY_CH)
    cq_ref[...] = roped(_C_Q, _C_QR, cosc_ref, sinc_ref)
    ck_ref[...] = roped(_C_K, _C_KR, cosc_ref, sinc_ref)
    cv_ref[...] = proj(_C_V, GROUP_W)
    cg_ref[...] = proj(_C_G, GROUP_W)
    dq_ref[...] = proj(_D_Q, LANES)
    dk_ref[...] = proj(_D_K, LANES)
    dv_ref[...] = proj(_D_V, GROUP_W)
    dr_ref[...] = proj(_D_R, GROUP_W)
    dl_ref[...] = proj(_D_L, LANES)


def _in_projection(x, mod, g, w_wide, rope, geo):
    nt, d = x.shape
    tm = ROW_TILE
    lat_tiles, per_batch, batch = geo["lat_tiles"], geo["tiles_per_batch"], geo["batch"]

    def mod_map(i):
        return (jnp.where(i < lat_tiles, i // per_batch, batch), 0, 0)

    def rope_map(i):
        return (jnp.where(i < lat_tiles, i % per_batch, per_batch), 0)

    row = lambda i: (i, 0)
    head = lambda i: (0, i, 0)
    const = lambda i: (0, 0)
    f32_out = lambda w: jax.ShapeDtypeStruct((nt, w), F32)
    head_out = lambda w: jax.ShapeDtypeStruct((N_HEADS, nt, w), BF16)
    head_spec = lambda w: pl.BlockSpec((N_HEADS, tm, w), head)
    widths = [3 * HY_CH, GROUP_W, GROUP_W, GROUP_W, GROUP_W, LANES, LANES, GROUP_W, GROUP_W, LANES]
    return pl.pallas_call(
        _inproj_kernel,
        grid=(nt // tm,),
        in_specs=[pl.BlockSpec((tm, d), row),
                  pl.BlockSpec((None, 6, d), mod_map),
                  pl.BlockSpec((1, d), const),
                  pl.BlockSpec((d, _IN_COLS), const, pipeline_mode=pl.Buffered(1)),
                  pl.BlockSpec((tm, GROUP_W), rope_map), pl.BlockSpec((tm, GROUP_W), rope_map),
                  pl.BlockSpec((tm, GROUP_W), rope_map), pl.BlockSpec((tm, GROUP_W), rope_map)],
        out_specs=[head_spec(DIFF_DV), head_spec(DIFF_DV), head_spec(LANES)]
                  + [pl.BlockSpec((tm, w), row) for w in widths],
        out_shape=[head_out(DIFF_DV), head_out(DIFF_DV), head_out(LANES)] + [f32_out(w) for w in widths],
        compiler_params=_cparams(("arbitrary",)),
        name="in_projection",
    )(x, mod, g, w_wide, *rope)


def _attn_kernel(lam_ref, g_ref, q_ref, *rest, lam_init, has_lat):
    if has_lat:
        kl_ref, vl_ref, kc_ref, vc_ref, o_ref = rest
        keys = [(kl_ref, vl_ref), (kc_ref, vc_ref)]
    else:
        kc_ref, vc_ref, o_ref = rest
        keys = [(kc_ref, vc_ref)]
    lp = lam_ref[...]
    lam = (jnp.exp(jnp.sum(lp[0:1] * lp[1:2], axis=-1, keepdims=True))
           - jnp.exp(jnp.sum(lp[2:3] * lp[3:4], axis=-1, keepdims=True)) + lam_init)
    q = q_ref[...]
    tq = q.shape[0]
    lane = lax.broadcasted_iota(jnp.int32, q.shape, 1)
    chunks = []
    for k_ref, v_ref in keys:
        size = min(ATT_KEY_CHUNK, k_ref.shape[0])
        chunks += [(k_ref, v_ref, s0, size) for s0 in range(0, k_ref.shape[0], size)]

    def lane_groups(t):
        return [t[:, c0:c0 + LANES] for c0 in range(0, t.shape[1], LANES)]

    qm = [jnp.where((lane >= m * DIFF_DQK) & (lane < (m + 1) * DIFF_DQK), q, jnp.zeros_like(q)) for m in range(2)]

    def score(m, j):
        k_ref, _, s0, size = chunks[j]
        return _dot_nt(qm[m], k_ref[s0:s0 + size, :])

    def row_max(scores):
        wide = functools.reduce(jnp.maximum, [g for t in scores for g in lane_groups(t)])
        return jnp.max(wide, axis=-1, keepdims=True)

    def weighted(m, j, s, mx, acc):
        _, v_ref, s0, size = chunks[j]
        return acc + _dot(jnp.exp2(s - mx).astype(BF16), v_ref[s0:s0 + size, :])

    n = len(chunks)
    s1 = [score(0, j) for j in range(n)]
    mx1 = row_max(s1)
    s2 = []
    acc1 = jnp.zeros((tq, LANES), F32)
    for j in range(n):
        s2.append(score(1, j))
        acc1 = weighted(0, j, s1[j], mx1, acc1)
    mx2 = row_max(s2)
    acc2 = jnp.zeros((tq, LANES), F32)
    for j in range(n):
        acc2 = weighted(1, j, s2[j], mx2, acc2)
    o = (acc1[:, :DIFF_DV] * (1.0 / acc1[:, DIFF_DV:DIFF_DV + 1])
         - lam * (acc2[:, :DIFF_DV] * (1.0 / acc2[:, DIFF_DV:DIFF_DV + 1])))
    o_ref[...] = _rms(o) * g_ref[...] * (1.0 - lam_init)


def _diff_attention(lam_p, subln_g, aq, ak, av, geo, lam_init, prev=None):
    batch, seq, ctx = geo["batch"], geo["seq"], geo["ctx"]
    nt = aq.shape[1]
    has_lat = prev is None
    tq = ATT_Q_TILE if has_lat else ROW_TILE
    n_q = (seq if has_lat else ctx) // tq
    q_off = 0 if has_lat else (batch * seq) // tq
    ctx_blk0 = (batch * seq) // ctx

    qmap = lambda b, h, i: (h, q_off + b * n_q + i, 0)
    lat_map = lambda b, h, i: (h, b, 0)
    ctx_map = lambda b, h, i: (h, ctx_blk0 + b, 0)
    const = lambda b, h, i: (0, 0)
    in_specs = [pl.BlockSpec((4, DIFF_DQK), const), pl.BlockSpec((1, DIFF_DV), const),
                pl.BlockSpec((None, tq, DIFF_DV), qmap)]
    args = [lam_p, subln_g, aq]
    if has_lat:
        in_specs += [pl.BlockSpec((None, seq, DIFF_DV), lat_map), pl.BlockSpec((None, seq, LANES), lat_map)]
        args += [ak, av]
    in_specs += [pl.BlockSpec((None, ctx, DIFF_DV), ctx_map), pl.BlockSpec((None, ctx, LANES), ctx_map)]
    args += [ak, av]
    aliases = {}
    if not has_lat:
        in_specs.append(pl.BlockSpec(memory_space=pl.ANY))
        args.append(prev)
        aliases = {len(args) - 1: 0}
    kern = functools.partial(_attn_kernel, lam_init=lam_init, has_lat=has_lat)
    if not has_lat:
        kern = _drop_last_input(kern, n_in=len(args))
    return pl.pallas_call(
        kern,
        grid=(batch, N_HEADS, n_q),
        in_specs=in_specs,
        out_specs=pl.BlockSpec((None, tq, DIFF_DV), qmap),
        out_shape=jax.ShapeDtypeStruct((N_HEADS, nt, DIFF_DV), F32),
        input_output_aliases=aliases,
        compiler_params=_cparams(("arbitrary", "arbitrary", "arbitrary")),
        name="diff_attention" if has_lat else "diff_attention_ctx",
    )(*args)


def _drop_last_input(kern, n_in):
    def wrapped(*refs):
        return kern(*refs[:n_in - 1], *refs[n_in:])
    return wrapped


def _dft_table_kernel(cx_ref, sx_ref, cy_ref, sy_ref, c_ref, s_ref, *, n, span):
    t = _iota((LANES, n), 1)
    row = _iota((LANES, n), 0)
    pick_a = jnp.where(t // span == row, 1.0, 0.0).astype(BF16)
    pick_b = jnp.where(t % span == row, 1.0, 0.0).astype(BF16)

    def widen(ref, pick):
        x = ref[...]
        hi = x.astype(BF16)
        lo = (x - hi.astype(F32)).astype(BF16)
        return _dot(hi, pick) + _dot(lo, pick)

    cx, sx = widen(cx_ref, pick_a), widen(sx_ref, pick_a)
    cy, sy = widen(cy_ref, pick_b), widen(sy_ref, pick_b)
    c_ref[...] = (cx * cy - sx * sy).astype(BF16)
    s_ref[...] = (sx * cy + cx * sy).astype(BF16)


def _dft_tables(n):
    span = 64
    assert n % span == 0 and n // span <= LANES
    k = jnp.arange(n, dtype=jnp.int32)[:, None]
    j = jnp.arange(LANES, dtype=jnp.int32)[None, :]

    def factor(step, count):
        ang = ((k * (j * step)) % (2 * n)).astype(F32) * (math.pi / n)
        live = j < count
        return jnp.where(live, jnp.cos(ang), 0.0), jnp.where(live, jnp.sin(ang), 0.0)

    cx, sx = factor(span, n // span)
    cy, sy = factor(1, span)
    tk = min(DFT_TABLE_ROWS, n)
    small = pl.BlockSpec((tk, LANES), lambda i: (i, 0))
    big = pl.BlockSpec((tk, n), lambda i: (i, 0))
    return pl.pallas_call(
        functools.partial(_dft_table_kernel, n=n, span=span),
        grid=(n // tk,),
        in_specs=[small] * 4,
        out_specs=[big, big],
        out_shape=[jax.ShapeDtypeStruct((n, n), BF16)] * 2,
        compiler_params=_cparams(("arbitrary",)),
        name="dft_tables",
    )(cx, sx, cy, sy)


def _hy_filter_kernel(w1t_ref, w1c_ref, w1s_ref, b1_ref, w2_ref, b2_ref, w3_ref, fr_ref,
                      bands_ref, deltas_ref, e_ref, d_ref, nyq_ref, *, n):
    pos_i = lax.broadcasted_iota(jnp.int32, (n, 1), 0)
    pos = pos_i.astype(F32)
    t = pos / (n - 1)
    ang = ((2.0 * math.pi) * pos / n) * bands_ref[...]
    pre = t * w1t_ref[...] + _dot_exact(jnp.cos(ang), w1c_ref[...]) - _dot_exact(jnp.sin(ang), w1s_ref[...])
    hdn = jnp.sin(fr_ref[0:1, :] * (pre + b1_ref[...]))
    hdn = jnp.sin(fr_ref[1:2, :] * (_dot_exact(hdn, w2_ref[...]) + b2_ref[...]))
    raw = _dot_exact(hdn, w3_ref[...])
    window = jnp.exp(-t * deltas_ref[...])
    hf = raw[:, :HY_CH] * window
    hb = jnp.where(pos_i > 0, raw[:, HY_CH:] * window, 0.0)
    inv = 1.0 / (jnp.sum(jnp.abs(hf), axis=0, keepdims=True) + jnp.sum(jnp.abs(hb), axis=0, keepdims=True))
    e = (hf + hb) * inv
    e_ref[...] = e.astype(BF16)
    d_ref[...] = ((hb - hf) * inv).astype(BF16)
    sign = (1 - 2 * (pos_i & 1)).astype(F32)
    nyq_ref[...] = jnp.sum(e * sign, axis=0, keepdims=True)


def _hyena_filter(n, w1, b1, w2, b2, w3, freq):
    bands = jnp.linspace(1e-4, HY_BANDS - 1, HY_BANDS, dtype=F32).reshape(1, HY_BANDS)
    max_decay = math.log(HY_DECAY_TARGET) / HY_FAST_DECAY_PCT
    min_decay = math.log(HY_DECAY_TARGET) / HY_SLOW_DECAY_PCT
    deltas = jnp.abs(jnp.linspace(min_decay, max_decay, HY_CH, dtype=F32)).reshape(1, HY_CH)
    args = [w1[0:1], w1[1:1 + HY_BANDS], w1[1 + HY_BANDS:], b1.reshape(1, HY_FFN), w2, b2.reshape(1, HY_FFN),
            w3, freq, bands, deltas]
    return pl.pallas_call(
        functools.partial(_hy_filter_kernel, n=n),
        out_shape=[jax.ShapeDtypeStruct((n, HY_CH), BF16), jax.ShapeDtypeStruct((n, HY_CH), BF16),
                   jax.ShapeDtypeStruct((1, HY_CH), F32)],
        compiler_params=pltpu.CompilerParams(vmem_limit_bytes=VMEM_LIMIT),
        name="hyena_filter",
    )(*args)


def _hy_gate_kernel(x0_ref, x1_ref, v_ref, w0_ref, w1_ref, wv_ref, b0_ref, b1_ref, bv_ref,
                    zb_ref, x0c_ref, zf_ref, nyq_ref, *, n):
    row = lax.broadcasted_iota(jnp.int32, (n, 1), 0)

    def conv(u_ref, w_ref, b_ref):
        u = u_ref[...]
        up = jnp.where(row > 0, pltpu.roll(u, 1, 0), 0.0)
        dn = jnp.where(row < n - 1, pltpu.roll(u, n - 1, 0), 0.0)
        return up * w_ref[0:1, :] + u * w_ref[1:2, :] + dn * w_ref[2:3, :] + b_ref[...]

    z = conv(x1_ref, w1_ref, b1_ref) * conv(v_ref, wv_ref, bv_ref)
    x0c_ref[...] = conv(x0_ref, w0_ref, b0_ref)
    zf_ref[...] = z
    zb_ref[...] = z.astype(BF16)
    sign = (1 - 2 * (row & 1)).astype(F32)
    nyq_ref[...] = jnp.sum(z * sign, axis=0, keepdims=True)


def _hyena_gate(bu, short_w, short_b, n, batch, row_blk0):
    halves = HY_CH // LANES
    sb = short_b.reshape(1, 3 * HY_CH)
    seg = lambda part: pl.BlockSpec((n, LANES), lambda b, j: (row_blk0 + b, part * halves + j))
    wsp = lambda part: pl.BlockSpec((3, LANES), lambda b, j: (0, part * halves + j))
    bsp = lambda part: pl.BlockSpec((1, LANES), lambda b, j: (0, part * halves + j))
    return pl.pallas_call(
        functools.partial(_hy_gate_kernel, n=n),
        grid=(batch, halves),
        in_specs=[seg(0), seg(1), seg(2), wsp(0), wsp(1), wsp(2), bsp(0), bsp(1), bsp(2)],
        out_specs=[pl.BlockSpec((n, LANES), lambda b, j: (0, b * halves + j)),
                   pl.BlockSpec((n, LANES), lambda b, j: (b, j)),
                   pl.BlockSpec((n, LANES), lambda b, j: (b, j)),
                   pl.BlockSpec((None, 1, LANES), lambda b, j: (b, 0, j))],
        out_shape=[jax.ShapeDtypeStruct((n, batch * HY_CH), BF16),
                   jax.ShapeDtypeStruct((batch * n, HY_CH), F32),
                   jax.ShapeDtypeStruct((batch * n, HY_CH), F32),
                   jax.ShapeDtypeStruct((batch, 1, HY_CH), F32)],
        compiler_params=_cparams(("arbitrary", "arbitrary")),
        name="hyena_gate",
    )(bu, bu, bu, short_w, short_w, short_w, sb, sb, sb)


def _hy_spectrum_kernel(c_ref, s_ref, z_ref, e_ref, d_ref, yr_ref, yi_ref, *, n, tk, batch):
    c = c_ref[...]
    s = s_ref[...]
    zr = _dot(c, z_ref[...])
    zs = _dot(s, z_ref[...])
    fr = _dot(c, e_ref[...])
    fi = _dot(s, d_ref[...])
    k = pl.program_id(0) * tk + lax.broadcasted_iota(jnp.int32, (tk, 1), 0)
    wk = jnp.where(k == 0, 1.0, 2.0) * (1.0 / (2 * n))
    for b in range(batch):
        sl = slice(b * HY_CH, (b + 1) * HY_CH)
        yr = zr[:, sl] * fr + zs[:, sl] * fi
        yi = zr[:, sl] * fi - zs[:, sl] * fr
        yr_ref[:, sl] = (yr * wk).astype(BF16)
        yi_ref[:, sl] = (-(yi * wk)).astype(BF16)


def _hy_inverse_kernel(c_ref, s_ref, yr_ref, yi_ref, x0c_ref, zf_ref, nyqz_ref, nyqf_ref, bias_ref,
                       *rest, n, tt, batch):
    o_ref, y_scr = rest[-2], rest[-1]
    b = pl.program_id(1)

    @pl.when(b == 0)
    def _():
        y = _dot(c_ref[...], yr_ref[...]) + _dot(s_ref[...], yi_ref[...])
        for bb in range(batch):
            y_scr[bb] = y[:, bb * HY_CH:(bb + 1) * HY_CH]

    t = pl.program_id(0) * tt + lax.broadcasted_iota(jnp.int32, (tt, 1), 0)
    sign = (1 - 2 * (t & 1)).astype(F32)
    nyq = nyqz_ref[...] * nyqf_ref[...] * (1.0 / (2 * n))
    zf = zf_ref[...]
    o_ref[...] = x0c_ref[...] * (y_scr[b] + sign * nyq + bias_ref[...] * zf)


def _hyena_conv(cs, zb, e, d, x0c, zf, nyqz, nyqf, bias, n, batch, nt, out_blk0, prev=None):
    c_tab, s_tab = cs
    bw = batch * HY_CH
    tk = min(512, n)
    whole = lambda shape: pl.BlockSpec(shape, lambda *_: (0,) * len(shape), pipeline_mode=pl.Buffered(1))
    yr, yi = pl.pallas_call(
        functools.partial(_hy_spectrum_kernel, n=n, tk=tk, batch=batch),
        grid=(n // tk,),
        in_specs=[pl.BlockSpec((tk, n), lambda i: (i, 0)), pl.BlockSpec((tk, n), lambda i: (i, 0)),
                  whole((n, bw)), whole((n, HY_CH)), whole((n, HY_CH))],
        out_specs=[pl.BlockSpec((tk, bw), lambda i: (i, 0))] * 2,
        out_shape=[jax.ShapeDtypeStruct((n, bw), BF16)] * 2,
        compiler_params=_cparams(("arbitrary",)),
        name="hyena_spectrum",
    )(c_tab, s_tab, zb, e, d)

    tt = min(512, n)
    n_t = n // tt
    seg = lambda i, b: (b * n_t + i, 0)
    in_specs = [pl.BlockSpec((tt, n), lambda i, b: (i, 0)), pl.BlockSpec((tt, n), lambda i, b: (i, 0)),
                whole((n, bw)), whole((n, bw)),
                pl.BlockSpec((tt, HY_CH), seg), pl.BlockSpec((tt, HY_CH), seg),
                pl.BlockSpec((None, 1, HY_CH), lambda i, b: (b, 0, 0)),
                pl.BlockSpec((1, HY_CH), lambda i, b: (0, 0)), pl.BlockSpec((1, HY_CH), lambda i, b: (0, 0))]
    args = [c_tab, s_tab, yr, yi, x0c, zf, nyqz, nyqf, bias.reshape(1, HY_CH)]
    aliases = {}
    if prev is not None:
        in_specs.append(pl.BlockSpec(memory_space=pl.ANY))
        args.append(prev)
        aliases = {len(args) - 1: 0}
    return pl.pallas_call(
        functools.partial(_hy_inverse_kernel, n=n, tt=tt, batch=batch),
        grid=(n_t, batch),
        in_specs=in_specs,
        out_specs=pl.BlockSpec((tt, HY_CH), lambda i, b: (out_blk0 + b * n_t + i, 0)),
        out_shape=jax.ShapeDtypeStruct((nt, HY_CH), F32),
        scratch_shapes=[pltpu.VMEM((batch, tt, HY_CH), F32)],
        input_output_aliases=aliases,
        compiler_params=_cparams(("arbitrary", "arbitrary")),
        name="hyena_inverse" if prev is None else "hyena_inverse_ctx",
    )(*args)


def _scan_maps(geo, cc):
    batch, seq, ctx = geo["batch"], geo["seq"], geo["ctx"]
    n_cc, n_lc = ctx // cc, seq // cc
    ctx0 = (batch * seq) // cc

    def fwd(b, i):
        return (jnp.where(i < n_cc, ctx0 + b * n_cc + i, b * n_lc + i - n_cc), 0)

    def bwd(b, i):
        return (jnp.where(i < n_cc, ctx0 + b * n_cc + (n_cc - 1 - i), b * n_lc + (n_lc - 1 - (i - n_cc))), 0)

    return fwd, bwd, n_cc + n_lc


def _ret_kernel(qf_ref, kf_ref, vf_ref, qb_ref, kb_ref, vb_ref, dec_ref, of_ref, ob_ref, st_ref):
    @pl.when(pl.program_id(1) == 0)
    def _():
        st_ref[...] = jnp.zeros_like(st_ref)

    cc = qf_ref.shape[0]
    r_i, c_i = _iota((cc, cc), 0), _iota((cc, cc), 1)
    pos = _iota((cc, 1), 0).astype(F32)
    for d, (q_ref, k_ref, v_ref, o_ref) in enumerate(((qf_ref, kf_ref, vf_ref, of_ref),
                                                      (qb_ref, kb_ref, vb_ref, ob_ref))):
        reverse = d == 1
        lg = -jnp.exp(dec_ref[d])
        steps_in = (cc - pos) if reverse else (pos + 1.0)
        steps_out = pos if reverse else (cc - 1.0 - pos)
        q = q_ref[...]
        k = k_ref[...] * (RET_DH ** -0.5)
        qd = (q * jnp.exp(steps_in * lg)).astype(BF16)
        kd = (k * jnp.exp(steps_out * lg)).astype(BF16)
        qb, kb, vb = q.astype(BF16), k.astype(BF16), v_ref[...].astype(BF16)
        gain = jnp.exp(cc * lg)
        dist = (c_i - r_i) if reverse else (r_i - c_i)
        keep = dist >= 0
        dist_f = jnp.where(keep, dist, 0).astype(F32)
        for h in range(N_HEADS):
            hs = slice(h * RET_DH, (h + 1) * RET_DH)
            decay = jnp.where(keep, jnp.exp(dist_f * lg[:, h * RET_DH:h * RET_DH + 1]), 0.0)
            sc = (_dot_nt(qb[:, hs], kb[:, hs]) * decay).astype(BF16)
            st = st_ref[d, h]
            o_ref[:, hs] = _dot(sc, vb[:, hs]) + _dot_nt(qd[:, hs], st.astype(BF16))
            st_ref[d, h] = st * gain[:, hs] + _dot_tn(vb[:, hs], kd[:, hs])


def _retention(q, k, v, dec, geo):
    nt, w = q.shape
    cc = RET_CHUNK
    fwd, bwd, steps = _scan_maps(geo, cc)
    blk = lambda m: pl.BlockSpec((cc, w), m)
    return pl.pallas_call(
        _ret_kernel,
        grid=(geo["batch"], steps),
        in_specs=[blk(fwd), blk(fwd), blk(fwd), blk(bwd), blk(bwd), blk(bwd),
                  pl.BlockSpec((2, 1, w), lambda b, i: (0, 0, 0))],
        out_specs=[blk(fwd), blk(bwd)],
        out_shape=[jax.ShapeDtypeStruct((nt, w), F32)] * 2,
        scratch_shapes=[pltpu.VMEM((2, N_HEADS, RET_DH, RET_DH), F32)],
        compiler_params=_cparams(("arbitrary", "arbitrary")),
        name="recurrence_ret",
    )(q, k, v, q, k, v, dec)


def _gla_direction(q, k, v, la, st_ref, o_ref, *, reverse):
    cc, wk = q.shape
    wv = v.shape[1]
    r_i, c_i = _iota((cc, cc), 0), _iota((cc, cc), 1)
    incl = (c_i >= r_i) if reverse else (c_i <= r_i)
    cum = _dot_exact(jnp.where(incl, 1.0, 0.0), la)
    cum_end = cum[0:1] if reverse else cum[cc - 1:cc]

    qd = (q * jnp.exp(cum)).astype(BF16)
    kd = (k * jnp.exp(cum_end - cum)).astype(BF16)
    vb = v.astype(BF16)
    st = st_ref[...]
    inter = _dot_nt(qd, st.astype(BF16))
    same_head = (_iota((wv, wk), 0) // GLA_DV) == (_iota((wv, wk), 1) // GLA_DK)
    st_ref[...] = st * jnp.exp(cum_end) + jnp.where(same_head, _dot_tn(vb, kd), 0.0)

    sub = REC_SUB
    hs = N_HEADS * sub
    q_own = (_iota((hs, wk), 0) // sub) == (_iota((hs, wk), 1) // GLA_DK)
    for j in range(cc // sub):
        r0, r1 = j * sub, (j + 1) * sub
        ka, kb = (r0, cc) if reverse else (0, r1)
        mid = cum[r0 + sub // 2:r0 + sub // 2 + 1]
        qj = q[r0:r1] * jnp.exp(jnp.minimum(cum[r0:r1] - mid, EXP_CLAMP))
        kj = (k[ka:kb] * jnp.exp(jnp.minimum(mid - cum[ka:kb], EXP_CLAMP))).astype(BF16)
        q_stack = jnp.where(q_own, jnp.concatenate([qj] * N_HEADS, axis=0), 0.0).astype(BF16)
        rows = r0 + (_iota((hs, kb - ka), 0) % sub)
        cols = ka + _iota((hs, kb - ka), 1)
        keep = (cols >= rows) if reverse else (cols <= rows)
        sc = jnp.where(keep, _dot_nt(q_stack, kj), 0.0).astype(BF16)
        full = _dot(sc, vb[ka:kb])
        lane_head = _iota((sub, wv), 1) // GLA_DV
        oj = inter[r0:r1]
        for h in range(N_HEADS):
            oj = oj + jnp.where(lane_head == h, full[h * sub:(h + 1) * sub], 0.0)
        o_ref[r0:r1, :] = oj


def _gla_kernel(*refs, batch):
    per_batch = refs[:8 * batch]
    gw_ref, gb_ref, of_ref, ob_ref, st_ref = refs[8 * batch:]

    @pl.when(pl.program_id(0) == 0)
    def _():
        st_ref[...] = jnp.zeros_like(st_ref)

    for b in range(batch):
        qf_ref, kf_ref, vf_ref, lf_ref, qb_ref, kb_ref, vb_ref, lb_ref = per_batch[8 * b:8 * b + 8]
        for d, (q_ref, k_ref, v_ref, l_ref, o_ref) in enumerate(((qf_ref, kf_ref, vf_ref, lf_ref, of_ref),
                                                                 (qb_ref, kb_ref, vb_ref, lb_ref, ob_ref))):
            logit = _dot_exact(l_ref[...], gw_ref[d]) + gb_ref[d]
            la = (jnp.minimum(logit, 0.0) - jnp.log(1.0 + jnp.exp(-jnp.abs(logit)))) * (1.0 / GLA_GATE_NORM)
            _gla_direction(q_ref[...] * (GLA_DK ** -0.5), k_ref[...], v_ref[...], la, st_ref.at[b, d],
                           o_ref.at[b], reverse=(d == 1))


def _gla(q, k, v, glr, gw, gb, geo):
    batch, seq, ctx = geo["batch"], geo["seq"], geo["ctx"]
    wq, wv = q.shape[1], v.shape[1]
    cc = REC_CHUNK
    n_cc, n_lc = ctx // cc, seq // cc
    ctx0 = (batch * seq) // cc
    in_specs, args = [], []
    for b in range(batch):
        fwd = lambda i, b=b: (jnp.where(i < n_cc, ctx0 + b * n_cc + i, b * n_lc + i - n_cc), 0)
        bwd = lambda i, b=b: (jnp.where(i < n_cc, ctx0 + b * n_cc + (n_cc - 1 - i),
                                        b * n_lc + (n_lc - 1 - (i - n_cc))), 0)
        for m in (fwd, bwd):
            in_specs += [pl.BlockSpec((cc, wq), m), pl.BlockSpec((cc, wq), m), pl.BlockSpec((cc, wv), m),
                         pl.BlockSpec((cc, LANES), m)]
            args += [q, k, v, glr]
    in_specs += [pl.BlockSpec((2, LANES, wq), lambda i: (0, 0, 0)), pl.BlockSpec((2, 1, wq), lambda i: (0, 0, 0))]
    args += [gw, gb]
    steps = n_cc + n_lc
    out_blk = lambda m: pl.BlockSpec((batch, cc, wv), m)
    return pl.pallas_call(
        functools.partial(_gla_kernel, batch=batch),
        grid=(steps,),
        in_specs=in_specs,
        out_specs=[out_blk(lambda i: (0, i, 0)),
                   out_blk(lambda i: (0, jnp.where(i < n_cc, n_cc - 1 - i, steps - 1 - (i - n_cc)), 0))],
        out_shape=[jax.ShapeDtypeStruct((batch, ctx + seq, wv), F32)] * 2,
        scratch_shapes=[pltpu.VMEM((batch, 2, wv, wq), F32)],
        compiler_params=_cparams(("arbitrary",)),
        name="recurrence_gla",
    )(*args)


def _mixout_kernel(x_ref, mod_ref, at_ref, hy_ref, rf_ref, rb_ref, cg_ref, gf_ref, gb_ref, dr_ref,
                   seg_ref, rg_ref, gg_ref, wo_ref, g1_ref, g2_ref, rwh_ref, rwl_ref, rbias_ref,
                   x1_ref, h2_ref, ti_ref, tw_ref, rk_ref, cnt_ref):
    @pl.when(pl.program_id(0) == 0)
    def _():
        cnt_ref[...] = jnp.zeros_like(cnt_ref)

    def head_norm(o):
        ms = _dot((o * o).astype(BF16), seg_ref[...]) * (1.0 / RET_DH)
        return o * lax.rsqrt(ms + EPS)

    a = jnp.concatenate([at_ref[h] for h in range(N_HEADS)], axis=-1)
    cg = cg_ref[...]
    dr = dr_ref[...]
    rt = head_norm(rf_ref[...] + rb_ref[...]) * rg_ref[...] * (cg * _sigmoid(cg))
    gl = head_norm(gf_ref[...] + gb_ref[...]) * gg_ref[...] * (dr * _sigmoid(dr))
    cat = jnp.concatenate([a, hy_ref[...], rt, gl], axis=-1).astype(BF16)
    y = _dot(cat, wo_ref[...])
    x1 = x_ref[...] + mod_ref[2:3, :] * (_rms(y) * g1_ref[...])
    x1_ref[...] = x1
    h2 = _rms(x1) * g2_ref[...] * (1.0 + mod_ref[4:5, :]) + mod_ref[3:4, :]
    h2_ref[...] = h2

    h_hi = h2.astype(BF16)
    h_lo = (h2 - h_hi.astype(F32)).astype(BF16)
    vals = (_dot(h_hi, rwh_ref[...]) + _dot(h_lo, rwh_ref[...]) + _dot(h_hi, rwl_ref[...])
            + rbias_ref[...])
    tm = vals.shape[0]
    lane = lax.broadcasted_iota(jnp.int32, vals.shape, 1)
    idx_out = jnp.zeros(vals.shape, jnp.int32)
    w_out = jnp.zeros(vals.shape, F32)
    top = None
    den = 0.0
    picks = []
    for r in range(TOP_K):
        m = jnp.max(vals, axis=-1, keepdims=True)
        idx = jnp.min(jnp.where(vals == m, lane, LANES), axis=-1, keepdims=True)
        hit = lane == idx
        vals = jnp.where(hit, -jnp.inf, vals)
        top = m if top is None else top
        e = jnp.exp(m - top)
        den = den + e
        picks.append((idx, e, hit))
    inv = 1.0 / den
    for r, (idx, e, _) in enumerate(picks):
        idx_out = jnp.where(lane == r, idx, idx_out)
        w_out = jnp.where(lane == r, e * inv, w_out)
    ti_ref[...] = idx_out
    tw_ref[...] = w_out

    chosen = functools.reduce(jnp.logical_or, [hit for _, _, hit in picks])
    chosen_f = jnp.where(chosen, 1.0, 0.0)
    earlier = (lax.broadcasted_iota(jnp.int32, (tm, tm), 1) < lax.broadcasted_iota(jnp.int32, (tm, tm), 0))
    before = _dot(jnp.where(earlier, 1.0, 0.0).astype(BF16), chosen_f.astype(BF16)) + cnt_ref[...]
    rk_out = jnp.zeros(vals.shape, jnp.int32)
    for r, (_, _, hit) in enumerate(picks):
        rank = jnp.sum(jnp.where(hit, before, 0.0), axis=-1, keepdims=True)
        rk_out = jnp.where(lane == r, rank.astype(jnp.int32), rk_out)
    rk_ref[...] = rk_out
    cnt_ref[...] = cnt_ref[...] + jnp.sum(chosen_f, axis=0, keepdims=True)


def _mix_out(n_tiles, x, mod, at, hy, rf, rb, cg, gf, gb, dr, seg, rg, gg, wo, g1, g2, rw, rbias, geo):
    rw_hi = rw.astype(BF16)
    rw_lo = (rw - rw_hi.astype(F32)).astype(BF16)
    nt, d = x.shape
    tm = ROW_TILE
    lat_tiles, per_batch, batch = geo["lat_tiles"], geo["tiles_per_batch"], geo["batch"]

    def mod_map(i):
        return (jnp.where(i < lat_tiles, i // per_batch, batch), 0, 0)

    row = lambda i: (i, 0)
    const = lambda i: (0, 0)
    g_blk = pl.BlockSpec((tm, GROUP_W), row)
    ctx_tiles = geo["ctx"] // tm

    def scan_map(i):
        c = i - lat_tiles
        return (jnp.where(i < lat_tiles, i // per_batch, c // ctx_tiles),
                jnp.where(i < lat_tiles, ctx_tiles + i % per_batch, c % ctx_tiles), 0)

    s_blk = pl.BlockSpec((None, tm, GROUP_W), scan_map)
    return pl.pallas_call(
        _mixout_kernel,
        grid=(n_tiles,),
        in_specs=[pl.BlockSpec((tm, d), row), pl.BlockSpec((None, 6, d), mod_map),
                  pl.BlockSpec((N_HEADS, tm, DIFF_DV), lambda i: (0, i, 0)),
                  g_blk, g_blk, g_blk, g_blk, s_blk, s_blk, g_blk,
                  pl.BlockSpec((GROUP_W, GROUP_W), const), pl.BlockSpec((1, GROUP_W), const),
                  pl.BlockSpec((1, GROUP_W), const),
                  pl.BlockSpec((d, d), const, pipeline_mode=pl.Buffered(1)),
                  pl.BlockSpec((1, d), const), pl.BlockSpec((1, d), const),
                  pl.BlockSpec((d, LANES), const), pl.BlockSpec((d, LANES), const), pl.BlockSpec((1, LANES), const)],
        out_specs=[pl.BlockSpec((tm, d), row), pl.BlockSpec((tm, d), row),
                   pl.BlockSpec((tm, LANES), row), pl.BlockSpec((tm, LANES), row),
                   pl.BlockSpec((tm, LANES), row), pl.BlockSpec((1, LANES), const)],
        out_shape=[jax.ShapeDtypeStruct((nt, d), F32), jax.ShapeDtypeStruct((nt, d), F32),
                   jax.ShapeDtypeStruct((nt, LANES), jnp.int32), jax.ShapeDtypeStruct((nt, LANES), F32),
                   jax.ShapeDtypeStruct((nt, LANES), jnp.int32), jax.ShapeDtypeStruct((1, LANES), F32)],
        compiler_params=_cparams(("arbitrary",)),
        name="mix_out_router",
    )(x, mod, at, hy, rf, rb, cg, gf, gb, dr, seg.astype(BF16), rg, gg, wo, g1, g2, rw_hi, rw_lo, rbias)


def _expert_kernel(te_ref, na_ref, wres_ref, inv_ref, xs_ref, w1_ref, b1_ref, w2_ref, b2_ref, perm_ref, y4_ref,
                   w1s, w2s, ybuf, sem):
    i = pl.program_id(0)
    active = i < na_ref[0]
    tm = ybuf.shape[1]

    def send_previous():
        done = ybuf.at[(i + 1) % 2]
        for r in range(tm):
            pltpu.make_async_copy(done.at[pl.ds(r, 1)], y4_ref.at[pl.ds(inv_ref[0, r], 1)], sem).start()

    def drain():
        pltpu.make_async_copy(y4_ref.at[pl.ds(0, tm)], y4_ref.at[pl.ds(0, tm)], sem).wait()

    @pl.when(i == 0)
    def _():
        ybuf[1] = jnp.zeros(ybuf.shape[1:], ybuf.dtype)
    fresh = jnp.logical_or(i == 0, te_ref[i] != te_ref[jnp.maximum(i - 1, 0)])
    n_groups = w1_ref.shape[1] // (2 * LANES)

    @pl.when(jnp.logical_and(active, fresh))
    def _():
        for c in range(n_groups):
            cols = slice(c * 2 * LANES, (c + 1) * 2 * LANES)
            w1s[:, cols] = _dot(w1_ref[:, cols].astype(BF16), perm_ref[...]).astype(BF16)
        w2s[...] = w2_ref[...].astype(BF16)

    @pl.when(active)
    def _():
        send_previous()
        u = _dot(xs_ref[...].astype(BF16), w1s[...]) + b1_ref[...]
        acts = []
        for c in range(n_groups):
            glu = jnp.minimum(u[:, c * 2 * LANES:c * 2 * LANES + LANES], SWIGLU_LIMIT)
            lin = jnp.clip(u[:, c * 2 * LANES + LANES:(c + 1) * 2 * LANES], -SWIGLU_LIMIT, SWIGLU_LIMIT)
            acts.append((glu * _sigmoid(SWIGLU_ALPHA * glu) * (lin + 1.0)).astype(BF16))
        ybuf[i % 2] = _dot(jnp.concatenate(acts, axis=-1), w2s[...]) + b2_ref[...]
        drain()

    @pl.when(i == na_ref[0])
    def _():
        send_previous()
        drain()


def _expert_ffn(tile_e, n_active, inv, n_out, xs, w1, b1, w2, b2, layer):
    p, d = xs.shape
    tm = MOE_TILE
    de2 = w1.shape[3]
    de = de2 // 2
    sel = np.zeros((2 * LANES, 2 * LANES), np.float32)
    sel[2 * np.arange(LANES), np.arange(LANES)] = 1.0
    sel[2 * np.arange(LANES) + 1, LANES + np.arange(LANES)] = 1.0
    b1g = b1.reshape(N_EXPERTS, de2 // (2 * LANES), LANES, 2).transpose(0, 1, 3, 2).reshape(N_EXPERTS, 1, de2)
    n_tiles = p // tm
    tiles = jnp.arange(n_tiles, dtype=jnp.int32)
    live = tiles < n_active[0]
    te_live = jnp.where(live, tile_e, jnp.max(jnp.where(live, tile_e, 0)))
    first = jnp.concatenate([jnp.ones((1,), bool), te_live[1:] != te_live[:-1]])
    later = jnp.where(te_live[None, :] > te_live[:, None], te_live[None, :], N_EXPERTS)
    following = jnp.min(later, axis=1)
    w_res = jnp.where(first | (following == N_EXPERTS), te_live, following).astype(jnp.int32)
    wmap = lambda i, te, na, wr: (te[i], 0, 0)
    lmap = lambda i, te, na, wr: (layer, wr[i], 0, 0)
    xmap = lambda i, te, na, wr: (jnp.minimum(i, jnp.maximum(na[0] - 1, 0)), 0)
    imap = lambda i, te, na, wr: (jnp.clip(i - 1, 0, jnp.maximum(na[0] - 1, 0)), 0, 0)
    grid_spec = pltpu.PrefetchScalarGridSpec(
        num_scalar_prefetch=3,
        grid=(p // tm,),
        in_specs=[pl.BlockSpec((None, 1, tm), imap, memory_space=pltpu.SMEM),
                  pl.BlockSpec((tm, d), xmap),
                  pl.BlockSpec((None, None, d, de2), lmap), pl.BlockSpec((None, 1, de2), wmap),
                  pl.BlockSpec((None, None, de, d), lmap), pl.BlockSpec((None, 1, d), wmap),
                  pl.BlockSpec((2 * LANES, 2 * LANES), lambda i, te, na, wr: (0, 0))],
        out_specs=pl.BlockSpec(memory_space=pl.ANY),
        scratch_shapes=[pltpu.VMEM((d, de2), BF16), pltpu.VMEM((de, d), BF16), pltpu.VMEM((2, tm, d), F32),
                        pltpu.SemaphoreType.DMA(())],
    )
    return pl.pallas_call(
        _expert_kernel,
        grid_spec=grid_spec,
        out_shape=jax.ShapeDtypeStruct((n_out, d), F32),
        compiler_params=pltpu.CompilerParams(dimension_semantics=("arbitrary",), vmem_limit_bytes=VMEM_LIMIT,
                                             has_side_effects=True),
        name="expert_ffn",
    )(te_live, n_active, w_res, inv.reshape(n_tiles, 1, tm), xs, w1, b1g, w2, b2.reshape(N_EXPERTS, 1, d), jnp.asarray(sel, BF16))


def _route_plan(topi, rank, counts, tm):
    n, k = topi.shape
    padded = ((counts + tm - 1) // tm) * tm
    pend = jnp.cumsum(padded)
    pstart = pend - padded
    experts = jnp.arange(N_EXPERTS, dtype=jnp.int32)
    start = jnp.sum(jnp.where(topi[:, :, None] == experts, pstart, 0), axis=-1)
    dest = (start + rank).astype(jnp.int32)
    p = n * k + N_EXPERTS * tm
    tile_start = jnp.arange(p // tm, dtype=jnp.int32) * tm
    tile_e = jnp.minimum(jnp.sum((tile_start[:, None] >= pend[None, :]).astype(jnp.int32), axis=1), N_EXPERTS - 1)
    n_active = (pend[-1:] // tm).astype(jnp.int32)
    return dest, tile_e, n_active, p, jnp.stack([pstart, pend]).astype(jnp.int32)


def _dispatch_kernel(bounds_ref, dest_ref, x_ref, out_ref, inv_ref, zeros, sem, zsem, *, tb, n_tok):
    n_rows = tb * TOP_K

    @pl.when(pl.program_id(0) == 0)
    def _():
        zeros[...] = jnp.zeros_like(zeros)

        def fill(e):
            start = pl.multiple_of(bounds_ref[1, e] - MOE_TILE, MOE_TILE)
            return pltpu.make_async_copy(zeros, out_ref.at[pl.ds(start, MOE_TILE)], zsem)

        for e in range(N_EXPERTS):
            @pl.when(bounds_ref[1, e] > bounds_ref[0, e])
            def _():
                fill(e).start()
        for e in range(N_EXPERTS):
            @pl.when(bounds_ref[1, e] > bounds_ref[0, e])
            def _():
                fill(e).wait()
        for e in range(N_EXPERTS):
            @pl.when(bounds_ref[1, e] > bounds_ref[0, e])
            def _():
                tile = bounds_ref[1, e] // MOE_TILE - 1
                spare = TOP_K * n_tok + (tile % 2) * MOE_TILE

                def mark(c, carry):
                    inv_ref[tile, c] = spare + c
                    return carry

                lax.fori_loop(0, MOE_TILE, mark, 0)

    token0 = pl.program_id(0) * tb

    def issue(t, carry):
        for k in range(TOP_K):
            row = dest_ref[0, t * TOP_K + k]
            pltpu.make_async_copy(x_ref.at[pl.ds(t, 1)], out_ref.at[pl.ds(row, 1)], sem).start()
            inv_ref[row // MOE_TILE, row % MOE_TILE] = k * n_tok + token0 + t
        return carry

    lax.fori_loop(0, tb, issue, 0, unroll=4)
    pltpu.make_async_copy(out_ref.at[pl.ds(0, n_rows)], out_ref.at[pl.ds(0, n_rows)], sem).wait()


def _dispatch(h2, dest, bounds, p_rows, n_tok):
    d = h2.shape[1]
    tb = ROW_TILE
    steps = n_tok // tb
    return pl.pallas_call(
        functools.partial(_dispatch_kernel, tb=tb, n_tok=n_tok),
        grid=(steps,),
        in_specs=[pl.BlockSpec(memory_space=pltpu.SMEM),
                  pl.BlockSpec((None, 1, tb * TOP_K), lambda i: (i, 0, 0), memory_space=pltpu.SMEM),
                  pl.BlockSpec((tb, d), lambda i: (i, 0))],
        out_specs=[pl.BlockSpec(memory_space=pl.ANY), pl.BlockSpec(memory_space=pltpu.SMEM)],
        out_shape=[jax.ShapeDtypeStruct((p_rows, d), F32),
                   jax.ShapeDtypeStruct((p_rows // MOE_TILE, MOE_TILE), jnp.int32)],
        scratch_shapes=[pltpu.VMEM((MOE_TILE, d), F32), pltpu.SemaphoreType.DMA(()), pltpu.SemaphoreType.DMA(())],
        compiler_params=pltpu.CompilerParams(dimension_semantics=("arbitrary",), has_side_effects=True),
        name="moe_dispatch",
    )(bounds, dest.reshape(steps, 1, tb * TOP_K), h2)


def _combine_kernel(x_ref, y0_ref, y1_ref, y2_ref, y3_ref, tw_ref, mod_ref, g_ref, o_ref):
    y = None
    for k, y_ref in enumerate((y0_ref, y1_ref, y2_ref, y3_ref)):
        t = y_ref[...] * tw_ref[:, k:k + 1]
        y = t if y is None else y + t
    o_ref[...] = x_ref[...] + mod_ref[5:6, :] * (_rms(y) * g_ref[...])


def _combine_residual(n_tiles, x, y4, topw, mod, g, geo):
    nt, d = x.shape
    tb = ROW_TILE
    lat_tiles, per_batch, batch = geo["lat_tiles"], geo["tiles_per_batch"], geo["batch"]

    def mod_map(i):
        return (jnp.where(i < lat_tiles, i // per_batch, batch), 0, 0)

    row = lambda i: (i, 0)
    plane = lambda k: pl.BlockSpec((tb, d), lambda i: (k * n_tiles + i, 0))
    return pl.pallas_call(
        _combine_kernel,
        grid=(n_tiles,),
        in_specs=[pl.BlockSpec((tb, d), row)] + [plane(k) for k in range(TOP_K)]
                 + [pl.BlockSpec((tb, LANES), row), pl.BlockSpec((None, 6, d), mod_map),
                    pl.BlockSpec((1, d), lambda i: (0, 0))],
        out_specs=pl.BlockSpec((tb, d), row),
        out_shape=jax.ShapeDtypeStruct((n_tiles * tb, d), F32),
        compiler_params=_cparams(("arbitrary",)),
        name="combine_residual",
    )(x, y4, y4, y4, y4, topw, mod, g)


def _rope_partner(dim):
    h, q = dim // 2, dim // 4
    perm = np.zeros(dim, np.int32)
    sign = np.zeros(dim, np.float32)
    for base in (0, h):
        for j in range(q):
            perm[base + j], sign[base + j] = base + j + q, -1.0
            perm[base + q + j], sign[base + q + j] = base + j, 1.0
    return perm, sign


def _rot_cols(w, dim):
    perm, sign = _rope_partner(dim)
    reps = w.shape[1] // dim
    full_perm = np.concatenate([perm + r * dim for r in range(reps)])
    return w[:, full_perm] * jnp.asarray(np.tile(sign, reps))


def _rope_tables(n_tok, dim, reps, pad_rows):
    rows = n_tok // GRID_W
    row = jnp.repeat(jnp.arange(rows, dtype=F32), GRID_W)
    col = jnp.tile(jnp.arange(GRID_W, dtype=F32), rows)
    quarter = dim // 4
    inv = ROPE_BASE ** (-jnp.arange(quarter, dtype=F32) / quarter)
    ar = row[:, None] * inv[None]
    ac = col[:, None] * inv[None]
    ang = jnp.concatenate([ar, ar, ac, ac], axis=-1)
    cos = jnp.concatenate([jnp.tile(jnp.cos(ang), (1, reps)), jnp.ones((pad_rows, dim * reps), F32)])
    sin = jnp.concatenate([jnp.tile(jnp.sin(ang), (1, reps)), jnp.zeros((pad_rows, dim * reps), F32)])
    return cos, sin


def _widen_w_in(w):
    sizes = (256, 256, 256, 768, 256, 256, 256, 256, 128, 128, 256, 256, 32)
    cuts = np.cumsum(sizes)[:-1].tolist()
    aq, ak, av, bu, cq, ck, cv, cg, dq, dk, dv, dr, dl = jnp.split(w, cuts, axis=1)
    dl = jnp.pad(dl, ((0, 0), (0, LANES - dl.shape[1])))
    parts = [aq, _rot_cols(aq, DIFF_DQK), ak, _rot_cols(ak, DIFF_DQK), av, bu,
             cq, _rot_cols(cq, RET_DH), ck, _rot_cols(ck, RET_DH), cv, cg, dq, dk, dv, dr, dl]
    return jnp.concatenate(parts, axis=1).astype(BF16)


def kernel(x, c, ctx, c_ctx, w_mod, b_mod, norm_g, w_in, w_out, diff_lambda, diff_subln_g, hy_short_w, hy_short_b, hy_w1, hy_b1, hy_w2, hy_b2, hy_w3, hy_freq, hy_bias, ret_decay, ret_norm_g, gla_gate_w, gla_gate_b, gla_norm_g, router_w, router_b, exp_w1, exp_b1, exp_w2, exp_b2):
    batch, seq, d = x.shape
    n_ctx = ctx.shape[1]
    depth = w_mod.shape[0]
    n_lat_rows, n_ctx_rows = batch * seq, batch * n_ctx
    nt = n_lat_rows + n_ctx_rows
    assert d == D_MODEL and seq % ATT_Q_TILE == 0 and n_ctx % ROW_TILE == 0 and seq % n_ctx == 0
    geo = dict(batch=batch, seq=seq, ctx=n_ctx, lat_tiles=n_lat_rows // ROW_TILE,
               tiles_per_batch=seq // ROW_TILE)

    xs = jnp.concatenate([x.reshape(n_lat_rows, d), ctx.reshape(n_ctx_rows, d)], axis=0)
    mod_rows = 8
    cc = jnp.zeros((mod_rows, d), F32).at[:batch].set(c).at[batch].set(c_ctx)
    mod_all = _modulation(cc, w_mod, b_mod).reshape(depth, mod_rows, 6, d)

    rope = (*_rope_tables(seq, DIFF_DQK, GROUP_W // DIFF_DQK, ROW_TILE),
            *_rope_tables(seq, RET_DH, GROUP_W // RET_DH, ROW_TILE))
    dft_lat = _dft_tables(seq)
    dft_ctx = _dft_tables(n_ctx)
    seg = jnp.asarray(np.kron(np.eye(N_HEADS, dtype=np.float32), np.ones((RET_DH, RET_DH), np.float32)))

    for l in range(depth):
        need_ctx = l < depth - 1
        lam_init = 0.8 - 0.6 * math.exp(-0.3 * l)
        mod = mod_all[l]
        (aq, ak, av, bu, cq, ck, cv, cg, dq, dk, dv, dr, dl) = _in_projection(
            xs, mod, norm_g[l, 0].reshape(1, d), _widen_w_in(w_in[l]), rope, geo)

        sub_g = diff_subln_g[l].reshape(1, DIFF_DV)
        at = _diff_attention(diff_lambda[l], sub_g, aq, ak, av, geo, lam_init)
        if need_ctx:
            at = _diff_attention(diff_lambda[l], sub_g, aq, ak, av, geo, lam_init, prev=at)

        def hyena(n, row_blk0, tables, out_blk0, prev):
            e, dd, nyqf = _hyena_filter(n, hy_w1[l], hy_b1[l], hy_w2[l], hy_b2[l], hy_w3[l], hy_freq[l])
            zb, x0c, zf, nyqz = _hyena_gate(bu, hy_short_w[l], hy_short_b[l], n, batch, row_blk0)
            return _hyena_conv(tables, zb, e, dd, x0c, zf, nyqz, nyqf, hy_bias[l], n, batch, nt, out_blk0, prev)

        hy = hyena(seq, 0, dft_lat, 0, None)
        if need_ctx:
            hy = hyena(n_ctx, n_lat_rows // n_ctx, dft_ctx, n_lat_rows // min(512, n_ctx), hy)

        dec = jnp.repeat(ret_decay[l], RET_DH, axis=-1).reshape(2, 1, N_HEADS * RET_DH)
        rf, rb = _retention(cq, ck, cv, dec, geo)

        gw = jnp.zeros((2, LANES, N_HEADS * GLA_DK), F32)
        gw = gw.at[0, :GLA_RANK].set(gla_gate_w[l, 0]).at[1, GLA_RANK:2 * GLA_RANK].set(gla_gate_w[l, 1])
        gbias = gla_gate_b[l].reshape(2, 1, N_HEADS * GLA_DK)
        gf, gb = _gla(dq, dk, dv, dl, gw, gbias, geo)

        n_tiles = nt // ROW_TILE if need_ctx else n_lat_rows // ROW_TILE
        rw = jnp.pad(router_w[l], ((0, 0), (0, LANES - N_EXPERTS)))
        rbias = jnp.pad(router_b[l], (0, LANES - N_EXPERTS), constant_values=-jnp.inf).reshape(1, LANES)
        x1, h2, topi, topw, rank, cnt = _mix_out(
            n_tiles, xs, mod, at, hy, rf, rb, cg, gf, gb, dr, seg,
            ret_norm_g[l].reshape(1, GROUP_W), jnp.tile(gla_norm_g[l], N_HEADS).reshape(1, GROUP_W),
            w_out[l].astype(BF16), norm_g[l, 1].reshape(1, d), norm_g[l, 2].reshape(1, d), rw, rbias, geo)

        n_tok = n_tiles * ROW_TILE
        dest, tile_e, n_active, p_rows, bounds = _route_plan(
            topi[:n_tok, :TOP_K], rank[:n_tok, :TOP_K], cnt[0, :N_EXPERTS].astype(jnp.int32), MOE_TILE)
        dispatched, inv = _dispatch(h2, dest, bounds, p_rows, n_tok)
        y4 = _expert_ffn(tile_e, n_active, inv, TOP_K * n_tok + 2 * MOE_TILE, dispatched,
                         exp_w1, exp_b1[l], exp_w2, exp_b2[l], l)
        xs = _combine_residual(n_tiles, x1, y4, topw, mod, norm_g[l, 3].reshape(1, d), geo)

    return xs[:n_lat_rows].reshape(batch, seq, d)
```

```python
import functools
import math

import numpy as np
import jax
import jax.numpy as jnp
from jax import lax
from jax.experimental import pallas as pl
from jax.experimental.pallas import tpu as pltpu

F32 = jnp.float32
BF16 = jnp.bfloat16
HIGHEST = lax.Precision.HIGHEST

D_MODEL = 1024
GRID_W = 64
GROUP_W = 256
N_HEADS = 4
DIFF_DQK = 32
DIFF_DV = 64
ROPE_BASE = 10000.0
HY_CH = 256
HY_BANDS = 16
HY_FFN = 64
HY_FAST_DECAY_PCT = 0.3
HY_SLOW_DECAY_PCT = 1.5
HY_DECAY_TARGET = 1e-2
RET_DH = 64
GLA_DK = 32
GLA_DV = 64
GLA_RANK = 16
GLA_GATE_NORM = 16.0
N_EXPERTS = 32
TOP_K = 4
D_EXPERT = 1024
SWIGLU_LIMIT = 7.0
SWIGLU_ALPHA = 1.702
EPS = 1e-6

LANES = 128
ROW_TILE = 256
REC_CHUNK = 128
RET_CHUNK = 256
REC_SUB = 32
EXP_CLAMP = 80.0
MOE_TILE = 256
ATT_KEY_CHUNK = 512
ATT_Q_TILE = 512
DFT_TABLE_ROWS = 128
VMEM_LIMIT = 52 * 1024 * 1024

_A_Q, _A_K, _A_V = 0, 256, 512
_B_U = 768
_C_Q, _C_K, _C_V, _C_G = 1536, 1792, 2048, 2304
_D_QK, _D_V, _D_R, _D_L = 2560, 2816, 3072, 3328
_IN_COLS = 3456


def _cparams(sem):
    return pltpu.CompilerParams(dimension_semantics=sem, vmem_limit_bytes=VMEM_LIMIT)


def _sigmoid(x):
    return 1.0 / (1.0 + jnp.exp(-x))


def _rms(x):
    return x * lax.rsqrt(jnp.mean(x * x, axis=-1, keepdims=True) + EPS)


def _iota(shape, axis):
    return lax.broadcasted_iota(jnp.int32, shape, axis)


def _dot(a, b):
    return jnp.dot(a, b, preferred_element_type=F32)


def _dot_exact(a, b):
    return jnp.dot(a, b, preferred_element_type=F32, precision=HIGHEST)


def _dot_nt(a, b):
    return lax.dot_general(a, b, (((1,), (1,)), ((), ())), preferred_element_type=F32)


def _dot_tn(a, b):
    return lax.dot_general(a, b, (((0,), (0,)), ((), ())), preferred_element_type=F32)


def _mod_kernel(c_ref, w_ref, b_ref, o_ref):
    c = c_ref[...]
    o_ref[...] = _dot_exact(c * _sigmoid(c), w_ref[...]) + b_ref[...]


def _modulation(cc, w_mod, b_mod):
    depth, d, n = w_mod.shape
    tn = n // 4
    rows = cc.shape[0]
    return pl.pallas_call(
        _mod_kernel,
        grid=(depth, n // tn),
        in_specs=[pl.BlockSpec((rows, d), lambda l, j: (0, 0)),
                  pl.BlockSpec((None, d, tn), lambda l, j: (l, 0, j)),
                  pl.BlockSpec((None, 1, tn), lambda l, j: (l, 0, j))],
        out_specs=pl.BlockSpec((None, rows, tn), lambda l, j: (l, 0, j)),
        out_shape=jax.ShapeDtypeStruct((depth, rows, n), F32),
        compiler_params=_cparams(("arbitrary", "arbitrary")),
        name="modulation",
    )(cc, w_mod, b_mod.reshape(depth, 1, n))


def _inproj_kernel(x_ref, mod_ref, g_ref, w_ref, cosa_ref, sina_ref, cosc_ref, sinc_ref,
                   aq_ref, ak_ref, av_ref, bu_ref, cq_ref, ck_ref, cv_ref, cg_ref,
                   dq_ref, dk_ref, dv_ref, dr_ref, dl_ref):
    xn = _rms(x_ref[...]) * g_ref[...]
    h = (xn * (1.0 + mod_ref[1:2, :]) + mod_ref[0:1, :]).astype(BF16)

    def proj(a, width):
        return _dot(h, w_ref[:, a:a + width])

    def roped(a, dim, cos_ref, sin_ref):
        p = proj(a, GROUP_W)
        quarter = dim // 4
        lane = _iota(p.shape, 1)
        partner = jnp.where(lane % (dim // 2) < quarter, -pltpu.roll(p, GROUP_W - quarter, 1),
                            pltpu.roll(p, quarter, 1))
        return p * cos_ref[...] + partner * sin_ref[...]

    aq = roped(_A_Q, DIFF_DQK, cosa_ref, sina_ref) * (DIFF_DQK ** -0.5 * math.log2(math.e))
    ak = roped(_A_K, DIFF_DQK, cosa_ref, sina_ref)
    av = proj(_A_V, GROUP_W)
    ones_col = jnp.where(lax.broadcasted_iota(jnp.int32, (av.shape[0], LANES - DIFF_DV), 1) == 0, 1.0, 0.0)
    for hd in range(N_HEADS):
        sl = slice(hd * DIFF_DV, (hd + 1) * DIFF_DV)
        aq_ref[hd] = aq[:, sl].astype(BF16)
        ak_ref[hd] = ak[:, sl].astype(BF16)
        av_ref[hd] = jnp.concatenate([av[:, sl], ones_col], axis=-1).astype(BF16)
    bu_ref[...] = proj(_B_U, 3 * HY_CH)
    cq_ref[...] = roped(_C_Q, RET_DH, cosc_ref, sinc_ref)
    ck_ref[...] = roped(_C_K, RET_DH, cosc_ref, sinc_ref)
    cv_ref[...] = proj(_C_V, GROUP_W)
    cg_ref[...] = proj(_C_G, GROUP_W)
    dqk = proj(_D_QK, 2 * LANES)
    dq_ref[...] = dqk[:, :LANES]
    dk_ref[...] = dqk[:, LANES:]
    dv_ref[...] = proj(_D_V, GROUP_W)
    dr_ref[...] = proj(_D_R, GROUP_W)
    dl_ref[...] = proj(_D_L, LANES)


def _in_projection(x, mod, g, w_wide, rope, geo):
    nt, d = x.shape
    tm = ROW_TILE
    lat_tiles, per_batch, batch = geo["lat_tiles"], geo["tiles_per_batch"], geo["batch"]

    def mod_map(i):
        return (jnp.where(i < lat_tiles, i // per_batch, batch), 0, 0)

    def rope_map(i):
        return (jnp.where(i < lat_tiles, i % per_batch, per_batch), 0)

    row = lambda i: (i, 0)
    head = lambda i: (0, i, 0)
    const = lambda i: (0, 0)
    f32_out = lambda w: jax.ShapeDtypeStruct((nt, w), F32)
    head_out = lambda w: jax.ShapeDtypeStruct((N_HEADS, nt, w), BF16)
    head_spec = lambda w: pl.BlockSpec((N_HEADS, tm, w), head)
    widths = [3 * HY_CH, GROUP_W, GROUP_W, GROUP_W, GROUP_W, LANES, LANES, GROUP_W, GROUP_W, LANES]
    return pl.pallas_call(
        _inproj_kernel,
        grid=(nt // tm,),
        in_specs=[pl.BlockSpec((tm, d), row),
                  pl.BlockSpec((None, 6, d), mod_map),
                  pl.BlockSpec((1, d), const),
                  pl.BlockSpec((d, _IN_COLS), const, pipeline_mode=pl.Buffered(1)),
                  pl.BlockSpec((tm, GROUP_W), rope_map), pl.BlockSpec((tm, GROUP_W), rope_map),
                  pl.BlockSpec((tm, GROUP_W), rope_map), pl.BlockSpec((tm, GROUP_W), rope_map)],
        out_specs=[head_spec(DIFF_DV), head_spec(DIFF_DV), head_spec(LANES)]
                  + [pl.BlockSpec((tm, w), row) for w in widths],
        out_shape=[head_out(DIFF_DV), head_out(DIFF_DV), head_out(LANES)] + [f32_out(w) for w in widths],
        compiler_params=_cparams(("arbitrary",)),
        name="in_projection",
    )(x, mod, g, w_wide, *rope)


def _attn_kernel(lam_ref, g_ref, q_ref, *rest, lam_init, has_lat):
    if has_lat:
        kl_ref, vl_ref, kc_ref, vc_ref, o_ref = rest
        keys = [(kl_ref, vl_ref), (kc_ref, vc_ref)]
    else:
        kc_ref, vc_ref, o_ref = rest
        keys = [(kc_ref, vc_ref)]
    lp = lam_ref[...]
    lam = (jnp.exp(jnp.sum(lp[0:1] * lp[1:2], axis=-1, keepdims=True))
           - jnp.exp(jnp.sum(lp[2:3] * lp[3:4], axis=-1, keepdims=True)) + lam_init)
    q = q_ref[...]
    tq = q.shape[0]
    lane = lax.broadcasted_iota(jnp.int32, q.shape, 1)
    chunks = []
    for k_ref, v_ref in keys:
        size = min(ATT_KEY_CHUNK, k_ref.shape[0])
        chunks += [(k_ref, v_ref, s0, size) for s0 in range(0, k_ref.shape[0], size)]

    def lane_groups(t):
        return [t[:, c0:c0 + LANES] for c0 in range(0, t.shape[1], LANES)]

    qm = [jnp.where((lane >= m * DIFF_DQK) & (lane < (m + 1) * DIFF_DQK), q, jnp.zeros_like(q)) for m in range(2)]

    def score(m, j):
        k_ref, _, s0, size = chunks[j]
        return _dot_nt(qm[m], k_ref[s0:s0 + size, :])

    def row_max(scores):
        wide = functools.reduce(jnp.maximum, [g for t in scores for g in lane_groups(t)])
        return jnp.max(wide, axis=-1, keepdims=True)

    def weighted(m, j, s, mx, acc):
        _, v_ref, s0, size = chunks[j]
        return acc + _dot(jnp.exp2(s - mx).astype(BF16), v_ref[s0:s0 + size, :])

    n = len(chunks)
    s1 = [score(0, j) for j in range(n)]
    mx1 = row_max(s1)
    s2 = []
    acc1 = jnp.zeros((tq, LANES), F32)
    for j in range(n):
        s2.append(score(1, j))
        acc1 = weighted(0, j, s1[j], mx1, acc1)
    mx2 = row_max(s2)
    acc2 = jnp.zeros((tq, LANES), F32)
    for j in range(n):
        acc2 = weighted(1, j, s2[j], mx2, acc2)
    o = (acc1[:, :DIFF_DV] * (1.0 / acc1[:, DIFF_DV:DIFF_DV + 1])
         - lam * (acc2[:, :DIFF_DV] * (1.0 / acc2[:, DIFF_DV:DIFF_DV + 1])))
    o_ref[...] = _rms(o) * g_ref[...] * (1.0 - lam_init)


def _diff_attention(lam_p, subln_g, aq, ak, av, geo, lam_init, prev=None):
    batch, seq, ctx = geo["batch"], geo["seq"], geo["ctx"]
    nt = aq.shape[1]
    has_lat = prev is None
    tq = ATT_Q_TILE if has_lat else ROW_TILE
    n_q = (seq if has_lat else ctx) // tq
    q_off = 0 if has_lat else (batch * seq) // tq
    ctx_blk0 = (batch * seq) // ctx

    qmap = lambda b, h, i: (h, q_off + b * n_q + i, 0)
    lat_map = lambda b, h, i: (h, b, 0)
    ctx_map = lambda b, h, i: (h, ctx_blk0 + b, 0)
    const = lambda b, h, i: (0, 0)
    in_specs = [pl.BlockSpec((4, DIFF_DQK), const), pl.BlockSpec((1, DIFF_DV), const),
                pl.BlockSpec((None, tq, DIFF_DV), qmap)]
    args = [lam_p, subln_g, aq]
    if has_lat:
        in_specs += [pl.BlockSpec((None, seq, DIFF_DV), lat_map), pl.BlockSpec((None, seq, LANES), lat_map)]
        args += [ak, av]
    in_specs += [pl.BlockSpec((None, ctx, DIFF_DV), ctx_map), pl.BlockSpec((None, ctx, LANES), ctx_map)]
    args += [ak, av]
    aliases = {}
    if not has_lat:
        in_specs.append(pl.BlockSpec(memory_space=pl.ANY))
        args.append(prev)
        aliases = {len(args) - 1: 0}
    kern = functools.partial(_attn_kernel, lam_init=lam_init, has_lat=has_lat)
    if not has_lat:
        kern = _drop_last_input(kern, n_in=len(args))
    return pl.pallas_call(
        kern,
        grid=(batch, N_HEADS, n_q),
        in_specs=in_specs,
        out_specs=pl.BlockSpec((None, tq, DIFF_DV), qmap),
        out_shape=jax.ShapeDtypeStruct((N_HEADS, nt, DIFF_DV), F32),
        input_output_aliases=aliases,
        compiler_params=_cparams(("arbitrary", "arbitrary", "arbitrary")),
        name="diff_attention" if has_lat else "diff_attention_ctx",
    )(*args)


def _drop_last_input(kern, n_in):
    def wrapped(*refs):
        return kern(*refs[:n_in - 1], *refs[n_in:])
    return wrapped


def _dft_table_kernel(cx_ref, sx_ref, cy_ref, sy_ref, c_ref, s_ref, *, n, span):
    t = _iota((LANES, n), 1)
    row = _iota((LANES, n), 0)
    pick_a = jnp.where(t // span == row, 1.0, 0.0).astype(BF16)
    pick_b = jnp.where(t % span == row, 1.0, 0.0).astype(BF16)

    def widen(ref, pick):
        x = ref[...]
        hi = x.astype(BF16)
        lo = (x - hi.astype(F32)).astype(BF16)
        return _dot(hi, pick) + _dot(lo, pick)

    cx, sx = widen(cx_ref, pick_a), widen(sx_ref, pick_a)
    cy, sy = widen(cy_ref, pick_b), widen(sy_ref, pick_b)
    c_ref[...] = (cx * cy - sx * sy).astype(BF16)
    s_ref[...] = (sx * cy + cx * sy).astype(BF16)


def _dft_tables(n):
    span = 64
    assert n % span == 0 and n // span <= LANES
    k = jnp.arange(n, dtype=jnp.int32)[:, None]
    j = jnp.arange(LANES, dtype=jnp.int32)[None, :]

    def factor(step, count):
        ang = ((k * (j * step)) % (2 * n)).astype(F32) * (math.pi / n)
        live = j < count
        return jnp.where(live, jnp.cos(ang), 0.0), jnp.where(live, jnp.sin(ang), 0.0)

    cx, sx = factor(span, n // span)
    cy, sy = factor(1, span)
    tk = min(DFT_TABLE_ROWS, n)
    small = pl.BlockSpec((tk, LANES), lambda i: (i, 0))
    big = pl.BlockSpec((tk, n), lambda i: (i, 0))
    return pl.pallas_call(
        functools.partial(_dft_table_kernel, n=n, span=span),
        grid=(n // tk,),
        in_specs=[small] * 4,
        out_specs=[big, big],
        out_shape=[jax.ShapeDtypeStruct((n, n), BF16)] * 2,
        compiler_params=_cparams(("arbitrary",)),
        name="dft_tables",
    )(cx, sx, cy, sy)


def _hy_filter_kernel(w1t_ref, w1c_ref, w1s_ref, b1_ref, w2_ref, b2_ref, w3_ref, fr_ref,
                      bands_ref, deltas_ref, e_ref, d_ref, nyq_ref, *, n):
    pos_i = lax.broadcasted_iota(jnp.int32, (n, 1), 0)
    pos = pos_i.astype(F32)
    t = pos / (n - 1)
    ang = ((2.0 * math.pi) * pos / n) * bands_ref[...]
    pre = t * w1t_ref[...] + _dot_exact(jnp.cos(ang), w1c_ref[...]) - _dot_exact(jnp.sin(ang), w1s_ref[...])
    hdn = jnp.sin(fr_ref[0:1, :] * (pre + b1_ref[...]))
    hdn = jnp.sin(fr_ref[1:2, :] * (_dot_exact(hdn, w2_ref[...]) + b2_ref[...]))
    raw = _dot_exact(hdn, w3_ref[...])
    window = jnp.exp(-t * deltas_ref[...])
    hf = raw[:, :HY_CH] * window
    hb = jnp.where(pos_i > 0, raw[:, HY_CH:] * window, 0.0)
    inv = 1.0 / (jnp.sum(jnp.abs(hf), axis=0, keepdims=True) + jnp.sum(jnp.abs(hb), axis=0, keepdims=True))
    e = (hf + hb) * inv
    e_ref[...] = e.astype(BF16)
    d_ref[...] = ((hb - hf) * inv).astype(BF16)
    sign = (1 - 2 * (pos_i & 1)).astype(F32)
    nyq_ref[...] = jnp.sum(e * sign, axis=0, keepdims=True)


def _hyena_filter(n, w1, b1, w2, b2, w3, freq):
    bands = jnp.linspace(1e-4, HY_BANDS - 1, HY_BANDS, dtype=F32).reshape(1, HY_BANDS)
    max_decay = math.log(HY_DECAY_TARGET) / HY_FAST_DECAY_PCT
    min_decay = math.log(HY_DECAY_TARGET) / HY_SLOW_DECAY_PCT
    deltas = jnp.abs(jnp.linspace(min_decay, max_decay, HY_CH, dtype=F32)).reshape(1, HY_CH)
    args = [w1[0:1], w1[1:1 + HY_BANDS], w1[1 + HY_BANDS:], b1.reshape(1, HY_FFN), w2, b2.reshape(1, HY_FFN),
            w3, freq, bands, deltas]
    return pl.pallas_call(
        functools.partial(_hy_filter_kernel, n=n),
        out_shape=[jax.ShapeDtypeStruct((n, HY_CH), BF16), jax.ShapeDtypeStruct((n, HY_CH), BF16),
                   jax.ShapeDtypeStruct((1, HY_CH), F32)],
        compiler_params=pltpu.CompilerParams(vmem_limit_bytes=VMEM_LIMIT),
        name="hyena_filter",
    )(*args)


def _hy_gate_kernel(x0_ref, x1_ref, v_ref, w0_ref, w1_ref, wv_ref, b0_ref, b1_ref, bv_ref,
                    zb_ref, x0c_ref, zf_ref, nyq_ref, *, n):
    row = lax.broadcasted_iota(jnp.int32, (n, 1), 0)

    def conv(u_ref, w_ref, b_ref):
        u = u_ref[...]
        up = jnp.where(row > 0, pltpu.roll(u, 1, 0), 0.0)
        dn = jnp.where(row < n - 1, pltpu.roll(u, n - 1, 0), 0.0)
        return up * w_ref[0:1, :] + u * w_ref[1:2, :] + dn * w_ref[2:3, :] + b_ref[...]

    z = conv(x1_ref, w1_ref, b1_ref) * conv(v_ref, wv_ref, bv_ref)
    x0c_ref[...] = conv(x0_ref, w0_ref, b0_ref)
    zf_ref[...] = z
    zb_ref[...] = z.astype(BF16)
    sign = (1 - 2 * (row & 1)).astype(F32)
    nyq_ref[...] = jnp.sum(z * sign, axis=0, keepdims=True)


def _hyena_gate(bu, short_w, short_b, n, batch, row_blk0):
    halves = HY_CH // LANES
    sb = short_b.reshape(1, 3 * HY_CH)
    seg = lambda part: pl.BlockSpec((n, LANES), lambda b, j: (row_blk0 + b, part * halves + j))
    wsp = lambda part: pl.BlockSpec((3, LANES), lambda b, j: (0, part * halves + j))
    bsp = lambda part: pl.BlockSpec((1, LANES), lambda b, j: (0, part * halves + j))
    return pl.pallas_call(
        functools.partial(_hy_gate_kernel, n=n),
        grid=(batch, halves),
        in_specs=[seg(0), seg(1), seg(2), wsp(0), wsp(1), wsp(2), bsp(0), bsp(1), bsp(2)],
        out_specs=[pl.BlockSpec((n, LANES), lambda b, j: (0, b * halves + j)),
                   pl.BlockSpec((n, LANES), lambda b, j: (b, j)),
                   pl.BlockSpec((n, LANES), lambda b, j: (b, j)),
                   pl.BlockSpec((None, 1, LANES), lambda b, j: (b, 0, j))],
        out_shape=[jax.ShapeDtypeStruct((n, batch * HY_CH), BF16),
                   jax.ShapeDtypeStruct((batch * n, HY_CH), F32),
                   jax.ShapeDtypeStruct((batch * n, HY_CH), F32),
                   jax.ShapeDtypeStruct((batch, 1, HY_CH), F32)],
        compiler_params=_cparams(("arbitrary", "arbitrary")),
        name="hyena_gate",
    )(bu, bu, bu, short_w, short_w, short_w, sb, sb, sb)


def _hy_spectrum_kernel(c_ref, s_ref, z_ref, e_ref, d_ref, yr_ref, yi_ref, *, n, tk, batch):
    c = c_ref[...]
    s = s_ref[...]
    zr = _dot(c, z_ref[...])
    zs = _dot(s, z_ref[...])
    fr = _dot(c, e_ref[...])
    fi = _dot(s, d_ref[...])
    k = pl.program_id(0) * tk + lax.broadcasted_iota(jnp.int32, (tk, 1), 0)
    wk = jnp.where(k == 0, 1.0, 2.0) * (1.0 / (2 * n))
    for b in range(batch):
        sl = slice(b * HY_CH, (b + 1) * HY_CH)
        yr = zr[:, sl] * fr + zs[:, sl] * fi
        yi = zr[:, sl] * fi - zs[:, sl] * fr
        yr_ref[:, sl] = (yr * wk).astype(BF16)
        yi_ref[:, sl] = (-(yi * wk)).astype(BF16)


def _hy_inverse_kernel(c_ref, s_ref, yr_ref, yi_ref, x0c_ref, zf_ref, nyqz_ref, nyqf_ref, bias_ref,
                       *rest, n, tt, batch):
    o_ref, y_scr = rest[-2], rest[-1]
    b = pl.program_id(1)

    @pl.when(b == 0)
    def _():
        y = _dot(c_ref[...], yr_ref[...]) + _dot(s_ref[...], yi_ref[...])
        for bb in range(batch):
            y_scr[bb] = y[:, bb * HY_CH:(bb + 1) * HY_CH]

    t = pl.program_id(0) * tt + lax.broadcasted_iota(jnp.int32, (tt, 1), 0)
    sign = (1 - 2 * (t & 1)).astype(F32)
    nyq = nyqz_ref[...] * nyqf_ref[...] * (1.0 / (2 * n))
    zf = zf_ref[...]
    o_ref[...] = x0c_ref[...] * (y_scr[b] + sign * nyq + bias_ref[...] * zf)


def _hyena_conv(cs, zb, e, d, x0c, zf, nyqz, nyqf, bias, n, batch, nt, out_blk0, prev=None):
    c_tab, s_tab = cs
    bw = batch * HY_CH
    tk = min(512, n)
    whole = lambda shape: pl.BlockSpec(shape, lambda *_: (0,) * len(shape), pipeline_mode=pl.Buffered(1))
    yr, yi = pl.pallas_call(
        functools.partial(_hy_spectrum_kernel, n=n, tk=tk, batch=batch),
        grid=(n // tk,),
        in_specs=[pl.BlockSpec((tk, n), lambda i: (i, 0)), pl.BlockSpec((tk, n), lambda i: (i, 0)),
                  whole((n, bw)), whole((n, HY_CH)), whole((n, HY_CH))],
        out_specs=[pl.BlockSpec((tk, bw), lambda i: (i, 0))] * 2,
        out_shape=[jax.ShapeDtypeStruct((n, bw), BF16)] * 2,
        compiler_params=_cparams(("arbitrary",)),
        name="hyena_spectrum",
    )(c_tab, s_tab, zb, e, d)

    tt = min(512, n)
    n_t = n // tt
    seg = lambda i, b: (b * n_t + i, 0)
    in_specs = [pl.BlockSpec((tt, n), lambda i, b: (i, 0)), pl.BlockSpec((tt, n), lambda i, b: (i, 0)),
                whole((n, bw)), whole((n, bw)),
                pl.BlockSpec((tt, HY_CH), seg), pl.BlockSpec((tt, HY_CH), seg),
                pl.BlockSpec((None, 1, HY_CH), lambda i, b: (b, 0, 0)),
                pl.BlockSpec((1, HY_CH), lambda i, b: (0, 0)), pl.BlockSpec((1, HY_CH), lambda i, b: (0, 0))]
    args = [c_tab, s_tab, yr, yi, x0c, zf, nyqz, nyqf, bias.reshape(1, HY_CH)]
    aliases = {}
    if prev is not None:
        in_specs.append(pl.BlockSpec(memory_space=pl.ANY))
        args.append(prev)
        aliases = {len(args) - 1: 0}
    return pl.pallas_call(
        functools.partial(_hy_inverse_kernel, n=n, tt=tt, batch=batch),
        grid=(n_t, batch),
        in_specs=in_specs,
        out_specs=pl.BlockSpec((tt, HY_CH), lambda i, b: (out_blk0 + b * n_t + i, 0)),
        out_shape=jax.ShapeDtypeStruct((nt, HY_CH), F32),
        scratch_shapes=[pltpu.VMEM((batch, tt, HY_CH), F32)],
        input_output_aliases=aliases,
        compiler_params=_cparams(("arbitrary", "arbitrary")),
        name="hyena_inverse" if prev is None else "hyena_inverse_ctx",
    )(*args)


def _scan_maps(geo, cc):
    batch, seq, ctx = geo["batch"], geo["seq"], geo["ctx"]
    n_cc, n_lc = ctx // cc, seq // cc
    ctx0 = (batch * seq) // cc

    def fwd(b, i):
        return (jnp.where(i < n_cc, ctx0 + b * n_cc + i, b * n_lc + i - n_cc), 0)

    def bwd(b, i):
        return (jnp.where(i < n_cc, ctx0 + b * n_cc + (n_cc - 1 - i), b * n_lc + (n_lc - 1 - (i - n_cc))), 0)

    return fwd, bwd, n_cc + n_lc


def _ret_kernel(qf_ref, kf_ref, vf_ref, qb_ref, kb_ref, vb_ref, dec_ref, of_ref, ob_ref, st_ref):
    @pl.when(pl.program_id(1) == 0)
    def _():
        st_ref[...] = jnp.zeros_like(st_ref)

    cc = qf_ref.shape[0]
    r_i, c_i = _iota((cc, cc), 0), _iota((cc, cc), 1)
    pos = _iota((cc, 1), 0).astype(F32)
    for d, (q_ref, k_ref, v_ref, o_ref) in enumerate(((qf_ref, kf_ref, vf_ref, of_ref),
                                                      (qb_ref, kb_ref, vb_ref, ob_ref))):
        reverse = d == 1
        lg = -jnp.exp(dec_ref[d])
        steps_in = (cc - pos) if reverse else (pos + 1.0)
        steps_out = pos if reverse else (cc - 1.0 - pos)
        q = q_ref[...]
        k = k_ref[...] * (RET_DH ** -0.5)
        qd = (q * jnp.exp(steps_in * lg)).astype(BF16)
        kd = (k * jnp.exp(steps_out * lg)).astype(BF16)
        qb, kb, vb = q.astype(BF16), k.astype(BF16), v_ref[...].astype(BF16)
        gain = jnp.exp(cc * lg)
        dist = (c_i - r_i) if reverse else (r_i - c_i)
        keep = dist >= 0
        dist_f = jnp.where(keep, dist, 0).astype(F32)
        for h in range(N_HEADS):
            hs = slice(h * RET_DH, (h + 1) * RET_DH)
            decay = jnp.where(keep, jnp.exp(dist_f * lg[:, h * RET_DH:h * RET_DH + 1]), 0.0)
            sc = (_dot_nt(qb[:, hs], kb[:, hs]) * decay).astype(BF16)
            st = st_ref[d, h]
            o_ref[:, hs] = _dot(sc, vb[:, hs]) + _dot_nt(qd[:, hs], st.astype(BF16))
            st_ref[d, h] = st * gain[:, hs] + _dot_tn(vb[:, hs], kd[:, hs])


def _retention(q, k, v, dec, geo):
    nt, w = q.shape
    cc = RET_CHUNK
    fwd, bwd, steps = _scan_maps(geo, cc)
    blk = lambda m: pl.BlockSpec((cc, w), m)
    return pl.pallas_call(
        _ret_kernel,
        grid=(geo["batch"], steps),
        in_specs=[blk(fwd), blk(fwd), blk(fwd), blk(bwd), blk(bwd), blk(bwd),
                  pl.BlockSpec((2, 1, w), lambda b, i: (0, 0, 0))],
        out_specs=[blk(fwd), blk(bwd)],
        out_shape=[jax.ShapeDtypeStruct((nt, w), F32)] * 2,
        scratch_shapes=[pltpu.VMEM((2, N_HEADS, RET_DH, RET_DH), F32)],
        compiler_params=_cparams(("arbitrary", "arbitrary")),
        name="recurrence_ret",
    )(q, k, v, q, k, v, dec)


def _gla_direction(q, k, v, la, st_ref, o_ref, *, reverse):
    cc, wk = q.shape
    wv = v.shape[1]
    r_i, c_i = _iota((cc, cc), 0), _iota((cc, cc), 1)
    incl = (c_i >= r_i) if reverse else (c_i <= r_i)
    cum = _dot_exact(jnp.where(incl, 1.0, 0.0), la)
    cum_end = cum[0:1] if reverse else cum[cc - 1:cc]

    qd = (q * jnp.exp(cum)).astype(BF16)
    kd = (k * jnp.exp(cum_end - cum)).astype(BF16)
    vb = v.astype(BF16)
    st = st_ref[...]
    inter = _dot_nt(qd, st.astype(BF16))
    same_head = (_iota((wv, wk), 0) // GLA_DV) == (_iota((wv, wk), 1) // GLA_DK)
    st_ref[...] = st * jnp.exp(cum_end) + jnp.where(same_head, _dot_tn(vb, kd), 0.0)

    sub = REC_SUB
    hs = N_HEADS * sub
    q_own = (_iota((hs, wk), 0) // sub) == (_iota((hs, wk), 1) // GLA_DK)
    for j in range(cc // sub):
        r0, r1 = j * sub, (j + 1) * sub
        ka, kb = (r0, cc) if reverse else (0, r1)
        mid = cum[r0 + sub // 2:r0 + sub // 2 + 1]
        qj = q[r0:r1] * jnp.exp(jnp.minimum(cum[r0:r1] - mid, EXP_CLAMP))
        kj = (k[ka:kb] * jnp.exp(jnp.minimum(mid - cum[ka:kb], EXP_CLAMP))).astype(BF16)
        q_stack = jnp.where(q_own, jnp.concatenate([qj] * N_HEADS, axis=0), 0.0).astype(BF16)
        rows = r0 + (_iota((hs, kb - ka), 0) % sub)
        cols = ka + _iota((hs, kb - ka), 1)
        keep = (cols >= rows) if reverse else (cols <= rows)
        sc = jnp.where(keep, _dot_nt(q_stack, kj), 0.0).astype(BF16)
        full = _dot(sc, vb[ka:kb])
        lane_head = _iota((sub, wv), 1) // GLA_DV
        oj = inter[r0:r1]
        for h in range(N_HEADS):
            oj = oj + jnp.where(lane_head == h, full[h * sub:(h + 1) * sub], 0.0)
        o_ref[r0:r1, :] = oj


def _gla_kernel(*refs, batch):
    per_batch = refs[:8 * batch]
    gw_ref, gb_ref, of_ref, ob_ref, st_ref = refs[8 * batch:]

    @pl.when(pl.program_id(0) == 0)
    def _():
        st_ref[...] = jnp.zeros_like(st_ref)

    for b in range(batch):
        qf_ref, kf_ref, vf_ref, lf_ref, qb_ref, kb_ref, vb_ref, lb_ref = per_batch[8 * b:8 * b + 8]
        for d, (q_ref, k_ref, v_ref, l_ref, o_ref) in enumerate(((qf_ref, kf_ref, vf_ref, lf_ref, of_ref),
                                                                 (qb_ref, kb_ref, vb_ref, lb_ref, ob_ref))):
            logit = _dot_exact(l_ref[...], gw_ref[d]) + gb_ref[d]
            la = (jnp.minimum(logit, 0.0) - jnp.log(1.0 + jnp.exp(-jnp.abs(logit)))) * (1.0 / GLA_GATE_NORM)
            _gla_direction(q_ref[...] * (GLA_DK ** -0.5), k_ref[...], v_ref[...], la, st_ref.at[b, d],
                           o_ref.at[b], reverse=(d == 1))


def _gla(q, k, v, glr, gw, gb, geo):
    batch, seq, ctx = geo["batch"], geo["seq"], geo["ctx"]
    wq, wv = q.shape[1], v.shape[1]
    cc = REC_CHUNK
    n_cc, n_lc = ctx // cc, seq // cc
    ctx0 = (batch * seq) // cc
    in_specs, args = [], []
    for b in range(batch):
        fwd = lambda i, b=b: (jnp.where(i < n_cc, ctx0 + b * n_cc + i, b * n_lc + i - n_cc), 0)
        bwd = lambda i, b=b: (jnp.where(i < n_cc, ctx0 + b * n_cc + (n_cc - 1 - i),
                                        b * n_lc + (n_lc - 1 - (i - n_cc))), 0)
        for m in (fwd, bwd):
            in_specs += [pl.BlockSpec((cc, wq), m), pl.BlockSpec((cc, wq), m), pl.BlockSpec((cc, wv), m),
                         pl.BlockSpec((cc, LANES), m)]
            args += [q, k, v, glr]
    in_specs += [pl.BlockSpec((2, LANES, wq), lambda i: (0, 0, 0)), pl.BlockSpec((2, 1, wq), lambda i: (0, 0, 0))]
    args += [gw, gb]
    steps = n_cc + n_lc
    out_blk = lambda m: pl.BlockSpec((batch, cc, wv), m)
    return pl.pallas_call(
        functools.partial(_gla_kernel, batch=batch),
        grid=(steps,),
        in_specs=in_specs,
        out_specs=[out_blk(lambda i: (0, i, 0)),
                   out_blk(lambda i: (0, jnp.where(i < n_cc, n_cc - 1 - i, steps - 1 - (i - n_cc)), 0))],
        out_shape=[jax.ShapeDtypeStruct((batch, ctx + seq, wv), F32)] * 2,
        scratch_shapes=[pltpu.VMEM((batch, 2, wv, wq), F32)],
        compiler_params=_cparams(("arbitrary",)),
        name="recurrence_gla",
    )(*args)


def _mixout_kernel(x_ref, mod_ref, at_ref, hy_ref, rf_ref, rb_ref, cg_ref, gf_ref, gb_ref, dr_ref,
                   seg_ref, rg_ref, gg_ref, wo_ref, g1_ref, g2_ref, rwh_ref, rwl_ref, rbias_ref,
                   x1_ref, h2_ref, ti_ref, tw_ref, rk_ref, cnt_ref):
    @pl.when(pl.program_id(0) == 0)
    def _():
        cnt_ref[...] = jnp.zeros_like(cnt_ref)

    def head_norm(o):
        ms = _dot((o * o).astype(BF16), seg_ref[...]) * (1.0 / RET_DH)
        return o * lax.rsqrt(ms + EPS)

    a = jnp.concatenate([at_ref[h] for h in range(N_HEADS)], axis=-1)
    cg = cg_ref[...]
    dr = dr_ref[...]
    rt = head_norm(rf_ref[...] + rb_ref[...]) * rg_ref[...] * (cg * _sigmoid(cg))
    gl = head_norm(gf_ref[...] + gb_ref[...]) * gg_ref[...] * (dr * _sigmoid(dr))
    cat = jnp.concatenate([a, hy_ref[...], rt, gl], axis=-1).astype(BF16)
    y = _dot(cat, wo_ref[...])
    x1 = x_ref[...] + mod_ref[2:3, :] * (_rms(y) * g1_ref[...])
    x1_ref[...] = x1
    h2 = _rms(x1) * g2_ref[...] * (1.0 + mod_ref[4:5, :]) + mod_ref[3:4, :]
    h2_ref[...] = h2

    h_hi = h2.astype(BF16)
    h_lo = (h2 - h_hi.astype(F32)).astype(BF16)
    vals = (_dot(h_hi, rwh_ref[...]) + _dot(h_lo, rwh_ref[...]) + _dot(h_hi, rwl_ref[...])
            + rbias_ref[...])
    tm = vals.shape[0]
    lane = lax.broadcasted_iota(jnp.int32, vals.shape, 1)
    idx_out = jnp.zeros(vals.shape, jnp.int32)
    w_out = jnp.zeros(vals.shape, F32)
    top = None
    den = 0.0
    picks = []
    for r in range(TOP_K):
        m = jnp.max(vals, axis=-1, keepdims=True)
        idx = jnp.min(jnp.where(vals == m, lane, LANES), axis=-1, keepdims=True)
        hit = lane == idx
        vals = jnp.where(hit, -jnp.inf, vals)
        top = m if top is None else top
        e = jnp.exp(m - top)
        den = den + e
        picks.append((idx, e, hit))
    inv = 1.0 / den
    for r, (idx, e, _) in enumerate(picks):
        idx_out = jnp.where(lane == r, idx, idx_out)
        w_out = jnp.where(lane == r, e * inv, w_out)
    ti_ref[...] = idx_out
    tw_ref[...] = w_out

    chosen = functools.reduce(jnp.logical_or, [hit for _, _, hit in picks])
    chosen_f = jnp.where(chosen, 1.0, 0.0)
    earlier = (lax.broadcasted_iota(jnp.int32, (tm, tm), 1) < lax.broadcasted_iota(jnp.int32, (tm, tm), 0))
    before = _dot(jnp.where(earlier, 1.0, 0.0).astype(BF16), chosen_f.astype(BF16)) + cnt_ref[...]
    rk_out = jnp.zeros(vals.shape, jnp.int32)
    for r, (_, _, hit) in enumerate(picks):
        rank = jnp.sum(jnp.where(hit, before, 0.0), axis=-1, keepdims=True)
        rk_out = jnp.where(lane == r, rank.astype(jnp.int32), rk_out)
    rk_ref[...] = rk_out
    cnt_ref[...] = cnt_ref[...] + jnp.sum(chosen_f, axis=0, keepdims=True)


def _mix_out(n_tiles, x, mod, at, hy, rf, rb, cg, gf, gb, dr, seg, rg, gg, wo, g1, g2, rw, rbias, geo):
    rw_hi = rw.astype(BF16)
    rw_lo = (rw - rw_hi.astype(F32)).astype(BF16)
    nt, d = x.shape
    tm = ROW_TILE
    lat_tiles, per_batch, batch = geo["lat_tiles"], geo["tiles_per_batch"], geo["batch"]

    def mod_map(i):
        return (jnp.where(i < lat_tiles, i // per_batch, batch), 0, 0)

    row = lambda i: (i, 0)
    const = lambda i: (0, 0)
    g_blk = pl.BlockSpec((tm, GROUP_W), row)
    ctx_tiles = geo["ctx"] // tm

    def scan_map(i):
        c = i - lat_tiles
        return (jnp.where(i < lat_tiles, i // per_batch, c // ctx_tiles),
                jnp.where(i < lat_tiles, ctx_tiles + i % per_batch, c % ctx_tiles), 0)

    s_blk = pl.BlockSpec((None, tm, GROUP_W), scan_map)
    return pl.pallas_call(
        _mixout_kernel,
        grid=(n_tiles,),
        in_specs=[pl.BlockSpec((tm, d), row), pl.BlockSpec((None, 6, d), mod_map),
                  pl.BlockSpec((N_HEADS, tm, DIFF_DV), lambda i: (0, i, 0)),
                  g_blk, g_blk, g_blk, g_blk, s_blk, s_blk, g_blk,
                  pl.BlockSpec((GROUP_W, GROUP_W), const), pl.BlockSpec((1, GROUP_W), const),
                  pl.BlockSpec((1, GROUP_W), const),
                  pl.BlockSpec((d, d), const, pipeline_mode=pl.Buffered(1)),
                  pl.BlockSpec((1, d), const), pl.BlockSpec((1, d), const),
                  pl.BlockSpec((d, LANES), const), pl.BlockSpec((d, LANES), const), pl.BlockSpec((1, LANES), const)],
        out_specs=[pl.BlockSpec((tm, d), row), pl.BlockSpec((tm, d), row),
                   pl.BlockSpec((tm, LANES), row), pl.BlockSpec((tm, LANES), row),
                   pl.BlockSpec((tm, LANES), row), pl.BlockSpec((1, LANES), const)],
        out_shape=[jax.ShapeDtypeStruct((nt, d), F32), jax.ShapeDtypeStruct((nt, d), F32),
                   jax.ShapeDtypeStruct((nt, LANES), jnp.int32), jax.ShapeDtypeStruct((nt, LANES), F32),
                   jax.ShapeDtypeStruct((nt, LANES), jnp.int32), jax.ShapeDtypeStruct((1, LANES), F32)],
        compiler_params=_cparams(("arbitrary",)),
        name="mix_out_router",
    )(x, mod, at, hy, rf, rb, cg, gf, gb, dr, seg.astype(BF16), rg, gg, wo, g1, g2, rw_hi, rw_lo, rbias)


def _expert_kernel(te_ref, na_ref, wres_ref, xs_ref, w1_ref, b1_ref, w2_ref, b2_ref, perm_ref, ys_ref, w1s, w2s):
    i = pl.program_id(0)
    active = i < na_ref[0]
    fresh = jnp.logical_or(i == 0, te_ref[i] != te_ref[jnp.maximum(i - 1, 0)])
    n_groups = w1_ref.shape[1] // (2 * LANES)

    @pl.when(jnp.logical_and(active, fresh))
    def _():
        for c in range(n_groups):
            cols = slice(c * 2 * LANES, (c + 1) * 2 * LANES)
            w1s[:, cols] = _dot(w1_ref[:, cols].astype(BF16), perm_ref[...]).astype(BF16)
        w2s[...] = w2_ref[...].astype(BF16)

    @pl.when(active)
    def _():
        u = _dot(xs_ref[...].astype(BF16), w1s[...]) + b1_ref[...]
        acts = []
        for c in range(n_groups):
            glu = jnp.minimum(u[:, c * 2 * LANES:c * 2 * LANES + LANES], SWIGLU_LIMIT)
            lin = jnp.clip(u[:, c * 2 * LANES + LANES:(c + 1) * 2 * LANES], -SWIGLU_LIMIT, SWIGLU_LIMIT)
            acts.append((glu * _sigmoid(SWIGLU_ALPHA * glu) * (lin + 1.0)).astype(BF16))
        ys_ref[...] = _dot(jnp.concatenate(acts, axis=-1), w2s[...]) + b2_ref[...]

    @pl.when(jnp.logical_not(active))
    def _():
        ys_ref[...] = jnp.zeros_like(ys_ref)


def _expert_ffn(tile_e, n_active, xs, w1, b1, w2, b2, layer):
    p, d = xs.shape
    tm = MOE_TILE
    de2 = w1.shape[3]
    de = de2 // 2
    sel = np.zeros((2 * LANES, 2 * LANES), np.float32)
    sel[2 * np.arange(LANES), np.arange(LANES)] = 1.0
    sel[2 * np.arange(LANES) + 1, LANES + np.arange(LANES)] = 1.0
    b1g = b1.reshape(N_EXPERTS, de2 // (2 * LANES), LANES, 2).transpose(0, 1, 3, 2).reshape(N_EXPERTS, 1, de2)
    n_tiles = p // tm
    tiles = jnp.arange(n_tiles, dtype=jnp.int32)
    live = tiles < n_active[0]
    te_live = jnp.where(live, tile_e, jnp.max(jnp.where(live, tile_e, 0)))
    first = jnp.concatenate([jnp.ones((1,), bool), te_live[1:] != te_live[:-1]])
    later = jnp.where(te_live[None, :] > te_live[:, None], te_live[None, :], N_EXPERTS)
    following = jnp.min(later, axis=1)
    w_res = jnp.where(first | (following == N_EXPERTS), te_live, following).astype(jnp.int32)
    wmap = lambda i, te, na, wr: (te[i], 0, 0)
    lmap = lambda i, te, na, wr: (layer, wr[i], 0, 0)
    xmap = lambda i, te, na, wr: (jnp.minimum(i, jnp.maximum(na[0] - 1, 0)), 0)
    grid_spec = pltpu.PrefetchScalarGridSpec(
        num_scalar_prefetch=3,
        grid=(p // tm,),
        in_specs=[pl.BlockSpec((tm, d), xmap),
                  pl.BlockSpec((None, None, d, de2), lmap), pl.BlockSpec((None, 1, de2), wmap),
                  pl.BlockSpec((None, None, de, d), lmap), pl.BlockSpec((None, 1, d), wmap),
                  pl.BlockSpec((2 * LANES, 2 * LANES), lambda i, te, na, wr: (0, 0))],
        out_specs=pl.BlockSpec((tm, d), lambda i, te, na, wr: (i, 0)),
        scratch_shapes=[pltpu.VMEM((d, de2), BF16), pltpu.VMEM((de, d), BF16)],
    )
    return pl.pallas_call(
        _expert_kernel,
        grid_spec=grid_spec,
        out_shape=jax.ShapeDtypeStruct((p, d), F32),
        compiler_params=_cparams(("arbitrary",)),
        name="expert_ffn",
    )(te_live, n_active, w_res, xs, w1, b1g, w2, b2.reshape(N_EXPERTS, 1, d), jnp.asarray(sel, BF16))


def _route_plan(topi, rank, counts, tm):
    n, k = topi.shape
    padded = ((counts + tm - 1) // tm) * tm
    pend = jnp.cumsum(padded)
    pstart = pend - padded
    experts = jnp.arange(N_EXPERTS, dtype=jnp.int32)
    start = jnp.sum(jnp.where(topi[:, :, None] == experts, pstart, 0), axis=-1)
    dest = (start + rank).astype(jnp.int32)
    p = n * k + N_EXPERTS * tm
    tile_start = jnp.arange(p // tm, dtype=jnp.int32) * tm
    tile_e = jnp.minimum(jnp.sum((tile_start[:, None] >= pend[None, :]).astype(jnp.int32), axis=1), N_EXPERTS - 1)
    n_active = (pend[-1:] // tm).astype(jnp.int32)
    return dest, tile_e, n_active, p, jnp.stack([pstart, pend]).astype(jnp.int32)


def _dispatch_kernel(bounds_ref, dest_ref, x_ref, out_ref, zeros, sem, zsem, *, tb):
    n_rows = tb * TOP_K

    @pl.when(pl.program_id(0) == 0)
    def _():
        zeros[...] = jnp.zeros_like(zeros)

        def fill(e):
            start = pl.multiple_of(bounds_ref[1, e] - MOE_TILE, MOE_TILE)
            return pltpu.make_async_copy(zeros, out_ref.at[pl.ds(start, MOE_TILE)], zsem)

        for e in range(N_EXPERTS):
            @pl.when(bounds_ref[1, e] > bounds_ref[0, e])
            def _():
                fill(e).start()
        for e in range(N_EXPERTS):
            @pl.when(bounds_ref[1, e] > bounds_ref[0, e])
            def _():
                fill(e).wait()

    def issue(t, carry):
        for k in range(TOP_K):
            pltpu.make_async_copy(x_ref.at[pl.ds(t, 1)], out_ref.at[pl.ds(dest_ref[0, t * TOP_K + k], 1)],
                                  sem).start()
        return carry

    lax.fori_loop(0, tb, issue, 0, unroll=4)
    pltpu.make_async_copy(out_ref.at[pl.ds(0, n_rows)], out_ref.at[pl.ds(0, n_rows)], sem).wait()


def _dispatch(h2, dest, bounds, p_rows, n_tok):
    d = h2.shape[1]
    tb = ROW_TILE
    steps = n_tok // tb
    return pl.pallas_call(
        functools.partial(_dispatch_kernel, tb=tb),
        grid=(steps,),
        in_specs=[pl.BlockSpec(memory_space=pltpu.SMEM),
                  pl.BlockSpec((None, 1, tb * TOP_K), lambda i: (i, 0, 0), memory_space=pltpu.SMEM),
                  pl.BlockSpec((tb, d), lambda i: (i, 0))],
        out_specs=pl.BlockSpec(memory_space=pl.ANY),
        out_shape=jax.ShapeDtypeStruct((p_rows, d), F32),
        scratch_shapes=[pltpu.VMEM((MOE_TILE, d), F32), pltpu.SemaphoreType.DMA(()), pltpu.SemaphoreType.DMA(())],
        compiler_params=pltpu.CompilerParams(dimension_semantics=("arbitrary",), has_side_effects=True),
        name="moe_dispatch",
    )(bounds, dest.reshape(steps, 1, tb * TOP_K), h2)


def _combine_kernel(dest_ref, x_ref, tw_ref, mod_ref, g_ref, ys_ref, o_ref, buf, sem, *, tb):
    n_rows = tb * TOP_K

    def issue(r, carry):
        pltpu.make_async_copy(ys_ref.at[pl.ds(dest_ref[0, r], 1)], buf.at[pl.ds(r, 1)], sem).start()
        return carry

    lax.fori_loop(0, n_rows, issue, 0, unroll=8)
    pltpu.make_async_copy(ys_ref.at[pl.ds(0, n_rows)], buf, sem).wait()
    y = None
    for k in range(TOP_K):
        t = buf[k * tb:(k + 1) * tb, :] * tw_ref[:, k:k + 1]
        y = t if y is None else y + t
    o_ref[...] = x_ref[...] + mod_ref[5:6, :] * (_rms(y) * g_ref[...])


def _combine_residual(n_tiles, x, ys, dest, topw, mod, g, geo):
    nt, d = x.shape
    tb = ROW_TILE
    lat_tiles, per_batch, batch = geo["lat_tiles"], geo["tiles_per_batch"], geo["batch"]

    def mod_map(i):
        return (jnp.where(i < lat_tiles, i // per_batch, batch), 0, 0)

    row = lambda i: (i, 0)
    slot_major = jnp.swapaxes(dest.reshape(n_tiles, tb, TOP_K), 1, 2).reshape(n_tiles, 1, tb * TOP_K)
    return pl.pallas_call(
        functools.partial(_combine_kernel, tb=tb),
        grid=(n_tiles,),
        in_specs=[pl.BlockSpec((None, 1, tb * TOP_K), lambda i: (i, 0, 0), memory_space=pltpu.SMEM),
                  pl.BlockSpec((tb, d), row), pl.BlockSpec((tb, LANES), row),
                  pl.BlockSpec((None, 6, d), mod_map), pl.BlockSpec((1, d), lambda i: (0, 0)),
                  pl.BlockSpec(memory_space=pl.ANY)],
        out_specs=pl.BlockSpec((tb, d), row),
        out_shape=jax.ShapeDtypeStruct((n_tiles * tb, d), F32),
        scratch_shapes=[pltpu.VMEM((tb * TOP_K, d), F32), pltpu.SemaphoreType.DMA(())],
        compiler_params=_cparams(("arbitrary",)),
        name="combine_residual",
    )(slot_major, x, topw, mod, g, ys)


def _rope_tables(n_tok, dim, reps, pad_rows):
    rows = n_tok // GRID_W
    row = jnp.repeat(jnp.arange(rows, dtype=F32), GRID_W)
    col = jnp.tile(jnp.arange(GRID_W, dtype=F32), rows)
    quarter = dim // 4
    inv = ROPE_BASE ** (-jnp.arange(quarter, dtype=F32) / quarter)
    ar = row[:, None] * inv[None]
    ac = col[:, None] * inv[None]
    ang = jnp.concatenate([ar, ar, ac, ac], axis=-1)
    cos = jnp.concatenate([jnp.tile(jnp.cos(ang), (1, reps)), jnp.ones((pad_rows, dim * reps), F32)])
    sin = jnp.concatenate([jnp.tile(jnp.sin(ang), (1, reps)), jnp.zeros((pad_rows, dim * reps), F32)])
    return cos, sin


def _widen_w_in(w):
    return jnp.pad(w, ((0, 0), (0, _IN_COLS - w.shape[1]))).astype(BF16)


def kernel(x, c, ctx, c_ctx, w_mod, b_mod, norm_g, w_in, w_out, diff_lambda, diff_subln_g, hy_short_w, hy_short_b, hy_w1, hy_b1, hy_w2, hy_b2, hy_w3, hy_freq, hy_bias, ret_decay, ret_norm_g, gla_gate_w, gla_gate_b, gla_norm_g, router_w, router_b, exp_w1, exp_b1, exp_w2, exp_b2):
    batch, seq, d = x.shape
    n_ctx = ctx.shape[1]
    depth = w_mod.shape[0]
    n_lat_rows, n_ctx_rows = batch * seq, batch * n_ctx
    nt = n_lat_rows + n_ctx_rows
    assert d == D_MODEL and seq % ATT_Q_TILE == 0 and n_ctx % ROW_TILE == 0 and seq % n_ctx == 0
    geo = dict(batch=batch, seq=seq, ctx=n_ctx, lat_tiles=n_lat_rows // ROW_TILE,
               tiles_per_batch=seq // ROW_TILE)

    xs = jnp.concatenate([x.reshape(n_lat_rows, d), ctx.reshape(n_ctx_rows, d)], axis=0)
    mod_rows = 8
    cc = jnp.zeros((mod_rows, d), F32).at[:batch].set(c).at[batch].set(c_ctx)
    mod_all = _modulation(cc, w_mod, b_mod).reshape(depth, mod_rows, 6, d)

    rope = (*_rope_tables(seq, DIFF_DQK, GROUP_W // DIFF_DQK, ROW_TILE),
            *_rope_tables(seq, RET_DH, GROUP_W // RET_DH, ROW_TILE))
    dft_lat = _dft_tables(seq)
    dft_ctx = _dft_tables(n_ctx)
    seg = jnp.asarray(np.kron(np.eye(N_HEADS, dtype=np.float32), np.ones((RET_DH, RET_DH), np.float32)))

    for l in range(depth):
        need_ctx = l < depth - 1
        lam_init = 0.8 - 0.6 * math.exp(-0.3 * l)
        mod = mod_all[l]
        (aq, ak, av, bu, cq, ck, cv, cg, dq, dk, dv, dr, dl) = _in_projection(
            xs, mod, norm_g[l, 0].reshape(1, d), _widen_w_in(w_in[l]), rope, geo)

        sub_g = diff_subln_g[l].reshape(1, DIFF_DV)
        at = _diff_attention(diff_lambda[l], sub_g, aq, ak, av, geo, lam_init)
        if need_ctx:
            at = _diff_attention(diff_lambda[l], sub_g, aq, ak, av, geo, lam_init, prev=at)

        def hyena(n, row_blk0, tables, out_blk0, prev):
            e, dd, nyqf = _hyena_filter(n, hy_w1[l], hy_b1[l], hy_w2[l], hy_b2[l], hy_w3[l], hy_freq[l])
            zb, x0c, zf, nyqz = _hyena_gate(bu, hy_short_w[l], hy_short_b[l], n, batch, row_blk0)
            return _hyena_conv(tables, zb, e, dd, x0c, zf, nyqz, nyqf, hy_bias[l], n, batch, nt, out_blk0, prev)

        hy = hyena(seq, 0, dft_lat, 0, None)
        if need_ctx:
            hy = hyena(n_ctx, n_lat_rows // n_ctx, dft_ctx, n_lat_rows // min(512, n_ctx), hy)

        dec = jnp.repeat(ret_decay[l], RET_DH, axis=-1).reshape(2, 1, N_HEADS * RET_DH)
        rf, rb = _retention(cq, ck, cv, dec, geo)

        gw = jnp.zeros((2, LANES, N_HEADS * GLA_DK), F32)
        gw = gw.at[0, :GLA_RANK].set(gla_gate_w[l, 0]).at[1, GLA_RANK:2 * GLA_RANK].set(gla_gate_w[l, 1])
        gbias = gla_gate_b[l].reshape(2, 1, N_HEADS * GLA_DK)
        gf, gb = _gla(dq, dk, dv, dl, gw, gbias, geo)

        n_tiles = nt // ROW_TILE if need_ctx else n_lat_rows // ROW_TILE
        rw = jnp.pad(router_w[l], ((0, 0), (0, LANES - N_EXPERTS)))
        rbias = jnp.pad(router_b[l], (0, LANES - N_EXPERTS), constant_values=-jnp.inf).reshape(1, LANES)
        x1, h2, topi, topw, rank, cnt = _mix_out(
            n_tiles, xs, mod, at, hy, rf, rb, cg, gf, gb, dr, seg,
            ret_norm_g[l].reshape(1, GROUP_W), jnp.tile(gla_norm_g[l], N_HEADS).reshape(1, GROUP_W),
            w_out[l].astype(BF16), norm_g[l, 1].reshape(1, d), norm_g[l, 2].reshape(1, d), rw, rbias, geo)

        n_tok = n_tiles * ROW_TILE
        dest, tile_e, n_active, p_rows, bounds = _route_plan(
            topi[:n_tok, :TOP_K], rank[:n_tok, :TOP_K], cnt[0, :N_EXPERTS].astype(jnp.int32), MOE_TILE)
        dispatched = _dispatch(h2, dest, bounds, p_rows, n_tok)
        ys = _expert_ffn(tile_e, n_active, dispatched, exp_w1, exp_b1[l], exp_w2, exp_b2[l], l)
        xs = _combine_residual(n_tiles, x1, ys, dest, topw, mod, norm_g[l, 3].reshape(1, d), geo)

    return xs[:n_lat_rows].reshape(batch, seq, d)
```

```python
import functools
import math

import numpy as np
import jax
import jax.numpy as jnp
from jax import lax
from jax.experimental import pallas as pl
from jax.experimental.pallas import tpu as pltpu

F32 = jnp.float32
BF16 = jnp.bfloat16
HIGHEST = lax.Precision.HIGHEST

D_MODEL = 1024
GRID_W = 64
GROUP_W = 256
N_HEADS = 4
DIFF_DQK = 32
DIFF_DV = 64
ROPE_BASE = 10000.0
HY_CH = 256
HY_BANDS = 16
HY_FFN = 64
HY_FAST_DECAY_PCT = 0.3
HY_SLOW_DECAY_PCT = 1.5
HY_DECAY_TARGET = 1e-2
RET_DH = 64
GLA_DK = 32
GLA_DV = 64
GLA_RANK = 16
GLA_GATE_NORM = 16.0
N_EXPERTS = 32
TOP_K = 4
D_EXPERT = 1024
SWIGLU_LIMIT = 7.0
SWIGLU_ALPHA = 1.702
EPS = 1e-6

LANES = 128
ROW_TILE = 256
REC_CHUNK = 128
RET_CHUNK = 256
REC_SUB = 32
EXP_CLAMP = 80.0
MOE_TILE = 256
ATT_KEY_CHUNK = 512
ATT_Q_TILE = 512
DFT_TABLE_ROWS = 128
VMEM_LIMIT = 52 * 1024 * 1024

_A_Q, _A_K, _A_V = 0, 256, 512
_B_U = 768
_C_Q, _C_K, _C_V, _C_G = 1536, 1792, 2048, 2304
_D_QK, _D_V, _D_R, _D_L = 2560, 2816, 3072, 3328
_IN_COLS = 3456


def _cparams(sem):
    return pltpu.CompilerParams(dimension_semantics=sem, vmem_limit_bytes=VMEM_LIMIT)


def _sigmoid(x):
    return 1.0 / (1.0 + jnp.exp(-x))


def _rms(x):
    return x * lax.rsqrt(jnp.mean(x * x, axis=-1, keepdims=True) + EPS)


def _iota(shape, axis):
    return lax.broadcasted_iota(jnp.int32, shape, axis)


def _dot(a, b):
    return jnp.dot(a, b, preferred_element_type=F32)


def _dot_exact(a, b):
    return jnp.dot(a, b, preferred_element_type=F32, precision=HIGHEST)


def _dot_nt(a, b):
    return lax.dot_general(a, b, (((1,), (1,)), ((), ())), preferred_element_type=F32)


def _dot_tn(a, b):
    return lax.dot_general(a, b, (((0,), (0,)), ((), ())), preferred_element_type=F32)


def _mod_kernel(c_ref, w_ref, b_ref, o_ref):
    c = c_ref[...]
    o_ref[...] = _dot_exact(c * _sigmoid(c), w_ref[...]) + b_ref[...]


def _modulation(cc, w_mod, b_mod):
    depth, d, n = w_mod.shape
    tn = n // 4
    rows = cc.shape[0]
    return pl.pallas_call(
        _mod_kernel,
        grid=(depth, n // tn),
        in_specs=[pl.BlockSpec((rows, d), lambda l, j: (0, 0)),
                  pl.BlockSpec((None, d, tn), lambda l, j: (l, 0, j)),
                  pl.BlockSpec((None, 1, tn), lambda l, j: (l, 0, j))],
        out_specs=pl.BlockSpec((None, rows, tn), lambda l, j: (l, 0, j)),
        out_shape=jax.ShapeDtypeStruct((depth, rows, n), F32),
        compiler_params=_cparams(("arbitrary", "arbitrary")),
        name="modulation",
    )(cc, w_mod, b_mod.reshape(depth, 1, n))


def _inproj_kernel(x_ref, mod_ref, g_ref, w_ref, cosa_ref, sina_ref, cosc_ref, sinc_ref,
                   aq_ref, ak_ref, av_ref, bu_ref, cq_ref, ck_ref, cv_ref, cg_ref,
                   dq_ref, dk_ref, dv_ref, dr_ref, dl_ref):
    xn = _rms(x_ref[...]) * g_ref[...]
    h = (xn * (1.0 + mod_ref[1:2, :]) + mod_ref[0:1, :]).astype(BF16)

    def proj(a, width):
        return _dot(h, w_ref[:, a:a + width])

    def roped(a, dim, cos_ref, sin_ref):
        p = proj(a, GROUP_W)
        quarter = dim // 4
        lane = _iota(p.shape, 1)
        partner = jnp.where(lane % (dim // 2) < quarter, -pltpu.roll(p, GROUP_W - quarter, 1),
                            pltpu.roll(p, quarter, 1))
        return p * cos_ref[...] + partner * sin_ref[...]

    aq = roped(_A_Q, DIFF_DQK, cosa_ref, sina_ref) * (DIFF_DQK ** -0.5 * math.log2(math.e))
    ak = roped(_A_K, DIFF_DQK, cosa_ref, sina_ref)
    av = proj(_A_V, GROUP_W)
    ones_col = jnp.where(lax.broadcasted_iota(jnp.int32, (av.shape[0], LANES - DIFF_DV), 1) == 0, 1.0, 0.0)
    for hd in range(N_HEADS):
        sl = slice(hd * DIFF_DV, (hd + 1) * DIFF_DV)
        aq_ref[hd] = aq[:, sl].astype(BF16)
        ak_ref[hd] = ak[:, sl].astype(BF16)
        av_ref[hd] = jnp.concatenate([av[:, sl], ones_col], axis=-1).astype(BF16)
    bu_ref[...] = proj(_B_U, 3 * HY_CH)
    cq_ref[...] = roped(_C_Q, RET_DH, cosc_ref, sinc_ref)
    ck_ref[...] = roped(_C_K, RET_DH, cosc_ref, sinc_ref)
    cv_ref[...] = proj(_C_V, GROUP_W)
    cg_ref[...] = proj(_C_G, GROUP_W)
    dqk = proj(_D_QK, 2 * LANES)
    dq_ref[...] = dqk[:, :LANES]
    dk_ref[...] = dqk[:, LANES:]
    dv_ref[...] = proj(_D_V, GROUP_W)
    dr_ref[...] = proj(_D_R, GROUP_W)
    dl_ref[...] = proj(_D_L, LANES)


def _in_projection(x, mod, g, w_wide, rope, geo):
    nt, d = x.shape
    tm = ROW_TILE
    lat_tiles, per_batch, batch = geo["lat_tiles"], geo["tiles_per_batch"], geo["batch"]

    def mod_map(i):
        return (jnp.where(i < lat_tiles, i // per_batch, batch), 0, 0)

    def rope_map(i):
        return (jnp.where(i < lat_tiles, i % per_batch, per_batch), 0)

    row = lambda i: (i, 0)
    head = lambda i: (0, i, 0)
    const = lambda i: (0, 0)
    f32_out = lambda w: jax.ShapeDtypeStruct((nt, w), F32)
    head_out = lambda w: jax.ShapeDtypeStruct((N_HEADS, nt, w), BF16)
    head_spec = lambda w: pl.BlockSpec((N_HEADS, tm, w), head)
    widths = [3 * HY_CH, GROUP_W, GROUP_W, GROUP_W, GROUP_W, LANES, LANES, GROUP_W, GROUP_W, LANES]
    return pl.pallas_call(
        _inproj_kernel,
        grid=(nt // tm,),
        in_specs=[pl.BlockSpec((tm, d), row),
                  pl.BlockSpec((None, 6, d), mod_map),
                  pl.BlockSpec((1, d), const),
                  pl.BlockSpec((d, _IN_COLS), const, pipeline_mode=pl.Buffered(1)),
                  pl.BlockSpec((tm, GROUP_W), rope_map), pl.BlockSpec((tm, GROUP_W), rope_map),
                  pl.BlockSpec((tm, GROUP_W), rope_map), pl.BlockSpec((tm, GROUP_W), rope_map)],
        out_specs=[head_spec(DIFF_DV), head_spec(DIFF_DV), head_spec(LANES)]
                  + [pl.BlockSpec((tm, w), row) for w in widths],
        out_shape=[head_out(DIFF_DV), head_out(DIFF_DV), head_out(LANES)] + [f32_out(w) for w in widths],
        compiler_params=_cparams(("arbitrary",)),
        name="in_projection",
    )(x, mod, g, w_wide, *rope)


def _attn_kernel(lam_ref, g_ref, q_ref, *rest, lam_init, has_lat):
    if has_lat:
        kl_ref, vl_ref, kc_ref, vc_ref, o_ref = rest
        keys = [(kl_ref, vl_ref), (kc_ref, vc_ref)]
    else:
        kc_ref, vc_ref, o_ref = rest
        keys = [(kc_ref, vc_ref)]
    lp = lam_ref[...]
    lam = (jnp.exp(jnp.sum(lp[0:1] * lp[1:2], axis=-1, keepdims=True))
           - jnp.exp(jnp.sum(lp[2:3] * lp[3:4], axis=-1, keepdims=True)) + lam_init)
    q = q_ref[...]
    tq = q.shape[0]
    lane = lax.broadcasted_iota(jnp.int32, q.shape, 1)
    chunks = []
    for k_ref, v_ref in keys:
        size = min(ATT_KEY_CHUNK, k_ref.shape[0])
        chunks += [(k_ref, v_ref, s0, size) for s0 in range(0, k_ref.shape[0], size)]

    def lane_groups(t):
        return [t[:, c0:c0 + LANES] for c0 in range(0, t.shape[1], LANES)]

    qm = [jnp.where((lane >= m * DIFF_DQK) & (lane < (m + 1) * DIFF_DQK), q, jnp.zeros_like(q)) for m in range(2)]

    def score(m, j):
        k_ref, _, s0, size = chunks[j]
        return _dot_nt(qm[m], k_ref[s0:s0 + size, :])

    def row_max(scores):
        wide = functools.reduce(jnp.maximum, [g for t in scores for g in lane_groups(t)])
        return jnp.max(wide, axis=-1, keepdims=True)

    def weighted(m, j, s, mx, acc):
        _, v_ref, s0, size = chunks[j]
        return acc + _dot(jnp.exp2(s - mx).astype(BF16), v_ref[s0:s0 + size, :])

    n = len(chunks)
    s1 = [score(0, j) for j in range(n)]
    mx1 = row_max(s1)
    s2 = []
    acc1 = jnp.zeros((tq, LANES), F32)
    for j in range(n):
        s2.append(score(1, j))
        acc1 = weighted(0, j, s1[j], mx1, acc1)
    mx2 = row_max(s2)
    acc2 = jnp.zeros((tq, LANES), F32)
    for j in range(n):
        acc2 = weighted(1, j, s2[j], mx2, acc2)
    o = (acc1[:, :DIFF_DV] * (1.0 / acc1[:, DIFF_DV:DIFF_DV + 1])
         - lam * (acc2[:, :DIFF_DV] * (1.0 / acc2[:, DIFF_DV:DIFF_DV + 1])))
    o_ref[...] = _rms(o) * g_ref[...] * (1.0 - lam_init)


def _diff_attention(lam_p, subln_g, aq, ak, av, geo, lam_init, prev=None):
    batch, seq, ctx = geo["batch"], geo["seq"], geo["ctx"]
    nt = aq.shape[1]
    has_lat = prev is None
    tq = ATT_Q_TILE if has_lat else ROW_TILE
    n_q = (seq if has_lat else ctx) // tq
    q_off = 0 if has_lat else (batch * seq) // tq
    ctx_blk0 = (batch * seq) // ctx

    qmap = lambda b, h, i: (h, q_off + b * n_q + i, 0)
    lat_map = lambda b, h, i: (h, b, 0)
    ctx_map = lambda b, h, i: (h, ctx_blk0 + b, 0)
    const = lambda b, h, i: (0, 0)
    in_specs = [pl.BlockSpec((4, DIFF_DQK), const), pl.BlockSpec((1, DIFF_DV), const),
                pl.BlockSpec((None, tq, DIFF_DV), qmap)]
    args = [lam_p, subln_g, aq]
    if has_lat:
        in_specs += [pl.BlockSpec((None, seq, DIFF_DV), lat_map), pl.BlockSpec((None, seq, LANES), lat_map)]
        args += [ak, av]
    in_specs += [pl.BlockSpec((None, ctx, DIFF_DV), ctx_map), pl.BlockSpec((None, ctx, LANES), ctx_map)]
    args += [ak, av]
    aliases = {}
    if not has_lat:
        in_specs.append(pl.BlockSpec(memory_space=pl.ANY))
        args.append(prev)
        aliases = {len(args) - 1: 0}
    kern = functools.partial(_attn_kernel, lam_init=lam_init, has_lat=has_lat)
    if not has_lat:
        kern = _drop_last_input(kern, n_in=len(args))
    return pl.pallas_call(
        kern,
        grid=(batch, N_HEADS, n_q),
        in_specs=in_specs,
        out_specs=pl.BlockSpec((None, tq, DIFF_DV), qmap),
        out_shape=jax.ShapeDtypeStruct((N_HEADS, nt, DIFF_DV), F32),
        input_output_aliases=aliases,
        compiler_params=_cparams(("arbitrary", "arbitrary", "arbitrary")),
        name="diff_attention" if has_lat else "diff_attention_ctx",
    )(*args)


def _drop_last_input(kern, n_in):
    def wrapped(*refs):
        return kern(*refs[:n_in - 1], *refs[n_in:])
    return wrapped


def _dft_table_kernel(cx_ref, sx_ref, cy_ref, sy_ref, c_ref, s_ref, *, n, span):
    t = _iota((LANES, n), 1)
    row = _iota((LANES, n), 0)
    pick_a = jnp.where(t // span == row, 1.0, 0.0).astype(BF16)
    pick_b = jnp.where(t % span == row, 1.0, 0.0).astype(BF16)

    def widen(ref, pick):
        x = ref[...]
        hi = x.astype(BF16)
        lo = (x - hi.astype(F32)).astype(BF16)
        return _dot(hi, pick) + _dot(lo, pick)

    cx, sx = widen(cx_ref, pick_a), widen(sx_ref, pick_a)
    cy, sy = widen(cy_ref, pick_b), widen(sy_ref, pick_b)
    c_ref[...] = (cx * cy - sx * sy).astype(BF16)
    s_ref[...] = (sx * cy + cx * sy).astype(BF16)


def _dft_tables(n):
    span = 64
    assert n % span == 0 and n // span <= LANES
    k = jnp.arange(n, dtype=jnp.int32)[:, None]
    j = jnp.arange(LANES, dtype=jnp.int32)[None, :]

    def factor(step, count):
        ang = ((k * (j * step)) % (2 * n)).astype(F32) * (math.pi / n)
        live = j < count
        return jnp.where(live, jnp.cos(ang), 0.0), jnp.where(live, jnp.sin(ang), 0.0)

    cx, sx = factor(span, n // span)
    cy, sy = factor(1, span)
    tk = min(DFT_TABLE_ROWS, n)
    small = pl.BlockSpec((tk, LANES), lambda i: (i, 0))
    big = pl.BlockSpec((tk, n), lambda i: (i, 0))
    return pl.pallas_call(
        functools.partial(_dft_table_kernel, n=n, span=span),
        grid=(n // tk,),
        in_specs=[small] * 4,
        out_specs=[big, big],
        out_shape=[jax.ShapeDtypeStruct((n, n), BF16)] * 2,
        compiler_params=_cparams(("arbitrary",)),
        name="dft_tables",
    )(cx, sx, cy, sy)


def _hy_filter_kernel(w1t_ref, w1c_ref, w1s_ref, b1_ref, w2_ref, b2_ref, w3_ref, fr_ref,
                      bands_ref, deltas_ref, e_ref, d_ref, nyq_ref, *, n):
    pos_i = lax.broadcasted_iota(jnp.int32, (n, 1), 0)
    pos = pos_i.astype(F32)
    t = pos / (n - 1)
    ang = ((2.0 * math.pi) * pos / n) * bands_ref[...]
    pre = t * w1t_ref[...] + _dot_exact(jnp.cos(ang), w1c_ref[...]) - _dot_exact(jnp.sin(ang), w1s_ref[...])
    hdn = jnp.sin(fr_ref[0:1, :] * (pre + b1_ref[...]))
    hdn = jnp.sin(fr_ref[1:2, :] * (_dot_exact(hdn, w2_ref[...]) + b2_ref[...]))
    raw = _dot_exact(hdn, w3_ref[...])
    window = jnp.exp(-t * deltas_ref[...])
    hf = raw[:, :HY_CH] * window
    hb = jnp.where(pos_i > 0, raw[:, HY_CH:] * window, 0.0)
    inv = 1.0 / (jnp.sum(jnp.abs(hf), axis=0, keepdims=True) + jnp.sum(jnp.abs(hb), axis=0, keepdims=True))
    e = (hf + hb) * inv
    e_ref[...] = e.astype(BF16)
    d_ref[...] = ((hb - hf) * inv).astype(BF16)
    sign = (1 - 2 * (pos_i & 1)).astype(F32)
    nyq_ref[...] = jnp.sum(e * sign, axis=0, keepdims=True)


def _hyena_filter(n, w1, b1, w2, b2, w3, freq):
    bands = jnp.linspace(1e-4, HY_BANDS - 1, HY_BANDS, dtype=F32).reshape(1, HY_BANDS)
    max_decay = math.log(HY_DECAY_TARGET) / HY_FAST_DECAY_PCT
    min_decay = math.log(HY_DECAY_TARGET) / HY_SLOW_DECAY_PCT
    deltas = jnp.abs(jnp.linspace(min_decay, max_decay, HY_CH, dtype=F32)).reshape(1, HY_CH)
    args = [w1[0:1], w1[1:1 + HY_BANDS], w1[1 + HY_BANDS:], b1.reshape(1, HY_FFN), w2, b2.reshape(1, HY_FFN),
            w3, freq, bands, deltas]
    return pl.pallas_call(
        functools.partial(_hy_filter_kernel, n=n),
        out_shape=[jax.ShapeDtypeStruct((n, HY_CH), BF16), jax.ShapeDtypeStruct((n, HY_CH), BF16),
                   jax.ShapeDtypeStruct((1, HY_CH), F32)],
        compiler_params=pltpu.CompilerParams(vmem_limit_bytes=VMEM_LIMIT),
        name="hyena_filter",
    )(*args)


def _hy_gate_kernel(x0_ref, x1_ref, v_ref, w0_ref, w1_ref, wv_ref, b0_ref, b1_ref, bv_ref,
                    zb_ref, x0c_ref, zf_ref, nyq_ref, *, n):
    row = lax.broadcasted_iota(jnp.int32, (n, 1), 0)

    def conv(u_ref, w_ref, b_ref):
        u = u_ref[...]
        up = jnp.where(row > 0, pltpu.roll(u, 1, 0), 0.0)
        dn = jnp.where(row < n - 1, pltpu.roll(u, n - 1, 0), 0.0)
        return up * w_ref[0:1, :] + u * w_ref[1:2, :] + dn * w_ref[2:3, :] + b_ref[...]

    z = conv(x1_ref, w1_ref, b1_ref) * conv(v_ref, wv_ref, bv_ref)
    x0c_ref[...] = conv(x0_ref, w0_ref, b0_ref)
    zf_ref[...] = z
    zb_ref[...] = z.astype(BF16)
    sign = (1 - 2 * (row & 1)).astype(F32)
    nyq_ref[...] = jnp.sum(z * sign, axis=0, keepdims=True)


def _hyena_gate(bu, short_w, short_b, n, batch, row_blk0):
    halves = HY_CH // LANES
    sb = short_b.reshape(1, 3 * HY_CH)
    seg = lambda part: pl.BlockSpec((n, LANES), lambda b, j: (row_blk0 + b, part * halves + j))
    wsp = lambda part: pl.BlockSpec((3, LANES), lambda b, j: (0, part * halves + j))
    bsp = lambda part: pl.BlockSpec((1, LANES), lambda b, j: (0, part * halves + j))
    return pl.pallas_call(
        functools.partial(_hy_gate_kernel, n=n),
        grid=(batch, halves),
        in_specs=[seg(0), seg(1), seg(2), wsp(0), wsp(1), wsp(2), bsp(0), bsp(1), bsp(2)],
        out_specs=[pl.BlockSpec((n, LANES), lambda b, j: (0, b * halves + j)),
                   pl.BlockSpec((n, LANES), lambda b, j: (b, j)),
                   pl.BlockSpec((n, LANES), lambda b, j: (b, j)),
                   pl.BlockSpec((None, 1, LANES), lambda b, j: (b, 0, j))],
        out_shape=[jax.ShapeDtypeStruct((n, batch * HY_CH), BF16),
                   jax.ShapeDtypeStruct((batch * n, HY_CH), F32),
                   jax.ShapeDtypeStruct((batch * n, HY_CH), F32),
                   jax.ShapeDtypeStruct((batch, 1, HY_CH), F32)],
        compiler_params=_cparams(("arbitrary", "arbitrary")),
        name="hyena_gate",
    )(bu, bu, bu, short_w, short_w, short_w, sb, sb, sb)


def _hy_spectrum_kernel(c_ref, s_ref, z_ref, e_ref, d_ref, yr_ref, yi_ref, *, n, tk, batch):
    c = c_ref[...]
    s = s_ref[...]
    zr = _dot(c, z_ref[...])
    zs = _dot(s, z_ref[...])
    fr = _dot(c, e_ref[...])
    fi = _dot(s, d_ref[...])
    k = pl.program_id(0) * tk + lax.broadcasted_iota(jnp.int32, (tk, 1), 0)
    wk = jnp.where(k == 0, 1.0, 2.0) * (1.0 / (2 * n))
    for b in range(batch):
        sl = slice(b * HY_CH, (b + 1) * HY_CH)
        yr = zr[:, sl] * fr + zs[:, sl] * fi
        yi = zr[:, sl] * fi - zs[:, sl] * fr
        yr_ref[:, sl] = (yr * wk).astype(BF16)
        yi_ref[:, sl] = (-(yi * wk)).astype(BF16)


def _hy_inverse_kernel(c_ref, s_ref, yr_ref, yi_ref, x0c_ref, zf_ref, nyqz_ref, nyqf_ref, bias_ref,
                       *rest, n, tt, batch):
    o_ref, y_scr = rest[-2], rest[-1]
    b = pl.program_id(1)

    @pl.when(b == 0)
    def _():
        y = _dot(c_ref[...], yr_ref[...]) + _dot(s_ref[...], yi_ref[...])
        for bb in range(batch):
            y_scr[bb] = y[:, bb * HY_CH:(bb + 1) * HY_CH]

    t = pl.program_id(0) * tt + lax.broadcasted_iota(jnp.int32, (tt, 1), 0)
    sign = (1 - 2 * (t & 1)).astype(F32)
    nyq = nyqz_ref[...] * nyqf_ref[...] * (1.0 / (2 * n))
    zf = zf_ref[...]
    o_ref[...] = x0c_ref[...] * (y_scr[b] + sign * nyq + bias_ref[...] * zf)


def _hyena_conv(cs, zb, e, d, x0c, zf, nyqz, nyqf, bias, n, batch, nt, out_blk0, prev=None):
    c_tab, s_tab = cs
    bw = batch * HY_CH
    tk = min(512, n)
    whole = lambda shape: pl.BlockSpec(shape, lambda *_: (0,) * len(shape), pipeline_mode=pl.Buffered(1))
    yr, yi = pl.pallas_call(
        functools.partial(_hy_spectrum_kernel, n=n, tk=tk, batch=batch),
        grid=(n // tk,),
        in_specs=[pl.BlockSpec((tk, n), lambda i: (i, 0)), pl.BlockSpec((tk, n), lambda i: (i, 0)),
                  whole((n, bw)), whole((n, HY_CH)), whole((n, HY_CH))],
        out_specs=[pl.BlockSpec((tk, bw), lambda i: (i, 0))] * 2,
        out_shape=[jax.ShapeDtypeStruct((n, bw), BF16)] * 2,
        compiler_params=_cparams(("arbitrary",)),
        name="hyena_spectrum",
    )(c_tab, s_tab, zb, e, d)

    tt = min(512, n)
    n_t = n // tt
    seg = lambda i, b: (b * n_t + i, 0)
    in_specs = [pl.BlockSpec((tt, n), lambda i, b: (i, 0)), pl.BlockSpec((tt, n), lambda i, b: (i, 0)),
                whole((n, bw)), whole((n, bw)),
                pl.BlockSpec((tt, HY_CH), seg), pl.BlockSpec((tt, HY_CH), seg),
                pl.BlockSpec((None, 1, HY_CH), lambda i, b: (b, 0, 0)),
                pl.BlockSpec((1, HY_CH), lambda i, b: (0, 0)), pl.BlockSpec((1, HY_CH), lambda i, b: (0, 0))]
    args = [c_tab, s_tab, yr, yi, x0c, zf, nyqz, nyqf, bias.reshape(1, HY_CH)]
    aliases = {}
    if prev is not None:
        in_specs.append(pl.BlockSpec(memory_space=pl.ANY))
        args.append(prev)
        aliases = {len(args) - 1: 0}
    return pl.pallas_call(
        functools.partial(_hy_inverse_kernel, n=n, tt=tt, batch=batch),
        grid=(n_t, batch),
        in_specs=in_specs,
        out_specs=pl.BlockSpec((tt, HY_CH), lambda i, b: (out_blk0 + b * n_t + i, 0)),
        out_shape=jax.ShapeDtypeStruct((nt, HY_CH), F32),
        scratch_shapes=[pltpu.VMEM((batch, tt, HY_CH), F32)],
        input_output_aliases=aliases,
        compiler_params=_cparams(("arbitrary", "arbitrary")),
        name="hyena_inverse" if prev is None else "hyena_inverse_ctx",
    )(*args)


def _scan_maps(geo, cc):
    batch, seq, ctx = geo["batch"], geo["seq"], geo["ctx"]
    n_cc, n_lc = ctx // cc, seq // cc
    ctx0 = (batch * seq) // cc

    def fwd(b, i):
        return (jnp.where(i < n_cc, ctx0 + b * n_cc + i, b * n_lc + i - n_cc), 0)

    def bwd(b, i):
        return (jnp.where(i < n_cc, ctx0 + b * n_cc + (n_cc - 1 - i), b * n_lc + (n_lc - 1 - (i - n_cc))), 0)

    return fwd, bwd, n_cc + n_lc


def _ret_kernel(qf_ref, kf_ref, vf_ref, qb_ref, kb_ref, vb_ref, dec_ref, of_ref, ob_ref, st_ref):
    @pl.when(pl.program_id(1) == 0)
    def _():
        st_ref[...] = jnp.zeros_like(st_ref)

    cc = qf_ref.shape[0]
    r_i, c_i = _iota((cc, cc), 0), _iota((cc, cc), 1)
    pos = _iota((cc, 1), 0).astype(F32)
    for d, (q_ref, k_ref, v_ref, o_ref) in enumerate(((qf_ref, kf_ref, vf_ref, of_ref),
                                                      (qb_ref, kb_ref, vb_ref, ob_ref))):
        reverse = d == 1
        lg = -jnp.exp(dec_ref[d])
        steps_in = (cc - pos) if reverse else (pos + 1.0)
        steps_out = pos if reverse else (cc - 1.0 - pos)
        q = q_ref[...]
        k = k_ref[...] * (RET_DH ** -0.5)
        qd = (q * jnp.exp(steps_in * lg)).astype(BF16)
        kd = (k * jnp.exp(steps_out * lg)).astype(BF16)
        qb, kb, vb = q.astype(BF16), k.astype(BF16), v_ref[...].astype(BF16)
        gain = jnp.exp(cc * lg)
        dist = (c_i - r_i) if reverse else (r_i - c_i)
        keep = dist >= 0
        dist_f = jnp.where(keep, dist, 0).astype(F32)
        for h in range(N_HEADS):
            hs = slice(h * RET_DH, (h + 1) * RET_DH)
            decay = jnp.where(keep, jnp.exp(dist_f * lg[:, h * RET_DH:h * RET_DH + 1]), 0.0)
            sc = (_dot_nt(qb[:, hs], kb[:, hs]) * decay).astype(BF16)
            st = st_ref[d, h]
            o_ref[:, hs] = _dot(sc, vb[:, hs]) + _dot_nt(qd[:, hs], st.astype(BF16))
            st_ref[d, h] = st * gain[:, hs] + _dot_tn(vb[:, hs], kd[:, hs])


def _retention(q, k, v, dec, geo):
    nt, w = q.shape
    cc = RET_CHUNK
    fwd, bwd, steps = _scan_maps(geo, cc)
    blk = lambda m: pl.BlockSpec((cc, w), m)
    return pl.pallas_call(
        _ret_kernel,
        grid=(geo["batch"], steps),
        in_specs=[blk(fwd), blk(fwd), blk(fwd), blk(bwd), blk(bwd), blk(bwd),
                  pl.BlockSpec((2, 1, w), lambda b, i: (0, 0, 0))],
        out_specs=[blk(fwd), blk(bwd)],
        out_shape=[jax.ShapeDtypeStruct((nt, w), F32)] * 2,
        scratch_shapes=[pltpu.VMEM((2, N_HEADS, RET_DH, RET_DH), F32)],
        compiler_params=_cparams(("arbitrary", "arbitrary")),
        name="recurrence_ret",
    )(q, k, v, q, k, v, dec)


def _gla_direction(q, k, v, la, st_ref, o_ref, *, reverse):
    cc, wk = q.shape
    wv = v.shape[1]
    r_i, c_i = _iota((cc, cc), 0), _iota((cc, cc), 1)
    incl = (c_i >= r_i) if reverse else (c_i <= r_i)
    cum = _dot_exact(jnp.where(incl, 1.0, 0.0), la)
    cum_end = cum[0:1] if reverse else cum[cc - 1:cc]

    qd = (q * jnp.exp(cum)).astype(BF16)
    kd = (k * jnp.exp(cum_end - cum)).astype(BF16)
    vb = v.astype(BF16)
    st = st_ref[...]
    inter = _dot_nt(qd, st.astype(BF16))
    same_head = (_iota((wv, wk), 0) // GLA_DV) == (_iota((wv, wk), 1) // GLA_DK)
    st_ref[...] = st * jnp.exp(cum_end) + jnp.where(same_head, _dot_tn(vb, kd), 0.0)

    sub = REC_SUB
    hs = N_HEADS * sub
    q_own = (_iota((hs, wk), 0) // sub) == (_iota((hs, wk), 1) // GLA_DK)
    for j in range(cc // sub):
        r0, r1 = j * sub, (j + 1) * sub
        ka, kb = (r0, cc) if reverse else (0, r1)
        mid = cum[r0 + sub // 2:r0 + sub // 2 + 1]
        qj = q[r0:r1] * jnp.exp(jnp.minimum(cum[r0:r1] - mid, EXP_CLAMP))
        kj = (k[ka:kb] * jnp.exp(jnp.minimum(mid - cum[ka:kb], EXP_CLAMP))).astype(BF16)
        q_stack = jnp.where(q_own, jnp.concatenate([qj] * N_HEADS, axis=0), 0.0).astype(BF16)
        rows = r0 + (_iota((hs, kb - ka), 0) % sub)
        cols = ka + _iota((hs, kb - ka), 1)
        keep = (cols >= rows) if reverse else (cols <= rows)
        sc = jnp.where(keep, _dot_nt(q_stack, kj), 0.0).astype(BF16)
        full = _dot(sc, vb[ka:kb])
        lane_head = _iota((sub, wv), 1) // GLA_DV
        oj = inter[r0:r1]
        for h in range(N_HEADS):
            oj = oj + jnp.where(lane_head == h, full[h * sub:(h + 1) * sub], 0.0)
        o_ref[r0:r1, :] = oj


def _gla_kernel(*refs, batch):
    per_batch = refs[:8 * batch]
    gw_ref, gb_ref, of_ref, ob_ref, st_ref = refs[8 * batch:]

    @pl.when(pl.program_id(0) == 0)
    def _():
        st_ref[...] = jnp.zeros_like(st_ref)

    for b in range(batch):
        qf_ref, kf_ref, vf_ref, lf_ref, qb_ref, kb_ref, vb_ref, lb_ref = per_batch[8 * b:8 * b + 8]
        for d, (q_ref, k_ref, v_ref, l_ref, o_ref) in enumerate(((qf_ref, kf_ref, vf_ref, lf_ref, of_ref),
                                                                 (qb_ref, kb_ref, vb_ref, lb_ref, ob_ref))):
            logit = _dot_exact(l_ref[...], gw_ref[d]) + gb_ref[d]
            la = (jnp.minimum(logit, 0.0) - jnp.log(1.0 + jnp.exp(-jnp.abs(logit)))) * (1.0 / GLA_GATE_NORM)
            _gla_direction(q_ref[...] * (GLA_DK ** -0.5), k_ref[...], v_ref[...], la, st_ref.at[b, d],
                           o_ref.at[b], reverse=(d == 1))


def _gla(q, k, v, glr, gw, gb, geo):
    batch, seq, ctx = geo["batch"], geo["seq"], geo["ctx"]
    wq, wv = q.shape[1], v.shape[1]
    cc = REC_CHUNK
    n_cc, n_lc = ctx // cc, seq // cc
    ctx0 = (batch * seq) // cc
    in_specs, args = [], []
    for b in range(batch):
        fwd = lambda i, b=b: (jnp.where(i < n_cc, ctx0 + b * n_cc + i, b * n_lc + i - n_cc), 0)
        bwd = lambda i, b=b: (jnp.where(i < n_cc, ctx0 + b * n_cc + (n_cc - 1 - i),
                                        b * n_lc + (n_lc - 1 - (i - n_cc))), 0)
        for m in (fwd, bwd):
            in_specs += [pl.BlockSpec((cc, wq), m), pl.BlockSpec((cc, wq), m), pl.BlockSpec((cc, wv), m),
                         pl.BlockSpec((cc, LANES), m)]
            args += [q, k, v, glr]
    in_specs += [pl.BlockSpec((2, LANES, wq), lambda i: (0, 0, 0)), pl.BlockSpec((2, 1, wq), lambda i: (0, 0, 0))]
    args += [gw, gb]
    steps = n_cc + n_lc
    out_blk = lambda m: pl.BlockSpec((batch, cc, wv), m)
    return pl.pallas_call(
        functools.partial(_gla_kernel, batch=batch),
        grid=(steps,),
        in_specs=in_specs,
        out_specs=[out_blk(lambda i: (0, i, 0)),
                   out_blk(lambda i: (0, jnp.where(i < n_cc, n_cc - 1 - i, steps - 1 - (i - n_cc)), 0))],
        out_shape=[jax.ShapeDtypeStruct((batch, ctx + seq, wv), F32)] * 2,
        scratch_shapes=[pltpu.VMEM((batch, 2, wv, wq), F32)],
        compiler_params=_cparams(("arbitrary",)),
        name="recurrence_gla",
    )(*args)


def _mixout_kernel(x_ref, mod_ref, at_ref, hy_ref, rf_ref, rb_ref, cg_ref, gf_ref, gb_ref, dr_ref,
                   seg_ref, rg_ref, gg_ref, wo_ref, g1_ref, g2_ref, rwh_ref, rwl_ref, rbias_ref,
                   x1_ref, h2_ref, ti_ref, tw_ref, rk_ref, cnt_ref):
    @pl.when(pl.program_id(0) == 0)
    def _():
        cnt_ref[...] = jnp.zeros_like(cnt_ref)

    def head_norm(o):
        ms = _dot((o * o).astype(BF16), seg_ref[...]) * (1.0 / RET_DH)
        return o * lax.rsqrt(ms + EPS)

    a = jnp.concatenate([at_ref[h] for h in range(N_HEADS)], axis=-1)
    cg = cg_ref[...]
    dr = dr_ref[...]
    rt = head_norm(rf_ref[...] + rb_ref[...]) * rg_ref[...] * (cg * _sigmoid(cg))
    gl = head_norm(gf_ref[...] + gb_ref[...]) * gg_ref[...] * (dr * _sigmoid(dr))
    cat = jnp.concatenate([a, hy_ref[...], rt, gl], axis=-1).astype(BF16)
    y = _dot(cat, wo_ref[...])
    x1 = x_ref[...] + mod_ref[2:3, :] * (_rms(y) * g1_ref[...])
    x1_ref[...] = x1
    h2 = _rms(x1) * g2_ref[...] * (1.0 + mod_ref[4:5, :]) + mod_ref[3:4, :]
    h2_ref[...] = h2

    h_hi = h2.astype(BF16)
    h_lo = (h2 - h_hi.astype(F32)).astype(BF16)
    vals = (_dot(h_hi, rwh_ref[...]) + _dot(h_lo, rwh_ref[...]) + _dot(h_hi, rwl_ref[...])
            + rbias_ref[...])
    tm = vals.shape[0]
    lane = lax.broadcasted_iota(jnp.int32, vals.shape, 1)
    idx_out = jnp.zeros(vals.shape, jnp.int32)
    w_out = jnp.zeros(vals.shape, F32)
    top = None
    den = 0.0
    picks = []
    for r in range(TOP_K):
        m = jnp.max(vals, axis=-1, keepdims=True)
        idx = jnp.min(jnp.where(vals == m, lane, LANES), axis=-1, keepdims=True)
        hit = lane == idx
        vals = jnp.where(hit, -jnp.inf, vals)
        top = m if top is None else top
        e = jnp.exp(m - top)
        den = den + e
        picks.append((idx, e, hit))
    inv = 1.0 / den
    for r, (idx, e, _) in enumerate(picks):
        idx_out = jnp.where(lane == r, idx, idx_out)
        w_out = jnp.where(lane == r, e * inv, w_out)
    ti_ref[...] = idx_out
    tw_ref[...] = w_out

    chosen = functools.reduce(jnp.logical_or, [hit for _, _, hit in picks])
    chosen_f = jnp.where(chosen, 1.0, 0.0)
    earlier = (lax.broadcasted_iota(jnp.int32, (tm, tm), 1) < lax.broadcasted_iota(jnp.int32, (tm, tm), 0))
    before = _dot(jnp.where(earlier, 1.0, 0.0).astype(BF16), chosen_f.astype(BF16)) + cnt_ref[...]
    rk_out = jnp.zeros(vals.shape, jnp.int32)
    for r, (_, _, hit) in enumerate(picks):
        rank = jnp.sum(jnp.where(hit, before, 0.0), axis=-1, keepdims=True)
        rk_out = jnp.where(lane == r, rank.astype(jnp.int32), rk_out)
    rk_ref[...] = rk_out
    cnt_ref[...] = cnt_ref[...] + jnp.sum(chosen_f, axis=0, keepdims=True)


def _mix_out(n_tiles, x, mod, at, hy, rf, rb, cg, gf, gb, dr, seg, rg, gg, wo, g1, g2, rw, rbias, geo):
    rw_hi = rw.astype(BF16)
    rw_lo = (rw - rw_hi.astype(F32)).astype(BF16)
    nt, d = x.shape
    tm = ROW_TILE
    lat_tiles, per_batch, batch = geo["lat_tiles"], geo["tiles_per_batch"], geo["batch"]

    def mod_map(i):
        return (jnp.where(i < lat_tiles, i // per_batch, batch), 0, 0)

    row = lambda i: (i, 0)
    const = lambda i: (0, 0)
    g_blk = pl.BlockSpec((tm, GROUP_W), row)
    ctx_tiles = geo["ctx"] // tm

    def scan_map(i):
        c = i - lat_tiles
        return (jnp.where(i < lat_tiles, i // per_batch, c // ctx_tiles),
                jnp.where(i < lat_tiles, ctx_tiles + i % per_batch, c % ctx_tiles), 0)

    s_blk = pl.BlockSpec((None, tm, GROUP_W), scan_map)
    return pl.pallas_call(
        _mixout_kernel,
        grid=(n_tiles,),
        in_specs=[pl.BlockSpec((tm, d), row), pl.BlockSpec((None, 6, d), mod_map),
                  pl.BlockSpec((N_HEADS, tm, DIFF_DV), lambda i: (0, i, 0)),
                  g_blk, g_blk, g_blk, g_blk, s_blk, s_blk, g_blk,
                  pl.BlockSpec((GROUP_W, GROUP_W), const), pl.BlockSpec((1, GROUP_W), const),
                  pl.BlockSpec((1, GROUP_W), const),
                  pl.BlockSpec((d, d), const, pipeline_mode=pl.Buffered(1)),
                  pl.BlockSpec((1, d), const), pl.BlockSpec((1, d), const),
                  pl.BlockSpec((d, LANES), const), pl.BlockSpec((d, LANES), const), pl.BlockSpec((1, LANES), const)],
        out_specs=[pl.BlockSpec((tm, d), row), pl.BlockSpec((tm, d), row),
                   pl.BlockSpec((tm, LANES), row), pl.BlockSpec((tm, LANES), row),
                   pl.BlockSpec((tm, LANES), row), pl.BlockSpec((1, LANES), const)],
        out_shape=[jax.ShapeDtypeStruct((nt, d), F32), jax.ShapeDtypeStruct((nt, d), F32),
                   jax.ShapeDtypeStruct((nt, LANES), jnp.int32), jax.ShapeDtypeStruct((nt, LANES), F32),
                   jax.ShapeDtypeStruct((nt, LANES), jnp.int32), jax.ShapeDtypeStruct((1, LANES), F32)],
        compiler_params=_cparams(("arbitrary",)),
        name="mix_out_router",
    )(x, mod, at, hy, rf, rb, cg, gf, gb, dr, seg.astype(BF16), rg, gg, wo, g1, g2, rw_hi, rw_lo, rbias)


def _expert_kernel(te_ref, na_ref, wres_ref, xs_ref, w1_ref, b1_ref, w2_ref, b2_ref, perm_ref, ys_ref, w1s, w2s):
    i = pl.program_id(0)
    active = i < na_ref[0]
    fresh = jnp.logical_or(i == 0, te_ref[i] != te_ref[jnp.maximum(i - 1, 0)])
    n_groups = w1_ref.shape[1] // (2 * LANES)

    @pl.when(jnp.logical_and(active, fresh))
    def _():
        for c in range(n_groups):
            cols = slice(c * 2 * LANES, (c + 1) * 2 * LANES)
            w1s[:, cols] = _dot(w1_ref[:, cols].astype(BF16), perm_ref[...]).astype(BF16)
        w2s[...] = w2_ref[...].astype(BF16)

    @pl.when(active)
    def _():
        u = _dot(xs_ref[...].astype(BF16), w1s[...]) + b1_ref[...]
        acts = []
        for c in range(n_groups):
            glu = jnp.minimum(u[:, c * 2 * LANES:c * 2 * LANES + LANES], SWIGLU_LIMIT)
            lin = jnp.clip(u[:, c * 2 * LANES + LANES:(c + 1) * 2 * LANES], -SWIGLU_LIMIT, SWIGLU_LIMIT)
            acts.append((glu * _sigmoid(SWIGLU_ALPHA * glu) * (lin + 1.0)).astype(BF16))
        ys_ref[...] = _dot(jnp.concatenate(acts, axis=-1), w2s[...]) + b2_ref[...]

    @pl.when(jnp.logical_not(active))
    def _():
        ys_ref[...] = jnp.zeros_like(ys_ref)


def _expert_ffn(tile_e, n_active, xs, w1, b1, w2, b2, layer):
    p, d = xs.shape
    tm = MOE_TILE
    de2 = w1.shape[3]
    de = de2 // 2
    sel = np.zeros((2 * LANES, 2 * LANES), np.float32)
    sel[2 * np.arange(LANES), np.arange(LANES)] = 1.0
    sel[2 * np.arange(LANES) + 1, LANES + np.arange(LANES)] = 1.0
    b1g = b1.reshape(N_EXPERTS, de2 // (2 * LANES), LANES, 2).transpose(0, 1, 3, 2).reshape(N_EXPERTS, 1, de2)
    n_tiles = p // tm
    tiles = jnp.arange(n_tiles, dtype=jnp.int32)
    live = tiles < n_active[0]
    te_live = jnp.where(live, tile_e, jnp.max(jnp.where(live, tile_e, 0)))
    first = jnp.concatenate([jnp.ones((1,), bool), te_live[1:] != te_live[:-1]])
    later = jnp.where(te_live[None, :] > te_live[:, None], te_live[None, :], N_EXPERTS)
    following = jnp.min(later, axis=1)
    w_res = jnp.where(first | (following == N_EXPERTS), te_live, following).astype(jnp.int32)
    wmap = lambda i, te, na, wr: (te[i], 0, 0)
    lmap = lambda i, te, na, wr: (layer, wr[i], 0, 0)
    xmap = lambda i, te, na, wr: (jnp.minimum(i, jnp.maximum(na[0] - 1, 0)), 0)
    grid_spec = pltpu.PrefetchScalarGridSpec(
        num_scalar_prefetch=3,
        grid=(p // tm,),
        in_specs=[pl.BlockSpec((tm, d), xmap),
                  pl.BlockSpec((None, None, d, de2), lmap), pl.BlockSpec((None, 1, de2), wmap),
                  pl.BlockSpec((None, None, de, d), lmap), pl.BlockSpec((None, 1, d), wmap),
                  pl.BlockSpec((2 * LANES, 2 * LANES), lambda i, te, na, wr: (0, 0))],
        out_specs=pl.BlockSpec((tm, d), lambda i, te, na, wr: (i, 0)),
        scratch_shapes=[pltpu.VMEM((d, de2), BF16), pltpu.VMEM((de, d), BF16)],
    )
    return pl.pallas_call(
        _expert_kernel,
        grid_spec=grid_spec,
        out_shape=jax.ShapeDtypeStruct((p, d), F32),
        compiler_params=_cparams(("arbitrary",)),
        name="expert_ffn",
    )(te_live, n_active, w_res, xs, w1, b1g, w2, b2.reshape(N_EXPERTS, 1, d), jnp.asarray(sel, BF16))


def _route_plan(topi, rank, counts, tm):
    n, k = topi.shape
    padded = ((counts + tm - 1) // tm) * tm
    pend = jnp.cumsum(padded)
    pstart = pend - padded
    experts = jnp.arange(N_EXPERTS, dtype=jnp.int32)
    start = jnp.sum(jnp.where(topi[:, :, None] == experts, pstart, 0), axis=-1)
    dest = (start + rank).astype(jnp.int32)
    p = n * k + N_EXPERTS * tm
    tile_start = jnp.arange(p // tm, dtype=jnp.int32) * tm
    tile_e = jnp.minimum(jnp.sum((tile_start[:, None] >= pend[None, :]).astype(jnp.int32), axis=1), N_EXPERTS - 1)
    n_active = (pend[-1:] // tm).astype(jnp.int32)
    return dest, tile_e, n_active, p, jnp.stack([pstart, pend]).astype(jnp.int32)


def _dispatch_kernel(bounds_ref, dest_ref, x_ref, out_ref, zeros, sem, zsem, *, tb):
    n_rows = tb * TOP_K

    @pl.when(pl.program_id(0) == 0)
    def _():
        zeros[...] = jnp.zeros_like(zeros)

        def fill(e):
            start = pl.multiple_of(bounds_ref[1, e] - MOE_TILE, MOE_TILE)
            return pltpu.make_async_copy(zeros, out_ref.at[pl.ds(start, MOE_TILE)], zsem)

        for e in range(N_EXPERTS):
            @pl.when(bounds_ref[1, e] > bounds_ref[0, e])
            def _():
                fill(e).start()
        for e in range(N_EXPERTS):
            @pl.when(bounds_ref[1, e] > bounds_ref[0, e])
            def _():
                fill(e).wait()

    def issue(t, carry):
        for k in range(TOP_K):
            pltpu.make_async_copy(x_ref.at[pl.ds(t, 1)], out_ref.at[pl.ds(dest_ref[0, t * TOP_K + k], 1)],
                                  sem).start(priority=k % 2)
        return carry

    lax.fori_loop(0, tb, issue, 0, unroll=4)
    pltpu.make_async_copy(out_ref.at[pl.ds(0, n_rows)], out_ref.at[pl.ds(0, n_rows)], sem).wait()


def _dispatch(h2, dest, bounds, p_rows, n_tok):
    d = h2.shape[1]
    tb = ROW_TILE
    steps = n_tok // tb
    return pl.pallas_call(
        functools.partial(_dispatch_kernel, tb=tb),
        grid=(steps,),
        in_specs=[pl.BlockSpec(memory_space=pltpu.SMEM),
                  pl.BlockSpec((None, 1, tb * TOP_K), lambda i: (i, 0, 0), memory_space=pltpu.SMEM),
                  pl.BlockSpec((tb, d), lambda i: (i, 0))],
        out_specs=pl.BlockSpec(memory_space=pl.ANY),
        out_shape=jax.ShapeDtypeStruct((p_rows, d), F32),
        scratch_shapes=[pltpu.VMEM((MOE_TILE, d), F32), pltpu.SemaphoreType.DMA(()), pltpu.SemaphoreType.DMA(())],
        compiler_params=pltpu.CompilerParams(dimension_semantics=("arbitrary",), has_side_effects=True),
        name="moe_dispatch",
    )(bounds, dest.reshape(steps, 1, tb * TOP_K), h2)


def _combine_kernel(dest_ref, x_ref, tw_ref, mod_ref, g_ref, ys_ref, o_ref, buf, sem, *, tb):
    n_rows = tb * TOP_K

    def issue(pair, carry):
        for half in range(2):
            r = 2 * pair + half
            pltpu.make_async_copy(ys_ref.at[pl.ds(dest_ref[0, r], 1)], buf.at[pl.ds(r, 1)], sem).start(priority=half)
        return carry

    lax.fori_loop(0, n_rows // 2, issue, 0, unroll=4)
    pltpu.make_async_copy(ys_ref.at[pl.ds(0, n_rows)], buf, sem).wait()
    y = None
    for k in range(TOP_K):
        t = buf[k * tb:(k + 1) * tb, :] * tw_ref[:, k:k + 1]
        y = t if y is None else y + t
    o_ref[...] = x_ref[...] + mod_ref[5:6, :] * (_rms(y) * g_ref[...])


def _combine_residual(n_tiles, x, ys, dest, topw, mod, g, geo):
    nt, d = x.shape
    tb = ROW_TILE
    lat_tiles, per_batch, batch = geo["lat_tiles"], geo["tiles_per_batch"], geo["batch"]

    def mod_map(i):
        return (jnp.where(i < lat_tiles, i // per_batch, batch), 0, 0)

    row = lambda i: (i, 0)
    slot_major = jnp.swapaxes(dest.reshape(n_tiles, tb, TOP_K), 1, 2).reshape(n_tiles, 1, tb * TOP_K)
    return pl.pallas_call(
        functools.partial(_combine_kernel, tb=tb),
        grid=(n_tiles,),
        in_specs=[pl.BlockSpec((None, 1, tb * TOP_K), lambda i: (i, 0, 0), memory_space=pltpu.SMEM),
                  pl.BlockSpec((tb, d), row), pl.BlockSpec((tb, LANES), row),
                  pl.BlockSpec((None, 6, d), mod_map), pl.BlockSpec((1, d), lambda i: (0, 0)),
                  pl.BlockSpec(memory_space=pl.ANY)],
        out_specs=pl.BlockSpec((tb, d), row),
        out_shape=jax.ShapeDtypeStruct((n_tiles * tb, d), F32),
        scratch_shapes=[pltpu.VMEM((tb * TOP_K, d), F32), pltpu.SemaphoreType.DMA(())],
        compiler_params=_cparams(("arbitrary",)),
        name="combine_residual",
    )(slot_major, x, topw, mod, g, ys)


def _rope_tables(n_tok, dim, reps, pad_rows):
    rows = n_tok // GRID_W
    row = jnp.repeat(jnp.arange(rows, dtype=F32), GRID_W)
    col = jnp.tile(jnp.arange(GRID_W, dtype=F32), rows)
    quarter = dim // 4
    inv = ROPE_BASE ** (-jnp.arange(quarter, dtype=F32) / quarter)
    ar = row[:, None] * inv[None]
    ac = col[:, None] * inv[None]
    ang = jnp.concatenate([ar, ar, ac, ac], axis=-1)
    cos = jnp.concatenate([jnp.tile(jnp.cos(ang), (1, reps)), jnp.ones((pad_rows, dim * reps), F32)])
    sin = jnp.concatenate([jnp.tile(jnp.sin(ang), (1, reps)), jnp.zeros((pad_rows, dim * reps), F32)])
    return cos, sin


def _widen_w_in(w):
    return jnp.pad(w, ((0, 0), (0, _IN_COLS - w.shape[1]))).astype(BF16)


def kernel(x, c, ctx, c_ctx, w_mod, b_mod, norm_g, w_in, w_out, diff_lambda, diff_subln_g, hy_short_w, hy_short_b, hy_w1, hy_b1, hy_w2, hy_b2, hy_w3, hy_freq, hy_bias, ret_decay, ret_norm_g, gla_gate_w, gla_gate_b, gla_norm_g, router_w, router_b, exp_w1, exp_b1, exp_w2, exp_b2):
    batch, seq, d = x.shape
    n_ctx = ctx.shape[1]
    depth = w_mod.shape[0]
    n_lat_rows, n_ctx_rows = batch * seq, batch * n_ctx
    nt = n_lat_rows + n_ctx_rows
    assert d == D_MODEL and seq % ATT_Q_TILE == 0 and n_ctx % ROW_TILE == 0 and seq % n_ctx == 0
    geo = dict(batch=batch, seq=seq, ctx=n_ctx, lat_tiles=n_lat_rows // ROW_TILE,
               tiles_per_batch=seq // ROW_TILE)

    xs = jnp.concatenate([x.reshape(n_lat_rows, d), ctx.reshape(n_ctx_rows, d)], axis=0)
    mod_rows = 8
    cc = jnp.zeros((mod_rows, d), F32).at[:batch].set(c).at[batch].set(c_ctx)
    mod_all = _modulation(cc, w_mod, b_mod).reshape(depth, mod_rows, 6, d)

    rope = (*_rope_tables(seq, DIFF_DQK, GROUP_W // DIFF_DQK, ROW_TILE),
            *_rope_tables(seq, RET_DH, GROUP_W // RET_DH, ROW_TILE))
    dft_lat = _dft_tables(seq)
    dft_ctx = _dft_tables(n_ctx)
    seg = jnp.asarray(np.kron(np.eye(N_HEADS, dtype=np.float32), np.ones((RET_DH, RET_DH), np.float32)))

    for l in range(depth):
        need_ctx = l < depth - 1
        lam_init = 0.8 - 0.6 * math.exp(-0.3 * l)
        mod = mod_all[l]
        (aq, ak, av, bu, cq, ck, cv, cg, dq, dk, dv, dr, dl) = _in_projection(
            xs, mod, norm_g[l, 0].reshape(1, d), _widen_w_in(w_in[l]), rope, geo)

        sub_g = diff_subln_g[l].reshape(1, DIFF_DV)
        at = _diff_attention(diff_lambda[l], sub_g, aq, ak, av, geo, lam_init)
        if need_ctx:
            at = _diff_attention(diff_lambda[l], sub_g, aq, ak, av, geo, lam_init, prev=at)

        def hyena(n, row_blk0, tables, out_blk0, prev):
            e, dd, nyqf = _hyena_filter(n, hy_w1[l], hy_b1[l], hy_w2[l], hy_b2[l], hy_w3[l], hy_freq[l])
            zb, x0c, zf, nyqz = _hyena_gate(bu, hy_short_w[l], hy_short_b[l], n, batch, row_blk0)
            return _hyena_conv(tables, zb, e, dd, x0c, zf, nyqz, nyqf, hy_bias[l], n, batch, nt, out_blk0, prev)

        hy = hyena(seq, 0, dft_lat, 0, None)
        if need_ctx:
            hy = hyena(n_ctx, n_lat_rows // n_ctx, dft_ctx, n_lat_rows // min(512, n_ctx), hy)

        dec = jnp.repeat(ret_decay[l], RET_DH, axis=-1).reshape(2, 1, N_HEADS * RET_DH)
        rf, rb = _retention(cq, ck, cv, dec, geo)

        gw = jnp.zeros((2, LANES, N_HEADS * GLA_DK), F32)
        gw = gw.at[0, :GLA_RANK].set(gla_gate_w[l, 0]).at[1, GLA_RANK:2 * GLA_RANK].set(gla_gate_w[l, 1])
        gbias = gla_gate_b[l].reshape(2, 1, N_HEADS * GLA_DK)
        gf, gb = _gla(dq, dk, dv, dl, gw, gbias, geo)

        n_tiles = nt // ROW_TILE if need_ctx else n_lat_rows // ROW_TILE
        rw = jnp.pad(router_w[l], ((0, 0), (0, LANES - N_EXPERTS)))
        rbias = jnp.pad(router_b[l], (0, LANES - N_EXPERTS), constant_values=-jnp.inf).reshape(1, LANES)
        x1, h2, topi, topw, rank, cnt = _mix_out(
            n_tiles, xs, mod, at, hy, rf, rb, cg, gf, gb, dr, seg,
            ret_norm_g[l].reshape(1, GROUP_W), jnp.tile(gla_norm_g[l], N_HEADS).reshape(1, GROUP_W),
            w_out[l].astype(BF16), norm_g[l, 1].reshape(1, d), norm_g[l, 2].reshape(1, d), rw, rbias, geo)

        n_tok = n_tiles * ROW_TILE
        dest, tile_e, n_active, p_rows, bounds = _route_plan(
            topi[:n_tok, :TOP_K], rank[:n_tok, :TOP_K], cnt[0, :N_EXPERTS].astype(jnp.int32), MOE_TILE)
        dispatched = _dispatch(h2, dest, bounds, p_rows, n_tok)
        ys = _expert_ffn(tile_e, n_active, dispatched, exp_w1, exp_b1[l], exp_w2, exp_b2[l], l)
        xs = _combine_residual(n_tiles, x1, ys, dest, topw, mod, norm_g[l, 3].reshape(1, d), geo)

    return xs[:n_lat_rows].reshape(batch, seq, d)
```
